```python
import math
import jax
import jax.numpy as jnp
from jax import lax
import numpy as np

D_MODEL = 4096
BATCH = 2
SEQ = 4096
DEPTH = 2

CHUNK = 64
Q_BLOCK = 128
SSD_HEADS = 64
SSD_HEAD_DIM = 64
SSD_INNER = SSD_HEADS * SSD_HEAD_DIM
SSD_GROUPS = 8
SSD_STATE = 128
SSD_CONV = 4
SSD_XBC = SSD_INNER + 2 * SSD_GROUPS * SSD_STATE
DIFF_HEADS = 32
DIFF_QK_DIM = 64
DIFF_V_DIM = 2 * DIFF_QK_DIM
DIFF_WIDTH = DIFF_HEADS * DIFF_V_DIM
ROPE_THETA = 10000.0
RET_HEADS = 16
RET_QK_DIM = D_MODEL // RET_HEADS
RET_V_DIM = 2 * RET_QK_DIM
RET_QK_WIDTH = RET_HEADS * RET_QK_DIM
RET_V_WIDTH = RET_HEADS * RET_V_DIM
FFN_DIM = 11008
N_EXPERTS = 8
TOP_K = 2
EXPERT_DIM = 4096
EVEN_IN = SSD_INNER + SSD_XBC + SSD_HEADS + 3 * DIFF_WIDTH
EVEN_MIX = SSD_INNER + DIFF_WIDTH
ODD_IN = 2 * RET_QK_WIDTH + 2 * RET_V_WIDTH
N_EVEN = (DEPTH + 1) // 2
N_ODD = DEPTH // 2
DEEPNORM_ALPHA = (2 * DEPTH) ** 0.25
DEEPNORM_BETA = (8 * DEPTH) ** -0.25
LN_EPS = 1e-5
RMS_EPS = 1e-5

kernel_name = 'hybrid_ssd_diffattn_retention_moe_deepnorm'


def _standardize(x):
    xf = x.astype(jnp.float32)
    mu = jnp.mean(xf, axis=-1, keepdims=True)
    var = jnp.mean(jnp.square(xf - mu), axis=-1, keepdims=True)
    return (xf - mu) * lax.rsqrt(var + LN_EPS)


def layer_norm(x, g, b):
    return _standardize(x).astype(x.dtype) * g + b


def rms_norm(x, w):
    xf = x.astype(jnp.float32)
    y = xf * lax.rsqrt(jnp.mean(xf * xf, axis=-1, keepdims=True) + RMS_EPS)
    return y.astype(x.dtype) * w


def rotary(x, inv_freq):
    seq, d = x.shape[1], x.shape[-1]
    ang = jnp.arange(seq, dtype=jnp.float32)[:, None] * inv_freq[None, :]
    shape = (1, seq) + (1,) * (x.ndim - 3) + (d // 2,)
    cos = jnp.cos(ang).reshape(shape).astype(x.dtype)
    sin = jnp.sin(ang).reshape(shape).astype(x.dtype)
    x1, x2 = x[..., : d // 2], x[..., d // 2:]
    return jnp.concatenate([x1 * cos - x2 * sin, x2 * cos + x1 * sin], axis=-1)


def causal_depthwise_conv(x, w, b):
    k = w.shape[0]
    xp = jnp.pad(x, ((0, 0), (k - 1, 0), (0, 0)))
    y = lax.conv_general_dilated(xp, w[:, None, :], window_strides=(1,), padding='VALID',
                                 dimension_numbers=('NWC', 'WIO', 'NWC'),
                                 feature_group_count=x.shape[-1])
    return y + b


def segsum(a):
    t = a.shape[-1]
    cs = jnp.cumsum(a, axis=-1)
    diff = cs[..., :, None] - cs[..., None, :]
    mask = jnp.tril(jnp.ones((t, t), dtype=bool))
    return jnp.where(mask, diff, -jnp.inf)


def ssd_chunked(x, a, b, c):
    bsz, seq, nh, p = x.shape
    g, n = b.shape[2], b.shape[3]
    r = nh // g
    nc, L = seq // CHUNK, CHUNK
    x = x.reshape(bsz, nc, L, g, r, p)
    b = b.reshape(bsz, nc, L, g, n)
    c = c.reshape(bsz, nc, L, g, n)
    a = a.reshape(bsz, nc, L, g, r).transpose(0, 3, 4, 1, 2)
    a_cs = jnp.cumsum(a, axis=-1)
    decay = jnp.exp(segsum(a))
    cb = jnp.einsum('bclgn,bcsgn->bgcls', c, b)
    y_diag = jnp.einsum('bgrcls,bcsgrp->bclgrp', cb[:, :, None] * decay, x)
    decay_states = jnp.exp(a_cs[..., -1:] - a_cs).transpose(0, 3, 4, 1, 2)
    states = jnp.einsum('bclgn,bclgrp->bcgrpn', b, x * decay_states[..., None])
    states = jnp.concatenate([jnp.zeros_like(states[:, :1]), states], axis=1)
    chunk_a = jnp.pad(a_cs[..., -1], ((0, 0), (0, 0), (0, 0), (1, 0)))
    decay_chunk = jnp.exp(segsum(chunk_a))
    states = jnp.einsum('bgrzc,bcgrpn->bzgrpn', decay_chunk, states)[:, :-1]
    state_decay = jnp.exp(a_cs).transpose(0, 3, 4, 1, 2)
    y_off = jnp.einsum('bclgn,bcgrpn->bclgrp', c, states) * state_decay[..., None]
    return (y_diag + y_off).reshape(bsz, seq, nh, p)


def diff_attention(q, k, v, lam):
    bsz, seq, nh, _, d = q.shape
    nb = seq // Q_BLOCK
    scale = d ** -0.5
    qb = q.reshape(bsz, nb, Q_BLOCK, nh, 2, d).transpose(1, 0, 2, 3, 4, 5)
    k_chunk = jnp.arange(seq) // CHUNK
    lam = lam.astype(jnp.float32)

    def block(args):
        qi, i = args
        s = jnp.einsum('bqhcd,bkhcd->bhcqk', qi, k).astype(jnp.float32) * scale
        q_chunk = (i * Q_BLOCK + jnp.arange(Q_BLOCK)) // CHUNK
        mask = k_chunk[None, :] <= q_chunk[:, None]
        p = jax.nn.softmax(jnp.where(mask, s, -jnp.inf), axis=-1)
        attn = p[:, :, 0] - lam * p[:, :, 1]
        return jnp.einsum('bhqk,bkhe->bqhe', attn.astype(v.dtype), v)

    out = lax.map(block, (qb, jnp.arange(nb)))
    return out.transpose(1, 0, 2, 3, 4).reshape(bsz, seq, nh, 2 * d)


def mixer_ssd_diff(h, w_in, conv_w, conv_b, dt_bias, a_log, d_skip, ssd_norm_w,
                   lam_q1, lam_k1, lam_q2, lam_k2, diff_norm_w, w_out, lambda_init):
    bsz, seq, _ = h.shape
    f32 = jnp.float32
    cuts = [SSD_INNER, SSD_INNER + SSD_XBC, SSD_INNER + SSD_XBC + SSD_HEADS,
            SSD_INNER + SSD_XBC + SSD_HEADS + DIFF_WIDTH,
            SSD_INNER + SSD_XBC + SSD_HEADS + 2 * DIFF_WIDTH]
    z, xbc, dt, q, k, v = jnp.split(h @ w_in, cuts, axis=-1)
    xbc = jax.nn.silu(causal_depthwise_conv(xbc, conv_w, conv_b))
    xs, bs, cs = jnp.split(xbc, [SSD_INNER, SSD_INNER + SSD_GROUPS * SSD_STATE], axis=-1)
    dt = jax.nn.softplus((dt + dt_bias).astype(f32))
    a = -jnp.exp(a_log.astype(f32)) * dt
    xs = xs.reshape(bsz, seq, SSD_HEADS, SSD_HEAD_DIM)
    y = ssd_chunked(xs.astype(f32) * dt[..., None], a,
                    bs.reshape(bsz, seq, SSD_GROUPS, SSD_STATE).astype(f32),
                    cs.reshape(bsz, seq, SSD_GROUPS, SSD_STATE).astype(f32))
    y = y.astype(h.dtype) + d_skip[:, None] * xs
    y = y.reshape(bsz, seq, SSD_INNER) * jax.nn.silu(z)
    y = rms_norm(y.reshape(bsz, seq, SSD_GROUPS, SSD_INNER // SSD_GROUPS),
                 ssd_norm_w.reshape(SSD_GROUPS, SSD_INNER // SSD_GROUPS)).reshape(bsz, seq, SSD_INNER)
    inv_freq = ROPE_THETA ** (-jnp.arange(0, DIFF_QK_DIM, 2, dtype=f32) / DIFF_QK_DIM)
    q = rotary(q.reshape(bsz, seq, DIFF_HEADS, 2, DIFF_QK_DIM), inv_freq)
    k = rotary(k.reshape(bsz, seq, DIFF_HEADS, 2, DIFF_QK_DIM), inv_freq)
    v = v.reshape(bsz, seq, DIFF_HEADS, DIFF_V_DIM)
    lam = (jnp.exp(jnp.sum(lam_q1.astype(f32) * lam_k1.astype(f32)))
           - jnp.exp(jnp.sum(lam_q2.astype(f32) * lam_k2.astype(f32))) + lambda_init)
    o = diff_attention(q, k, v, lam)
    o = rms_norm(o, diff_norm_w) * (1.0 - lambda_init)
    mixed = jnp.concatenate([y, o.reshape(bsz, seq, DIFF_WIDTH)], axis=-1)
    return mixed @ w_out


def retention_chunkwise(q, k, v):
    bsz, seq, nh, dk = q.shape
    dv = v.shape[-1]
    nc, L = seq // CHUNK, CHUNK
    log_g = jnp.log1p(-(2.0 ** (-5.0 - jnp.arange(nh, dtype=jnp.float32))))
    idx = jnp.arange(L, dtype=jnp.float32)
    intra = jnp.exp(jnp.abs(idx[:, None] - idx[None, :])[None] * log_g[:, None, None])
    q_decay = jnp.exp((idx + 1.0)[:, None] * log_g[None, :])
    k_decay = jnp.exp((L - 1.0 - idx)[:, None] * log_g[None, :])
    chunk_decay = jnp.exp(L * log_g)
    qc = q.reshape(bsz, nc, L, nh, dk).transpose(1, 0, 2, 3, 4)
    kc = k.reshape(bsz, nc, L, nh, dk).transpose(1, 0, 2, 3, 4)
    vc = v.reshape(bsz, nc, L, nh, dv).transpose(1, 0, 2, 3, 4)

    def step(state, inp):
        qi, ki, vi = inp
        inner = jnp.einsum('blhd,bmhd->bhlm', qi, ki) * intra
        y = jnp.einsum('bhlm,bmhe->blhe', inner, vi)
        y = y + jnp.einsum('blhd,bhde->blhe', qi, state) * q_decay[:, :, None]
        state = state * chunk_decay[:, None, None] + jnp.einsum(
            'blhd,blhe->bhde', ki * k_decay[:, :, None], vi)
        return state, y

    state0 = jnp.zeros((bsz, nh, dk, dv), jnp.float32)
    _, ys = lax.scan(step, state0, (qc, kc, vc))
    return ys.transpose(1, 0, 2, 3, 4).reshape(bsz, seq, nh, dv)


def mixer_retention(h, w_in, w_out):
    bsz, seq, _ = h.shape
    f32 = jnp.float32
    q, k, v, g = jnp.split(h @ w_in, [RET_QK_WIDTH, 2 * RET_QK_WIDTH,
                                      2 * RET_QK_WIDTH + RET_V_WIDTH], axis=-1)
    inv_freq = 1.0 / (ROPE_THETA ** jnp.linspace(0.0, 1.0, RET_QK_DIM // 2, dtype=f32))
    q = rotary(q.reshape(bsz, seq, RET_HEADS, RET_QK_DIM), inv_freq)
    k = rotary(k.reshape(bsz, seq, RET_HEADS, RET_QK_DIM), inv_freq) * (RET_QK_DIM ** -0.5)
    v = v.reshape(bsz, seq, RET_HEADS, RET_V_DIM)
    o = retention_chunkwise(q.astype(f32), k.astype(f32), v.astype(f32))
    o = _standardize(o).astype(h.dtype).reshape(bsz, seq, RET_V_WIDTH)
    return (jax.nn.silu(g) * o) @ w_out


def swiglu(h, w_gate, w_up, w_down):
    return (jax.nn.silu(h @ w_gate) * (h @ w_up)) @ w_down


def moe_swiglu(h, w_router, w_gate, w_up, w_down):
    logits = (h @ w_router).astype(jnp.float32)
    top_v, top_i = lax.top_k(logits, TOP_K)
    top_w = jax.nn.softmax(top_v, axis=-1)
    gate = jnp.sum(jax.nn.one_hot(top_i, N_EXPERTS, dtype=jnp.float32) * top_w[..., None], axis=-2)
    out = jnp.zeros_like(h)
    for e in range(N_EXPERTS):
        out = out + gate[..., e:e + 1].astype(h.dtype) * swiglu(h, w_gate[e], w_up[e], w_down[e])
    return out


def setup_inputs(seed: int = 0) -> dict:
    key = jax.random.key(seed)
    ks = iter(jax.random.split(key, 32))
    f32 = jnp.float32

    def nrm(shape, scale):
        return jax.random.normal(next(ks), shape, f32) * scale

    def gain(shape):
        return 1.0 + 0.05 * jax.random.normal(next(ks), shape, f32)

    x = nrm((BATCH, SEQ, D_MODEL), 1.0)
    ab_w_in = nrm((N_EVEN, D_MODEL, EVEN_IN), D_MODEL ** -0.5)
    ab_conv_w = nrm((N_EVEN, SSD_CONV, SSD_XBC), SSD_CONV ** -0.5)
    ab_conv_b = nrm((N_EVEN, SSD_XBC), 0.02)
    u = jax.random.uniform(next(ks), (N_EVEN, SSD_HEADS), f32)
    dt0 = jnp.exp(u * (math.log(0.1) - math.log(0.001)) + math.log(0.001))
    ab_dt_bias = dt0 + jnp.log(-jnp.expm1(-dt0))
    ab_a_log = jnp.log(jax.random.uniform(next(ks), (N_EVEN, SSD_HEADS), f32, 1.0, 16.0))
    ab_d_skip = gain((N_EVEN, SSD_HEADS))
    ab_ssd_norm_w = gain((N_EVEN, SSD_INNER))
    ab_lam_q1 = nrm((N_EVEN, DIFF_QK_DIM), 0.1)
    ab_lam_k1 = nrm((N_EVEN, DIFF_QK_DIM), 0.1)
    ab_lam_q2 = nrm((N_EVEN, DIFF_QK_DIM), 0.1)
    ab_lam_k2 = nrm((N_EVEN, DIFF_QK_DIM), 0.1)
    ab_diff_norm_w = gain((N_EVEN, DIFF_V_DIM))
    ab_w_out = nrm((N_EVEN, EVEN_MIX, D_MODEL), EVEN_MIX ** -0.5 * DEEPNORM_BETA)
    ffn_w_gate = nrm((N_EVEN, D_MODEL, FFN_DIM), D_MODEL ** -0.5)
    ffn_w_up = nrm((N_EVEN, D_MODEL, FFN_DIM), D_MODEL ** -0.5)
    ffn_w_down = nrm((N_EVEN, FFN_DIM, D_MODEL), FFN_DIM ** -0.5 * DEEPNORM_BETA)
    c_w_in = nrm((N_ODD, D_MODEL, ODD_IN), D_MODEL ** -0.5)
    c_w_out = nrm((N_ODD, RET_V_WIDTH, D_MODEL), RET_V_WIDTH ** -0.5 * DEEPNORM_BETA)
    moe_w_router = nrm((N_ODD, D_MODEL, N_EXPERTS), D_MODEL ** -0.5)
    moe_w_gate = nrm((N_ODD, N_EXPERTS, D_MODEL, EXPERT_DIM), D_MODEL ** -0.5)
    moe_w_up = nrm((N_ODD, N_EXPERTS, D_MODEL, EXPERT_DIM), D_MODEL ** -0.5)
    moe_w_down = nrm((N_ODD, N_EXPERTS, EXPERT_DIM, D_MODEL), EXPERT_DIM ** -0.5 * DEEPNORM_BETA)
    ln_mix_g = gain((DEPTH, D_MODEL))
    ln_mix_b = nrm((DEPTH, D_MODEL), 0.02)
    ln_ffn_g = gain((DEPTH, D_MODEL))
    ln_ffn_b = nrm((DEPTH, D_MODEL), 0.02)
    return {'x': x, 'ab_w_in': ab_w_in, 'ab_conv_w': ab_conv_w, 'ab_conv_b': ab_conv_b,
            'ab_dt_bias': ab_dt_bias, 'ab_a_log': ab_a_log, 'ab_d_skip': ab_d_skip,
            'ab_ssd_norm_w': ab_ssd_norm_w, 'ab_lam_q1': ab_lam_q1, 'ab_lam_k1': ab_lam_k1,
            'ab_lam_q2': ab_lam_q2, 'ab_lam_k2': ab_lam_k2, 'ab_diff_norm_w': ab_diff_norm_w,
            'ab_w_out': ab_w_out, 'ffn_w_gate': ffn_w_gate, 'ffn_w_up': ffn_w_up,
            'ffn_w_down': ffn_w_down, 'c_w_in': c_w_in, 'c_w_out': c_w_out,
            'moe_w_router': moe_w_router, 'moe_w_gate': moe_w_gate, 'moe_w_up': moe_w_up,
            'moe_w_down': moe_w_down, 'ln_mix_g': ln_mix_g, 'ln_mix_b': ln_mix_b,
            'ln_ffn_g': ln_ffn_g, 'ln_ffn_b': ln_ffn_b}


def reference(x, ab_w_in, ab_conv_w, ab_conv_b, ab_dt_bias, ab_a_log, ab_d_skip, ab_ssd_norm_w,
              ab_lam_q1, ab_lam_k1, ab_lam_q2, ab_lam_k2, ab_diff_norm_w, ab_w_out,
              ffn_w_gate, ffn_w_up, ffn_w_down, c_w_in, c_w_out,
              moe_w_router, moe_w_gate, moe_w_up, moe_w_down,
              ln_mix_g, ln_mix_b, ln_ffn_g, ln_ffn_b):
    h = x
    for l in range(DEPTH):
        i = l // 2
        if l % 2 == 0:
            lambda_init = 0.8 - 0.6 * math.exp(-0.3 * l)
            mix = mixer_ssd_diff(h, ab_w_in[i], ab_conv_w[i], ab_conv_b[i], ab_dt_bias[i],
                                 ab_a_log[i], ab_d_skip[i], ab_ssd_norm_w[i],
                                 ab_lam_q1[i], ab_lam_k1[i], ab_lam_q2[i], ab_lam_k2[i],
                                 ab_diff_norm_w[i], ab_w_out[i], lambda_init)
        else:
            mix = mixer_retention(h, c_w_in[i], c_w_out[i])
        h = layer_norm(DEEPNORM_ALPHA * h + mix, ln_mix_g[l], ln_mix_b[l])
        if l % 2 == 0:
            ff = swiglu(h, ffn_w_gate[i], ffn_w_up[i], ffn_w_down[i])
        else:
            ff = moe_swiglu(h, moe_w_router[i], moe_w_gate[i], moe_w_up[i], moe_w_down[i])
        h = layer_norm(DEEPNORM_ALPHA * h + ff, ln_ffn_g[l], ln_ffn_b[l])
    return h
```

```python
import functools
import math

import numpy as np
import jax
import jax.numpy as jnp
from jax import lax
from jax.experimental import pallas as pl
from jax.experimental.pallas import tpu as pltpu

F32 = jnp.float32
BF16 = jnp.bfloat16
HIGHEST = lax.Precision.HIGHEST

D_MODEL = 4096
DEPTH = 2
CHUNK = 64
SSD_HEADS = 64
SSD_HEAD_DIM = 64
SSD_INNER = SSD_HEADS * SSD_HEAD_DIM
SSD_GROUPS = 8
SSD_STATE = 128
SSD_CONV = 4
SSD_BC = SSD_GROUPS * SSD_STATE
SSD_XBC = SSD_INNER + 2 * SSD_BC
DIFF_HEADS = 32
DIFF_QK_DIM = 64
DIFF_V_DIM = 2 * DIFF_QK_DIM
DIFF_WIDTH = DIFF_HEADS * DIFF_V_DIM
ROPE_THETA = 10000.0
RET_HEADS = 16
RET_QK_DIM = D_MODEL // RET_HEADS
RET_V_DIM = 2 * RET_QK_DIM
RET_QK_WIDTH = RET_HEADS * RET_QK_DIM
RET_V_WIDTH = RET_HEADS * RET_V_DIM
FFN_DIM = 11008
N_EXPERTS = 8
TOP_K = 2
EXPERT_DIM = 4096
DEEPNORM_ALPHA = (2 * DEPTH) ** 0.25
LN_EPS = 1e-5
RMS_EPS = 1e-5

LANES = 128
VMEM_LIMIT_BYTES = 56 * 1024 * 1024

SSD_BLOCK = 128
RET_BLOCK = 256
ATTN_BLOCK = 256
MOE_TM = 256


def _cparams(sem):
    return pltpu.CompilerParams(dimension_semantics=sem, vmem_limit_bytes=VMEM_LIMIT_BYTES)


def _silu(x):
    return x * (1.0 / (1.0 + jnp.exp(-x)))


def _mm_body(a_ref, w_ref, o_ref):
    o_ref[...] = jnp.dot(a_ref[...], w_ref[...], preferred_element_type=F32).astype(o_ref.dtype)


def matmul(a, w, out_dtype, tm, tn):
    m, k = a.shape
    n = w.shape[1]
    assert m % tm == 0 and n % tn == 0, (m, n, tm, tn)
    return pl.pallas_call(
        _mm_body,
        grid=(m // tm, n // tn),
        in_specs=[pl.BlockSpec((tm, k), lambda i, j: (i, 0)),
                  pl.BlockSpec((k, tn), lambda i, j: (0, j))],
        out_specs=pl.BlockSpec((tm, tn), lambda i, j: (i, j)),
        out_shape=jax.ShapeDtypeStruct((m, n), out_dtype),
        compiler_params=_cparams(("parallel", "arbitrary")),
        name="matmul",
    )(a, w)


def _swiglu_body(a_ref, wg_ref, wu_ref, o_ref):
    a = a_ref[...]
    g = jnp.dot(a, wg_ref[...], preferred_element_type=F32)
    u = jnp.dot(a, wu_ref[...], preferred_element_type=F32)
    o_ref[...] = (_silu(g) * u).astype(o_ref.dtype)


def swiglu_up(a, wg, wu, tm, tn):
    m, k = a.shape
    n = wg.shape[1]
    assert m % tm == 0 and n % tn == 0
    return pl.pallas_call(
        _swiglu_body,
        grid=(m // tm, n // tn),
        in_specs=[pl.BlockSpec((tm, k), lambda i, j: (i, 0)),
                  pl.BlockSpec((k, tn), lambda i, j: (0, j)),
                  pl.BlockSpec((k, tn), lambda i, j: (0, j))],
        out_specs=pl.BlockSpec((tm, tn), lambda i, j: (i, j)),
        out_shape=jax.ShapeDtypeStruct((m, n), BF16),
        compiler_params=_cparams(("parallel", "arbitrary")),
        name="swiglu_up",
    )(a, wg, wu)


def _ln_body(h_ref, y_ref, g_ref, b_ref, o_ref, ob_ref):
    x = DEEPNORM_ALPHA * h_ref[...] + y_ref[...]
    mu = jnp.mean(x, axis=-1, keepdims=True)
    xc = x - mu
    var = jnp.mean(xc * xc, axis=-1, keepdims=True)
    out = xc * lax.rsqrt(var + LN_EPS) * g_ref[...] + b_ref[...]
    o_ref[...] = out
    ob_ref[...] = out.astype(BF16)


def deepnorm_ln(h, y, g, b, tm=256):
    m, d = h.shape
    row = pl.BlockSpec((tm, d), lambda i: (i, 0))
    vec = pl.BlockSpec((1, d), lambda i: (0, 0))
    return pl.pallas_call(
        _ln_body,
        grid=(m // tm,),
        in_specs=[row, row, vec, vec],
        out_specs=[row, row],
        out_shape=[jax.ShapeDtypeStruct((m, d), F32), jax.ShapeDtypeStruct((m, d), BF16)],
        compiler_params=_cparams(("parallel",)),
        name="deepnorm_ln",
    )(h, y, g.reshape(1, d), b.reshape(1, d))


def _ssd_body(xs_ref, b_ref, c_ref, dt_ref, z_ref, aneg_ref, dskip_ref, nw_ref, o_ref,
              state_ref, y_ref, *, heads, head_dim, groups, n_state):
    blk = xs_ref.shape[0]
    hpg = heads // groups
    gw = hpg * head_dim
    inner = heads * head_dim

    @pl.when(pl.program_id(1) == 0)
    def _():
        state_ref[...] = jnp.zeros_like(state_ref)

    row = lax.broadcasted_iota(jnp.int32, (blk, blk), 0)
    col = lax.broadcasted_iota(jnp.int32, (blk, blk), 1)
    causal = row >= col
    eh = lax.broadcasted_iota(jnp.int32, (heads, inner), 0)
    ec = lax.broadcasted_iota(jnp.int32, (heads, inner), 1)
    expand = (ec // head_dim == eh).astype(F32)

    dt = dt_ref[...]
    a = dt * aneg_ref[...]
    cs = jnp.dot(causal.astype(F32), a, preferred_element_type=F32, precision=HIGHEST)
    cs_t = cs.T
    cs_x = jnp.dot(cs, expand, preferred_element_type=F32, precision=HIGHEST)
    dt_x = jnp.dot(dt, expand, preferred_element_type=F32, precision=HIGHEST)
    xs = xs_ref[...]
    xdt = xs * dt_x
    last_x = cs_x[blk - 1:blk, :]
    x_in = (xdt * jnp.exp(last_x - cs_x)).astype(BF16)
    x_b = xdt.astype(BF16)
    grow = jnp.exp(cs_x)

    for g in range(groups):
        bg = b_ref[:, g * n_state:(g + 1) * n_state].astype(BF16)
        cg = c_ref[:, g * n_state:(g + 1) * n_state].astype(BF16)
        cb = lax.dot_general(cg, bg, (((1,), (1,)), ((), ())), preferred_element_type=F32)
        st = state_ref[g]
        y_off = jnp.dot(cg, st.astype(BF16), preferred_element_type=F32)
        y_ref[:, g * gw:(g + 1) * gw] = y_off * grow[:, g * gw:(g + 1) * gw]
        for r in range(hpg):
            h = g * hpg + r
            seg = cs[:, h:h + 1] - cs_t[h:h + 1, :]
            dec = jnp.exp(jnp.where(causal, seg, -jnp.inf))
            mh = (cb * dec).astype(BF16)
            yh = jnp.dot(mh, x_b[:, h * head_dim:(h + 1) * head_dim], preferred_element_type=F32)
            y_ref[:, h * head_dim:(h + 1) * head_dim] += yh
        new = lax.dot_general(bg, x_in[:, g * gw:(g + 1) * gw], (((0,), (0,)), ((), ())),
                              preferred_element_type=F32)
        state_ref[g] = st * jnp.exp(last_x[:, g * gw:(g + 1) * gw]) + new

    y = y_ref[...] + dskip_ref[...] * xs
    y = y * _silu(z_ref[...])
    for g in range(groups):
        yg = y[:, g * gw:(g + 1) * gw]
        ms = jnp.mean(yg * yg, axis=-1, keepdims=True)
        o_ref[:, g * gw:(g + 1) * gw] = (yg * lax.rsqrt(ms + RMS_EPS)
                                         * nw_ref[:, g * gw:(g + 1) * gw]).astype(o_ref.dtype)


def ssd_mixer(xs, bmat, cmat, dt, z, a_neg, d_skip_x, norm_w, *, batch, heads=SSD_HEADS,
              head_dim=SSD_HEAD_DIM, groups=SSD_GROUPS, n_state=SSD_STATE, blk=SSD_BLOCK):
    m, inner = xs.shape
    seq = m // batch
    nblk = seq // blk
    gw = inner // groups
    rows = lambda w: pl.BlockSpec((blk, w), lambda b, c: (b * nblk + c, 0))
    vec = lambda w: pl.BlockSpec((1, w), lambda b, c: (0, 0))
    body = functools.partial(_ssd_body, heads=heads, head_dim=head_dim, groups=groups, n_state=n_state)
    return pl.pallas_call(
        body,
        grid=(batch, nblk),
        in_specs=[rows(inner), rows(groups * n_state), rows(groups * n_state), rows(heads), rows(inner),
                  vec(heads), vec(inner), vec(inner)],
        out_specs=rows(inner),
        out_shape=jax.ShapeDtypeStruct((m, inner), BF16),
        scratch_shapes=[pltpu.VMEM((groups, n_state, gw), F32), pltpu.VMEM((blk, inner), F32)],
        compiler_params=_cparams(("parallel", "arbitrary")),
        name="ssd_mixer",
    )(xs, bmat, cmat, dt, z, a_neg.reshape(1, heads), d_skip_x.reshape(1, inner), norm_w.reshape(1, inner))


def _attn_body(lam_ref, q_ref, k_ref, v_ref, nw_ref, o_ref, *, post_scale, chunk):
    t, hd = q_ref.shape
    half = hd // 2
    i = pl.program_id(2)
    q = q_ref[...]
    lane = lax.broadcasted_iota(jnp.int32, (t, hd), 1)
    zero = jnp.zeros_like(q)
    qs = jnp.concatenate([jnp.where(lane < half, q, zero), jnp.where(lane >= half, q, zero)], axis=0)

    def step(j, carry, masked):
        m, l, acc = carry
        start = pl.multiple_of(j * t, t)
        kb = k_ref[pl.ds(start, t), :]
        vb = v_ref[pl.ds(start, t), :]
        s = lax.dot_general(qs, kb, (((1,), (1,)), ((), ())), preferred_element_type=F32)
        if masked:
            r = lax.broadcasted_iota(jnp.int32, (2 * t, t), 0)
            c = lax.broadcasted_iota(jnp.int32, (2 * t, t), 1)
            r = jnp.where(r >= t, r - t, r)
            s = jnp.where(c // chunk <= r // chunk, s, -jnp.inf)
        m_new = jnp.maximum(m, jnp.max(s, axis=-1, keepdims=True))
        alpha = jnp.exp(m - m_new)
        p = jnp.exp(s - m_new)
        l = alpha * l + jnp.sum(p, axis=-1, keepdims=True)
        acc = alpha * acc + jnp.dot(p.astype(BF16), vb, preferred_element_type=F32)
        return m_new, l, acc

    init = (jnp.full((2 * t, 1), -jnp.inf, F32), jnp.zeros((2 * t, 1), F32), jnp.zeros((2 * t, hd), F32))
    carry = lax.fori_loop(0, i, lambda j, c: step(j, c, False), init)
    m, l, acc = step(i, carry, True)
    o = acc / l
    out = o[:t] - lam_ref[0] * o[t:]
    ms = jnp.mean(out * out, axis=-1, keepdims=True)
    o_ref[...] = (out * lax.rsqrt(ms + RMS_EPS) * nw_ref[...] * post_scale).astype(o_ref.dtype)


def diff_attention(q, k, v, lam, norm_w, post_scale, *, batch, heads=DIFF_HEADS, blk=ATTN_BLOCK,
                   chunk=CHUNK):
    m, width = q.shape
    hd = width // heads
    seq = m // batch
    nq = seq // blk
    q3, k3, v3 = (t.reshape(batch, seq, width) for t in (q, k, v))
    body = functools.partial(_attn_body, post_scale=post_scale, chunk=chunk)
    out = pl.pallas_call(
        body,
        grid=(batch, heads, nq),
        in_specs=[pl.BlockSpec(memory_space=pltpu.SMEM),
                  pl.BlockSpec((None, blk, hd), lambda b, h, i: (b, i, h)),
                  pl.BlockSpec((None, seq, hd), lambda b, h, i: (b, 0, h)),
                  pl.BlockSpec((None, seq, hd), lambda b, h, i: (b, 0, h)),
                  pl.BlockSpec((1, hd), lambda b, h, i: (0, 0))],
        out_specs=pl.BlockSpec((None, blk, hd), lambda b, h, i: (b, i, h)),
        out_shape=jax.ShapeDtypeStruct((batch, seq, width), BF16),
        compiler_params=_cparams(("parallel", "parallel", "arbitrary")),
        name="diff_attention",
    )(lam.reshape(1).astype(F32), q3, k3, v3, norm_w.reshape(1, hd))
    return out.reshape(m, width)


def _ret_body(q_ref, k_ref, v_ref, g_ref, dmat_ref, qd_ref, kd_ref, bd_ref, o_ref, state_ref):
    @pl.when(pl.program_id(2) == 0)
    def _():
        state_ref[...] = jnp.zeros_like(state_ref)

    qb = q_ref[...].astype(BF16)
    k = k_ref[...]
    vb = v_ref[...].astype(BF16)
    s = lax.dot_general(qb, k.astype(BF16), (((1,), (1,)), ((), ())), preferred_element_type=F32)
    y = jnp.dot((s * dmat_ref[...]).astype(BF16), vb, preferred_element_type=F32)
    st = state_ref[...]
    y = y + jnp.dot(qb, st.astype(BF16), preferred_element_type=F32) * qd_ref[...]
    kdec = (k * kd_ref[...]).astype(BF16)
    state_ref[...] = st * bd_ref[...] + lax.dot_general(kdec, vb, (((0,), (0,)), ((), ())),
                                                        preferred_element_type=F32)
    mu = jnp.mean(y, axis=-1, keepdims=True)
    yc = y - mu
    var = jnp.mean(yc * yc, axis=-1, keepdims=True)
    o_ref[...] = (_silu(g_ref[...]) * (yc * lax.rsqrt(var + LN_EPS))).astype(o_ref.dtype)


def _retention_tables(heads, blk, chunk):
    log_g = jnp.log1p(-(2.0 ** (-5.0 - jnp.arange(heads, dtype=F32))))
    idx = jnp.arange(blk, dtype=F32)
    n, mm = idx[:, None], idx[None, :]
    cn, cm = jnp.floor(n / chunk), jnp.floor(mm / chunk)
    dist = jnp.where(cn == cm, jnp.abs(n - mm), n - mm)
    dmat = jnp.where((cm <= cn)[None], jnp.exp(dist[None] * log_g[:, None, None]), 0.0)
    qd = jnp.exp((idx + 1.0)[None, :, None] * log_g[:, None, None])
    kd = jnp.exp((blk - 1.0 - idx)[None, :, None] * log_g[:, None, None])
    bd = jnp.exp(blk * log_g)[:, None, None]
    return dmat, qd, kd, bd


def retention_mixer(q, k, v, g, *, batch, heads=RET_HEADS, blk=RET_BLOCK, chunk=CHUNK):
    m = q.shape[0]
    dk = q.shape[1] // heads
    dv = v.shape[1] // heads
    seq = m // batch
    nblk = seq // blk
    dmat, qd, kd, bd = _retention_tables(heads, blk, chunk)
    rows = lambda w: pl.BlockSpec((blk, w), lambda b, h, c: (b * nblk + c, h))
    return pl.pallas_call(
        _ret_body,
        grid=(batch, heads, nblk),
        in_specs=[rows(dk), rows(dk), rows(dv), rows(dv),
                  pl.BlockSpec((None, blk, blk), lambda b, h, c: (h, 0, 0)),
                  pl.BlockSpec((None, blk, 1), lambda b, h, c: (h, 0, 0)),
                  pl.BlockSpec((None, blk, 1), lambda b, h, c: (h, 0, 0)),
                  pl.BlockSpec((None, 1, 1), lambda b, h, c: (h, 0, 0))],
        out_specs=rows(dv),
        out_shape=jax.ShapeDtypeStruct((m, heads * dv), BF16),
        scratch_shapes=[pltpu.VMEM((dk, dv), F32)],
        compiler_params=_cparams(("parallel", "parallel", "arbitrary")),
        name="retention_mixer",
    )(q, k, v, g, dmat, qd, kd, bd)


def _router_body(h_ref, w_ref, idx_ref, wt_ref, *, n_experts):
    logits = jnp.dot(h_ref[...], w_ref[...], preferred_element_type=F32, precision=HIGHEST)
    lane = lax.broadcasted_iota(jnp.int32, logits.shape, 1)
    logits = jnp.where(lane < n_experts, logits, -jnp.inf)
    v1 = jnp.max(logits, axis=-1, keepdims=True)
    i1 = jnp.min(jnp.where(logits == v1, lane, LANES), axis=-1, keepdims=True)
    rest = jnp.where(lane == i1, -jnp.inf, logits)
    v2 = jnp.max(rest, axis=-1, keepdims=True)
    i2 = jnp.min(jnp.where(rest == v2, lane, LANES), axis=-1, keepdims=True)
    e2 = jnp.exp(v2 - v1)
    w1 = 1.0 / (1.0 + e2)
    w2 = e2 / (1.0 + e2)
    idx_ref[...] = jnp.where(lane == 0, i1, jnp.where(lane == 1, i2, 0))
    wt_ref[...] = jnp.where(lane == 0, w1, jnp.where(lane == 1, w2, 0.0))


def moe_router(h, w_router, tm=512):
    m, d = h.shape
    n_experts = w_router.shape[1]
    w_pad = jnp.zeros((d, LANES), F32).at[:, :n_experts].set(w_router)
    row = pl.BlockSpec((tm, LANES), lambda i: (i, 0))
    idx, wt = pl.pallas_call(
        functools.partial(_router_body, n_experts=n_experts),
        grid=(m // tm,),
        in_specs=[pl.BlockSpec((tm, d), lambda i: (i, 0)), pl.BlockSpec((d, LANES), lambda i: (0, 0))],
        out_specs=[row, row],
        out_shape=[jax.ShapeDtypeStruct((m, LANES), jnp.int32), jax.ShapeDtypeStruct((m, LANES), F32)],
        compiler_params=_cparams(("parallel",)),
        name="moe_router",
    )(h, w_pad)
    return idx[:, :TOP_K], wt[:, :TOP_K]


def _moe_up_body(te_ref, nv_ref, x_ref, wg_ref, wu_ref, o_ref):
    t = pl.program_id(1)

    @pl.when(t < nv_ref[0])
    def _():
        x = x_ref[...]
        g = jnp.dot(x, wg_ref[...], preferred_element_type=F32)
        u = jnp.dot(x, wu_ref[...], preferred_element_type=F32)
        o_ref[...] = (_silu(g) * u).astype(o_ref.dtype)

    @pl.when(t >= nv_ref[0])
    def _():
        o_ref[...] = jnp.zeros_like(o_ref)


def _moe_down_body(te_ref, nv_ref, a_ref, w_ref, rw_ref, o_ref):
    t = pl.program_id(1)

    @pl.when(t < nv_ref[0])
    def _():
        y = jnp.dot(a_ref[...], w_ref[...], preferred_element_type=F32)
        o_ref[...] = (y * rw_ref[...]).astype(o_ref.dtype)

    @pl.when(t >= nv_ref[0])
    def _():
        o_ref[...] = jnp.zeros_like(o_ref)


def moe_experts(x_sorted, tile_expert, n_valid, row_w, w_gate, w_up, w_down, tm=MOE_TM, tn=512):
    p, d = x_sorted.shape
    f = w_gate.shape[2]
    nt = p // tm
    up_spec = pltpu.PrefetchScalarGridSpec(
        num_scalar_prefetch=2,
        grid=(f // tn, nt),
        in_specs=[pl.BlockSpec((tm, d), lambda j, t, te, nv: (t, 0)),
                  pl.BlockSpec((None, d, tn), lambda j, t, te, nv: (te[t], 0, j)),
                  pl.BlockSpec((None, d, tn), lambda j, t, te, nv: (te[t], 0, j))],
        out_specs=pl.BlockSpec((tm, tn), lambda j, t, te, nv: (t, j)),
    )
    act = pl.pallas_call(
        _moe_up_body, grid_spec=up_spec,
        out_shape=jax.ShapeDtypeStruct((p, f), BF16),
        compiler_params=_cparams(("parallel", "arbitrary")),
        name="moe_up",
    )(tile_expert, n_valid, x_sorted, w_gate, w_up)
    down_spec = pltpu.PrefetchScalarGridSpec(
        num_scalar_prefetch=2,
        grid=(d // tn, nt),
        in_specs=[pl.BlockSpec((tm, f), lambda j, t, te, nv: (t, 0)),
                  pl.BlockSpec((None, f, tn), lambda j, t, te, nv: (te[t], 0, j)),
                  pl.BlockSpec((tm, 1), lambda j, t, te, nv: (t, 0))],
        out_specs=pl.BlockSpec((tm, tn), lambda j, t, te, nv: (t, j)),
    )
    return pl.pallas_call(
        _moe_down_body, grid_spec=down_spec,
        out_shape=jax.ShapeDtypeStruct((p, d), F32),
        compiler_params=_cparams(("parallel", "arbitrary")),
        name="moe_down",
    )(tile_expert, n_valid, act, w_down, row_w)


def moe_routing(top_i, top_w, n_experts, tm):
    m = top_i.shape[0]
    pairs = m * TOP_K
    p = pairs + n_experts * tm
    flat_e = top_i.reshape(pairs)
    onehot = (flat_e[:, None] == jnp.arange(n_experts, dtype=jnp.int32)[None, :]).astype(jnp.int32)
    rank = jnp.sum((jnp.cumsum(onehot, axis=0) - onehot) * onehot, axis=1)
    counts = jnp.sum(onehot, axis=0)
    padded = ((counts + tm - 1) // tm) * tm
    ends = jnp.cumsum(padded)
    starts = ends - padded
    dest = starts[flat_e] + rank
    row_token = jnp.zeros((p,), jnp.int32).at[dest].set(jnp.arange(pairs, dtype=jnp.int32) // TOP_K)
    row_w = jnp.zeros((p,), F32).at[dest].set(top_w.reshape(pairs)).reshape(p, 1)
    tile_start = jnp.arange(p // tm, dtype=jnp.int32) * tm
    tile_expert = jnp.minimum(jnp.sum((tile_start[:, None] >= ends[None, :]).astype(jnp.int32), axis=1),
                              n_experts - 1).astype(jnp.int32)
    n_valid = (ends[-1] // tm).astype(jnp.int32).reshape(1)
    return row_token, row_w, dest.reshape(m, TOP_K), tile_expert, n_valid


def _rotary(x, inv_freq, seq):
    d = x.shape[-1]
    ang = jnp.arange(seq, dtype=F32)[:, None] * inv_freq[None, :]
    shape = (1, seq) + (1,) * (x.ndim - 3) + (d // 2,)
    cos = jnp.cos(ang).reshape(shape)
    sin = jnp.sin(ang).reshape(shape)
    x1, x2 = x[..., : d // 2], x[..., d // 2:]
    return jnp.concatenate([x1 * cos - x2 * sin, x2 * cos + x1 * sin], axis=-1)


def _causal_conv_silu(x, w, b):
    kk = w.shape[0]
    seq = x.shape[1]
    xp = jnp.pad(x, ((0, 0), (kk - 1, 0), (0, 0)))
    y = b
    for i in range(kk):
        y = y + w[i] * xp[:, i:i + seq, :]
    return jax.nn.silu(y)


def _layer_even(h, hb, batch, seq, w_in, conv_w, conv_b, dt_bias, a_log, d_skip, ssd_norm_w,
                lam_q1, lam_k1, lam_q2, lam_k2, diff_norm_w, w_out, lambda_init):
    m = batch * seq
    c0 = SSD_INNER
    c1 = c0 + SSD_XBC
    c2 = c1 + SSD_HEADS
    wb = lambda lo, hi: w_in[:, lo:hi].astype(BF16)
    z = matmul(hb, wb(0, c0), F32, 1024, 512)
    xbc = matmul(hb, wb(c0, c1), F32, 1024, 512)
    w_dt = jnp.zeros((D_MODEL, LANES), BF16).at[:, :SSD_HEADS].set(wb(c1, c2))
    dt = matmul(hb, w_dt, F32, 1024, LANES)[:, :SSD_HEADS]
    q = matmul(hb, wb(c2, c2 + DIFF_WIDTH), F32, 1024, 512)
    k = matmul(hb, wb(c2 + DIFF_WIDTH, c2 + 2 * DIFF_WIDTH), F32, 1024, 512)
    v = matmul(hb, wb(c2 + 2 * DIFF_WIDTH, c2 + 3 * DIFF_WIDTH), BF16, 1024, 512)
    xbc = _causal_conv_silu(xbc.reshape(batch, seq, SSD_XBC), conv_w, conv_b).reshape(m, SSD_XBC)
    dt = jax.nn.softplus(dt + dt_bias)
    y = ssd_mixer(xbc[:, :SSD_INNER], xbc[:, SSD_INNER:SSD_INNER + SSD_BC], xbc[:, SSD_INNER + SSD_BC:],
                  dt, z, -jnp.exp(a_log), jnp.repeat(d_skip, SSD_HEAD_DIM), ssd_norm_w, batch=batch)
    inv_freq = ROPE_THETA ** (-jnp.arange(0, DIFF_QK_DIM, 2, dtype=F32) / DIFF_QK_DIM)
    qr = _rotary(q.reshape(batch, seq, DIFF_HEADS, 2, DIFF_QK_DIM), inv_freq, seq) * (DIFF_QK_DIM ** -0.5)
    kr = _rotary(k.reshape(batch, seq, DIFF_HEADS, 2, DIFF_QK_DIM), inv_freq, seq)
    lam = (jnp.exp(jnp.sum(lam_q1 * lam_k1)) - jnp.exp(jnp.sum(lam_q2 * lam_k2)) + lambda_init)
    o = diff_attention(qr.reshape(m, DIFF_WIDTH).astype(BF16), kr.reshape(m, DIFF_WIDTH).astype(BF16), v,
                       lam, diff_norm_w, 1.0 - lambda_init, batch=batch)
    mixed = jnp.concatenate([y, o], axis=-1)
    return matmul(mixed, w_out.astype(BF16), F32, 512, 512)


def _layer_odd_mixer(hb, batch, seq, w_in, w_out):
    m = batch * seq
    wb = lambda lo, hi: w_in[:, lo:hi].astype(BF16)
    c0, c1, c2 = RET_QK_WIDTH, 2 * RET_QK_WIDTH, 2 * RET_QK_WIDTH + RET_V_WIDTH
    q = matmul(hb, wb(0, c0), F32, 1024, 512)
    k = matmul(hb, wb(c0, c1), F32, 1024, 512)
    v = matmul(hb, wb(c1, c2), F32, 1024, 512)
    g = matmul(hb, wb(c2, c2 + RET_V_WIDTH), F32, 1024, 512)
    inv_freq = 1.0 / (ROPE_THETA ** jnp.linspace(0.0, 1.0, RET_QK_DIM // 2, dtype=F32))
    qr = _rotary(q.reshape(batch, seq, RET_HEADS, RET_QK_DIM), inv_freq, seq).reshape(m, RET_QK_WIDTH)
    kr = (_rotary(k.reshape(batch, seq, RET_HEADS, RET_QK_DIM), inv_freq, seq)
          * (RET_QK_DIM ** -0.5)).reshape(m, RET_QK_WIDTH)
    o = retention_mixer(qr, kr, v, g, batch=batch)
    return matmul(o, w_out.astype(BF16), F32, 512, 512)


def _moe(h, hb, w_router, w_gate, w_up, w_down):
    top_i, top_w = moe_router(h, w_router)
    row_token, row_w, pos, tile_expert, n_valid = moe_routing(top_i, top_w, N_EXPERTS, MOE_TM)
    x_sorted = jnp.take(hb, row_token, axis=0)
    y = moe_experts(x_sorted, tile_expert, n_valid, row_w,
                    w_gate.astype(BF16), w_up.astype(BF16), w_down.astype(BF16))
    return jnp.take(y, pos[:, 0], axis=0) + jnp.take(y, pos[:, 1], axis=0)


def kernel(x, ab_w_in, ab_conv_w, ab_conv_b, ab_dt_bias, ab_a_log, ab_d_skip, ab_ssd_norm_w, ab_lam_q1, ab_lam_k1, ab_lam_q2, ab_lam_k2, ab_diff_norm_w, ab_w_out, ffn_w_gate, ffn_w_up, ffn_w_down, c_w_in, c_w_out, moe_w_router, moe_w_gate, moe_w_up, moe_w_down, ln_mix_g, ln_mix_b, ln_ffn_g, ln_ffn_b):
    batch, seq, d = x.shape
    m = batch * seq
    h = x.reshape(m, d)
    hb = h.astype(BF16)
    lambda_init = 0.8 - 0.6 * math.exp(-0.3 * 0)
    mix = _layer_even(h, hb, batch, seq, ab_w_in[0], ab_conv_w[0], ab_conv_b[0], ab_dt_bias[0], ab_a_log[0],
                      ab_d_skip[0], ab_ssd_norm_w[0], ab_lam_q1[0], ab_lam_k1[0], ab_lam_q2[0], ab_lam_k2[0],
                      ab_diff_norm_w[0], ab_w_out[0], lambda_init)
    h, hb = deepnorm_ln(h, mix, ln_mix_g[0], ln_mix_b[0])
    act = swiglu_up(hb, ffn_w_gate[0].astype(BF16), ffn_w_up[0].astype(BF16), 1024, 256)
    ff = matmul(act, ffn_w_down[0].astype(BF16), F32, 512, 256)
    h, hb = deepnorm_ln(h, ff, ln_ffn_g[0], ln_ffn_b[0])
    mix = _layer_odd_mixer(hb, batch, seq, c_w_in[0], c_w_out[0])
    h, hb = deepnorm_ln(h, mix, ln_mix_g[1], ln_mix_b[1])
    ff = _moe(h, hb, moe_w_router[0], moe_w_gate[0], moe_w_up[0], moe_w_down[0])
    h, _ = deepnorm_ln(h, ff, ln_ffn_g[1], ln_ffn_b[1])
    return h.reshape(batch, seq, d)
```

```python
import functools
import math

import jax
import jax.numpy as jnp
from jax import lax
from jax.experimental import pallas as pl
from jax.experimental.pallas import tpu as pltpu

F32 = jnp.float32
BF16 = jnp.bfloat16
HIGHEST = lax.Precision.HIGHEST

D_MODEL = 4096
DEPTH = 2
CHUNK = 64
SSD_HEADS = 64
SSD_HEAD_DIM = 64
SSD_INNER = SSD_HEADS * SSD_HEAD_DIM
SSD_GROUPS = 8
SSD_STATE = 128
SSD_CONV = 4
SSD_BC = SSD_GROUPS * SSD_STATE
SSD_XBC = SSD_INNER + 2 * SSD_BC
DIFF_HEADS = 32
DIFF_QK_DIM = 64
DIFF_V_DIM = 2 * DIFF_QK_DIM
DIFF_WIDTH = DIFF_HEADS * DIFF_V_DIM
ROPE_THETA = 10000.0
RET_HEADS = 16
RET_QK_DIM = D_MODEL // RET_HEADS
RET_V_DIM = 2 * RET_QK_DIM
RET_QK_WIDTH = RET_HEADS * RET_QK_DIM
RET_V_WIDTH = RET_HEADS * RET_V_DIM
FFN_DIM = 11008
N_EXPERTS = 8
TOP_K = 2
EXPERT_DIM = 4096
DEEPNORM_ALPHA = (2 * DEPTH) ** 0.25
LN_EPS = 1e-5
RMS_EPS = 1e-5

LANES = 128
SUBLANES = 8
VMEM_LIMIT_BYTES = 56 * 1024 * 1024

SSD_BLOCK = 128
RET_BLOCK = 256
ATTN_BLOCK = 256
ATTN_HEADS_PER_STEP = 2
MOE_TM = 512


def _cparams(sem):
    return pltpu.CompilerParams(dimension_semantics=sem, vmem_limit_bytes=VMEM_LIMIT_BYTES)


def _silu(x):
    return x * (1.0 / (1.0 + jnp.exp(-x)))


def _weight_bf16(w_ref, wb_ref, first):
    if wb_ref is None:
        return w_ref[...]

    @pl.when(first)
    def _():
        wb_ref[...] = w_ref[...].astype(BF16)

    return wb_ref[...]


def _rotate(acc, cos_ref, sa_ref, sb_ref, half):
    tn = acc.shape[1]
    reps = tn // cos_ref.shape[1]
    tile = lambda r: jnp.tile(r[...], (1, reps))
    return (acc * tile(cos_ref) + pltpu.roll(acc, tn - half, 1) * tile(sa_ref)
            + pltpu.roll(acc, half, 1) * tile(sb_ref))


def _proj_body(*refs, n_a, rot_half, cast):
    a_refs = refs[:n_a]
    w_ref = refs[n_a]
    pos = n_a + 1
    if rot_half:
        cos_ref, sa_ref, sb_ref = refs[pos:pos + 3]
        pos += 3
    o_ref = refs[pos]
    wb_ref = refs[pos + 1] if cast else None
    w = _weight_bf16(w_ref, wb_ref, pl.program_id(1) == 0)
    k0 = 0
    acc = None
    for a_ref in a_refs:
        ka = a_ref.shape[1]
        part = jnp.dot(a_ref[...], w[k0:k0 + ka, :], preferred_element_type=F32)
        acc = part if acc is None else acc + part
        k0 += ka
    if rot_half:
        acc = _rotate(acc, cos_ref, sa_ref, sb_ref, rot_half)
    o_ref[...] = acc.astype(o_ref.dtype)


def project(a_list, w, out_dtype, tm, tn, *, col_lo=0, n=None, rot=None, seq=None):
    m = a_list[0].shape[0]
    k = w.shape[0]
    n = w.shape[1] - col_lo if n is None else n
    assert m % tm == 0 and n % tn == 0 and col_lo % tn == 0 and sum(a.shape[1] for a in a_list) == k
    off = col_lo // tn
    cast = w.dtype != BF16
    in_specs = [pl.BlockSpec((tm, a.shape[1]), lambda j, i: (i, 0)) for a in a_list]
    in_specs.append(pl.BlockSpec((k, tn), lambda j, i: (0, j + off)))
    args = list(a_list) + [w]
    rot_half = 0
    if rot is not None:
        cos, sa, sb, rot_half = rot
        nrow = seq // tm
        tab = pl.BlockSpec((tm, cos.shape[1]), lambda j, i: (i % nrow, 0))
        in_specs += [tab, tab, tab]
        args += [cos, sa, sb]
    return pl.pallas_call(
        functools.partial(_proj_body, n_a=len(a_list), rot_half=rot_half, cast=cast),
        grid=(n // tn, m // tm),
        in_specs=in_specs,
        out_specs=pl.BlockSpec((tm, tn), lambda j, i: (i, j)),
        out_shape=jax.ShapeDtypeStruct((m, n), out_dtype),
        scratch_shapes=[pltpu.VMEM((k, tn), BF16)] if cast else [],
        compiler_params=_cparams(("parallel", "arbitrary")),
        name="project",
    )(*args)


def _swiglu_body(a_ref, wg_ref, wu_ref, o_ref, wgb_ref, wub_ref):
    first = pl.program_id(1) == 0
    a = a_ref[...]
    g = jnp.dot(a, _weight_bf16(wg_ref, wgb_ref, first), preferred_element_type=F32)
    u = jnp.dot(a, _weight_bf16(wu_ref, wub_ref, first), preferred_element_type=F32)
    o_ref[...] = (_silu(g) * u).astype(o_ref.dtype)


def swiglu_up(a, wg, wu, tm, tn):
    m, k = a.shape
    n = wg.shape[1]
    assert m % tm == 0 and n % tn == 0
    wspec = pl.BlockSpec((k, tn), lambda j, i: (0, j))
    return pl.pallas_call(
        _swiglu_body,
        grid=(n // tn, m // tm),
        in_specs=[pl.BlockSpec((tm, k), lambda j, i: (i, 0)), wspec, wspec],
        out_specs=pl.BlockSpec((tm, tn), lambda j, i: (i, j)),
        out_shape=jax.ShapeDtypeStruct((m, n), BF16),
        scratch_shapes=[pltpu.VMEM((k, tn), BF16), pltpu.VMEM((k, tn), BF16)],
        compiler_params=_cparams(("parallel", "arbitrary")),
        name="swiglu_up",
    )(a, wg, wu)


def _mm_body(a_ref, w_ref, o_ref):
    o_ref[...] = jnp.dot(a_ref[...], w_ref[...], preferred_element_type=F32).astype(o_ref.dtype)


def matmul(a, w, out_dtype, tm, tn):
    m, k = a.shape
    n = w.shape[1]
    assert m % tm == 0 and n % tn == 0, (m, n, tm, tn)
    return pl.pallas_call(
        _mm_body,
        grid=(m // tm, n // tn),
        in_specs=[pl.BlockSpec((tm, k), lambda i, j: (i, 0)),
                  pl.BlockSpec((k, tn), lambda i, j: (0, j))],
        out_specs=pl.BlockSpec((tm, tn), lambda i, j: (i, j)),
        out_shape=jax.ShapeDtypeStruct((m, n), out_dtype),
        compiler_params=_cparams(("parallel", "arbitrary")),
        name="matmul",
    )(a, w)


def _ln_body(h_ref, y_ref, g_ref, b_ref, o_ref, ob_ref):
    x = DEEPNORM_ALPHA * h_ref[...] + y_ref[...]
    mu = jnp.mean(x, axis=-1, keepdims=True)
    xc = x - mu
    var = jnp.mean(xc * xc, axis=-1, keepdims=True)
    out = xc * lax.rsqrt(var + LN_EPS) * g_ref[...] + b_ref[...]
    o_ref[...] = out
    ob_ref[...] = out.astype(BF16)


def deepnorm_ln(h, y, g, b, tm=256):
    m, d = h.shape
    row = pl.BlockSpec((tm, d), lambda i: (i, 0))
    vec = pl.BlockSpec((1, d), lambda i: (0, 0))
    return pl.pallas_call(
        _ln_body,
        grid=(m // tm,),
        in_specs=[row, row, vec, vec],
        out_specs=[row, row],
        out_shape=[jax.ShapeDtypeStruct((m, d), F32), jax.ShapeDtypeStruct((m, d), BF16)],
        compiler_params=_cparams(("parallel",)),
        name="deepnorm_ln",
    )(h, y, g.reshape(1, d), b.reshape(1, d))


def _conv_silu(raw_ref, stage_ref, w_ref, b_ref, first):
    blk = raw_ref.shape[0]
    kk = w_ref.shape[0]

    @pl.when(first)
    def _():
        stage_ref[0:SUBLANES, :] = jnp.zeros((SUBLANES, stage_ref.shape[1]), F32)

    stage_ref[SUBLANES:SUBLANES + blk, :] = raw_ref[...]
    y = b_ref[...]
    for i in range(kk):
        y = y + w_ref[i:i + 1, :] * stage_ref[pl.ds(SUBLANES - (kk - 1) + i, blk), :]
    stage_ref[0:SUBLANES, :] = stage_ref[blk:blk + SUBLANES, :]
    return _silu(y)


def _ssd_body(z_ref, xr_ref, br_ref, cr_ref, dtr_ref, cwx_ref, cwb_ref, cwc_ref, cbx_ref, cbb_ref, cbc_ref,
              dtb_ref, aneg_ref, dskip_ref, nw_ref, o_ref,
              state_ref, y_ref, sx_ref, sb_ref, sc_ref, *, heads, head_dim, groups, n_state):
    blk = xr_ref.shape[0]
    hpg = heads // groups
    gw = hpg * head_dim
    inner = heads * head_dim
    first = pl.program_id(1) == 0

    @pl.when(first)
    def _():
        state_ref[...] = jnp.zeros_like(state_ref)

    xs = _conv_silu(xr_ref, sx_ref, cwx_ref, cbx_ref, first)
    bmat = _conv_silu(br_ref, sb_ref, cwb_ref, cbb_ref, first).astype(BF16)
    cmat = _conv_silu(cr_ref, sc_ref, cwc_ref, cbc_ref, first).astype(BF16)

    row = lax.broadcasted_iota(jnp.int32, (blk, blk), 0)
    col = lax.broadcasted_iota(jnp.int32, (blk, blk), 1)
    causal = row >= col
    eh = lax.broadcasted_iota(jnp.int32, (heads, inner), 0)
    ec = lax.broadcasted_iota(jnp.int32, (heads, inner), 1)
    expand = (ec // head_dim == eh).astype(F32)

    dtp = dtr_ref[:, :heads] + dtb_ref[...]
    dt = jnp.maximum(dtp, 0.0) + jnp.log(1.0 + jnp.exp(-jnp.abs(dtp)))
    a = dt * aneg_ref[...]
    cs = jnp.dot(causal.astype(F32), a, preferred_element_type=F32, precision=HIGHEST)
    cs_t = cs.T
    cs_x = jnp.dot(cs, expand, preferred_element_type=F32, precision=HIGHEST)
    dt_x = jnp.dot(dt, expand, preferred_element_type=F32, precision=HIGHEST)
    xdt = xs * dt_x
    last_x = cs_x[blk - 1:blk, :]
    x_in = (xdt * jnp.exp(last_x - cs_x)).astype(BF16)
    x_b = xdt.astype(BF16)
    grow = jnp.exp(cs_x)

    for g in range(groups):
        bg = bmat[:, g * n_state:(g + 1) * n_state]
        cg = cmat[:, g * n_state:(g + 1) * n_state]
        cb = lax.dot_general(cg, bg, (((1,), (1,)), ((), ())), preferred_element_type=F32)
        st = state_ref[g]
        y_off = jnp.dot(cg, st.astype(BF16), preferred_element_type=F32)
        y_ref[:, g * gw:(g + 1) * gw] = y_off * grow[:, g * gw:(g + 1) * gw]
        for r in range(hpg):
            h = g * hpg + r
            seg = cs[:, h:h + 1] - cs_t[h:h + 1, :]
            dec = jnp.exp(jnp.where(causal, seg, -jnp.inf))
            mh = (cb * dec).astype(BF16)
            yh = jnp.dot(mh, x_b[:, h * head_dim:(h + 1) * head_dim], preferred_element_type=F32)
            y_ref[:, h * head_dim:(h + 1) * head_dim] += yh
        new = lax.dot_general(bg, x_in[:, g * gw:(g + 1) * gw], (((0,), (0,)), ((), ())),
                              preferred_element_type=F32)
        state_ref[g] = st * jnp.exp(last_x[:, g * gw:(g + 1) * gw]) + new

    y = y_ref[...] + dskip_ref[...] * xs
    y = y * _silu(z_ref[...])
    for g in range(groups):
        yg = y[:, g * gw:(g + 1) * gw]
        ms = jnp.mean(yg * yg, axis=-1, keepdims=True)
        o_ref[:, g * gw:(g + 1) * gw] = (yg * lax.rsqrt(ms + RMS_EPS)
                                         * nw_ref[:, g * gw:(g + 1) * gw]).astype(o_ref.dtype)


def ssd_mixer(zx, dt_raw, conv_w, conv_b, dt_bias, a_neg, d_skip_x, norm_w, *, batch, heads=SSD_HEADS,
              head_dim=SSD_HEAD_DIM, groups=SSD_GROUPS, n_state=SSD_STATE, blk=SSD_BLOCK):
    m = zx.shape[0]
    inner = heads * head_dim
    bc = groups * n_state
    seq = m // batch
    nblk = seq // blk
    gw = inner // groups
    kk = conv_w.shape[0]
    rows = lambda w, cb: pl.BlockSpec((blk, w), lambda b, c: (b * nblk + c, cb))
    full = lambda r, w: pl.BlockSpec((r, w), lambda b, c: (0, 0))
    body = functools.partial(_ssd_body, heads=heads, head_dim=head_dim, groups=groups, n_state=n_state)
    cw = (conv_w[:, :inner], conv_w[:, inner:inner + bc], conv_w[:, inner + bc:])
    cb = tuple(t.reshape(1, -1) for t in (conv_b[:inner], conv_b[inner:inner + bc], conv_b[inner + bc:]))
    return pl.pallas_call(
        body,
        grid=(batch, nblk),
        in_specs=[rows(inner, 0), rows(inner, 1), rows(bc, 2 * inner // bc), rows(bc, 2 * inner // bc + 1),
                  rows(dt_raw.shape[1], 0),
                  full(kk, inner), full(kk, bc), full(kk, bc), full(1, inner), full(1, bc), full(1, bc),
                  full(1, heads), full(1, heads), full(1, inner), full(1, inner)],
        out_specs=rows(inner, 0),
        out_shape=jax.ShapeDtypeStruct((m, inner), BF16),
        scratch_shapes=[pltpu.VMEM((groups, n_state, gw), F32), pltpu.VMEM((blk, inner), F32),
                        pltpu.VMEM((blk + SUBLANES, inner), F32), pltpu.VMEM((blk + SUBLANES, bc), F32),
                        pltpu.VMEM((blk + SUBLANES, bc), F32)],
        compiler_params=_cparams(("parallel", "arbitrary")),
        name="ssd_mixer",
    )(zx, zx, zx, zx, dt_raw, *cw, *cb, dt_bias.reshape(1, heads), a_neg.reshape(1, heads),
      d_skip_x.reshape(1, inner), norm_w.reshape(1, inner))


def _attn_body(lam_ref, q_ref, k_ref, v_ref, nw_ref, o_ref, acc_ref, *, post_scale, chunk, hd):
    t = q_ref.shape[0]
    hps = q_ref.shape[1] // hd
    half = hd // 2
    i = pl.program_id(2)
    sub = lax.broadcasted_iota(jnp.int32, (hd, t), 0)
    qs_t = []
    for hh in range(hps):
        qt = q_ref[:, hh * hd:(hh + 1) * hd].astype(F32).T
        qs_t.append(jnp.concatenate([jnp.where(sub < half, qt, 0.0), jnp.where(sub >= half, qt, 0.0)],
                                    axis=1).astype(BF16))
    acc_ref[...] = jnp.zeros_like(acc_ref)

    def scores(j, hh):
        start = pl.multiple_of(j * t, t)
        kb = k_ref[pl.ds(start, t), hh * hd:(hh + 1) * hd]
        return jnp.dot(kb, qs_t[hh], preferred_element_type=F32)

    def step(j, carry, last):
        start = pl.multiple_of(j * t, t)
        out = []
        for hh in range(hps):
            m, l, s = carry[hh]
            s_next = s if last else scores(j + 1, hh)
            vb = v_ref[pl.ds(start, t), hh * hd:(hh + 1) * hd]
            if last:
                kk = lax.broadcasted_iota(jnp.int32, (t, 2 * t), 0)
                qq = lax.broadcasted_iota(jnp.int32, (t, 2 * t), 1)
                qq = jnp.where(qq >= t, qq - t, qq)
                s = jnp.where(kk // chunk <= qq // chunk, s, -jnp.inf)
            m_new = jnp.maximum(m, jnp.max(s, axis=0, keepdims=True))
            alpha = jnp.exp2(m - m_new)
            p = jnp.exp2(s - m_new)
            l = alpha * l + jnp.sum(p, axis=0, keepdims=True)
            pv = lax.dot_general(vb, p.astype(BF16), (((0,), (0,)), ((), ())),
                                 preferred_element_type=F32)
            acc_ref[hh] = alpha * acc_ref[hh] + pv
            out.append((m_new, l, s_next))
        return tuple(out)

    init = tuple((jnp.full((1, 2 * t), -jnp.inf, F32), jnp.zeros((1, 2 * t), F32), scores(0, hh))
                 for hh in range(hps))
    carry = lax.fori_loop(0, i, lambda j, c: step(j, c, False), init)
    carry = step(i, carry, True)
    for hh in range(hps):
        o = acc_ref[hh] * (1.0 / carry[hh][1])
        out = o[:, :t] - lam_ref[0] * o[:, t:]
        ms = jnp.mean(out * out, axis=0, keepdims=True)
        out = out * lax.rsqrt(ms + RMS_EPS) * (nw_ref[...] * post_scale)
        o_ref[:, hh * hd:(hh + 1) * hd] = out.T.astype(o_ref.dtype)


def diff_attention(q, k, v, lam, norm_w, post_scale, *, batch, heads=DIFF_HEADS, blk=ATTN_BLOCK,
                   chunk=CHUNK, hps=ATTN_HEADS_PER_STEP):
    m, width = q.shape
    hd = width // heads
    seq = m // batch
    nq = seq // blk
    q3, k3, v3 = (t.reshape(batch, seq, width) for t in (q, k, v))
    body = functools.partial(_attn_body, post_scale=post_scale, chunk=chunk, hd=hd)
    out = pl.pallas_call(
        body,
        grid=(batch, heads // hps, nq),
        in_specs=[pl.BlockSpec(memory_space=pltpu.SMEM),
                  pl.BlockSpec((None, blk, hps * hd), lambda b, h, i: (b, i, h)),
                  pl.BlockSpec((None, seq, hps * hd), lambda b, h, i: (b, 0, h)),
                  pl.BlockSpec((None, seq, hps * hd), lambda b, h, i: (b, 0, h)),
                  pl.BlockSpec((hd, 1), lambda b, h, i: (0, 0))],
        out_specs=pl.BlockSpec((None, blk, hps * hd), lambda b, h, i: (b, i, h)),
        out_shape=jax.ShapeDtypeStruct((batch, seq, width), BF16),
        scratch_shapes=[pltpu.VMEM((hps, hd, 2 * blk), F32)],
        compiler_params=_cparams(("parallel", "parallel", "arbitrary")),
        name="diff_attention",
    )(lam.reshape(1).astype(F32), q3, k3, v3, norm_w.reshape(hd, 1))
    return out.reshape(m, width)


def _ret_body(q_ref, k_ref, v_ref, g_ref, dmat_ref, qd_ref, kd_ref, bd_ref, o_ref, state_ref):
    @pl.when(pl.program_id(2) == 0)
    def _():
        state_ref[...] = jnp.zeros_like(state_ref)

    qb = q_ref[...].astype(BF16)
    k = k_ref[...]
    vb = v_ref[...].astype(BF16)
    s = lax.dot_general(qb, k.astype(BF16), (((1,), (1,)), ((), ())), preferred_element_type=F32)
    y = jnp.dot((s * dmat_ref[...]).astype(BF16), vb, preferred_element_type=F32)
    st = state_ref[...]
    y = y + jnp.dot(qb, st.astype(BF16), preferred_element_type=F32) * qd_ref[...]
    kdec = (k.astype(F32) * kd_ref[...]).astype(BF16)
    state_ref[...] = st * bd_ref[...] + lax.dot_general(kdec, vb, (((0,), (0,)), ((), ())),
                                                        preferred_element_type=F32)
    mu = jnp.mean(y, axis=-1, keepdims=True)
    yc = y - mu
    var = jnp.mean(yc * yc, axis=-1, keepdims=True)
    o_ref[...] = (_silu(g_ref[...].astype(F32)) * (yc * lax.rsqrt(var + LN_EPS))).astype(o_ref.dtype)


def _retention_tables(heads, blk, chunk):
    log_g = jnp.log1p(-(2.0 ** (-5.0 - jnp.arange(heads, dtype=F32))))
    idx = jnp.arange(blk, dtype=F32)
    n, mm = idx[:, None], idx[None, :]
    cn, cm = jnp.floor(n / chunk), jnp.floor(mm / chunk)
    dist = jnp.where(cn == cm, jnp.abs(n - mm), n - mm)
    dmat = jnp.where((cm <= cn)[None], jnp.exp(dist[None] * log_g[:, None, None]), 0.0)
    qd = jnp.exp((idx + 1.0)[None, :, None] * log_g[:, None, None])
    kd = jnp.exp((blk - 1.0 - idx)[None, :, None] * log_g[:, None, None])
    bd = jnp.exp(blk * log_g)[:, None, None]
    return dmat, qd, kd, bd


def retention_mixer(q, k, vg, *, batch, heads=RET_HEADS, blk=RET_BLOCK, chunk=CHUNK):
    m = q.shape[0]
    dk = q.shape[1] // heads
    dv = vg.shape[1] // (2 * heads)
    seq = m // batch
    nblk = seq // blk
    dmat, qd, kd, bd = _retention_tables(heads, blk, chunk)
    rows = lambda w, off: pl.BlockSpec((blk, w), lambda b, h, c: (b * nblk + c, h + off))
    return pl.pallas_call(
        _ret_body,
        grid=(batch, heads, nblk),
        in_specs=[rows(dk, 0), rows(dk, 0), rows(dv, 0), rows(dv, heads),
                  pl.BlockSpec((None, blk, blk), lambda b, h, c: (h, 0, 0)),
                  pl.BlockSpec((None, blk, 1), lambda b, h, c: (h, 0, 0)),
                  pl.BlockSpec((None, blk, 1), lambda b, h, c: (h, 0, 0)),
                  pl.BlockSpec((None, 1, 1), lambda b, h, c: (h, 0, 0))],
        out_specs=rows(dv, 0),
        out_shape=jax.ShapeDtypeStruct((m, heads * dv), BF16),
        scratch_shapes=[pltpu.VMEM((dk, dv), F32)],
        compiler_params=_cparams(("parallel", "parallel", "arbitrary")),
        name="retention_mixer",
    )(q, k, vg, vg, dmat, qd, kd, bd)


def _router_body(h_ref, w_ref, idx_ref, wt_ref, *, n_experts):
    logits = jnp.dot(h_ref[...], w_ref[...], preferred_element_type=F32, precision=HIGHEST)
    lane = lax.broadcasted_iota(jnp.int32, logits.shape, 1)
    logits = jnp.where(lane < n_experts, logits, -jnp.inf)
    v1 = jnp.max(logits, axis=-1, keepdims=True)
    i1 = jnp.min(jnp.where(logits == v1, lane, LANES), axis=-1, keepdims=True)
    rest = jnp.where(lane == i1, -jnp.inf, logits)
    v2 = jnp.max(rest, axis=-1, keepdims=True)
    i2 = jnp.min(jnp.where(rest == v2, lane, LANES), axis=-1, keepdims=True)
    e2 = jnp.exp(v2 - v1)
    w1 = 1.0 / (1.0 + e2)
    w2 = e2 / (1.0 + e2)
    idx_ref[...] = jnp.where(lane == 0, i1, jnp.where(lane == 1, i2, 0))
    wt_ref[...] = jnp.where(lane == 0, w1, jnp.where(lane == 1, w2, 0.0))


def moe_router(h, w_router, tm=512):
    m, d = h.shape
    n_experts = w_router.shape[1]
    w_pad = jnp.zeros((d, LANES), F32).at[:, :n_experts].set(w_router)
    row = pl.BlockSpec((tm, LANES), lambda i: (i, 0))
    idx, wt = pl.pallas_call(
        functools.partial(_router_body, n_experts=n_experts),
        grid=(m // tm,),
        in_specs=[pl.BlockSpec((tm, d), lambda i: (i, 0)), pl.BlockSpec((d, LANES), lambda i: (0, 0))],
        out_specs=[row, row],
        out_shape=[jax.ShapeDtypeStruct((m, LANES), jnp.int32), jax.ShapeDtypeStruct((m, LANES), F32)],
        compiler_params=_cparams(("parallel",)),
        name="moe_router",
    )(h, w_pad)
    return idx[:, :TOP_K], wt[:, :TOP_K]


def _group_start(te_ref, t):
    return jnp.logical_or(t == 0, te_ref[t] != te_ref[jnp.maximum(t - 1, 0)])


def _moe_up_body(te_ref, nv_ref, x_ref, wg_ref, wu_ref, o_ref, wgb_ref, wub_ref):
    t = pl.program_id(1)

    @pl.when(t < nv_ref[0])
    def _():
        first = _group_start(te_ref, t)
        x = x_ref[...]
        g = jnp.dot(x, _weight_bf16(wg_ref, wgb_ref, first), preferred_element_type=F32)
        u = jnp.dot(x, _weight_bf16(wu_ref, wub_ref, first), preferred_element_type=F32)
        o_ref[...] = (_silu(g) * u).astype(o_ref.dtype)

    @pl.when(t >= nv_ref[0])
    def _():
        o_ref[...] = jnp.zeros_like(o_ref)


def _moe_down_body(te_ref, nv_ref, a_ref, w_ref, rw_ref, o_ref, wb_ref):
    t = pl.program_id(1)

    @pl.when(t < nv_ref[0])
    def _():
        w = _weight_bf16(w_ref, wb_ref, _group_start(te_ref, t))
        y = jnp.dot(a_ref[...], w, preferred_element_type=F32)
        o_ref[...] = (y * rw_ref[...]).astype(o_ref.dtype)

    @pl.when(t >= nv_ref[0])
    def _():
        o_ref[...] = jnp.zeros_like(o_ref)


def moe_experts(x_sorted, tile_expert, n_valid, row_w, w_gate, w_up, w_down, tm=MOE_TM, tn=512):
    p, d = x_sorted.shape
    f = w_gate.shape[2]
    nt = p // tm
    up_spec = pltpu.PrefetchScalarGridSpec(
        num_scalar_prefetch=2,
        grid=(f // tn, nt),
        in_specs=[pl.BlockSpec((tm, d), lambda j, t, te, nv: (t, 0)),
                  pl.BlockSpec((None, d, tn), lambda j, t, te, nv: (te[t], 0, j)),
                  pl.BlockSpec((None, d, tn), lambda j, t, te, nv: (te[t], 0, j))],
        out_specs=pl.BlockSpec((tm, tn), lambda j, t, te, nv: (t, j)),
        scratch_shapes=[pltpu.VMEM((d, tn), BF16), pltpu.VMEM((d, tn), BF16)],
    )
    act = pl.pallas_call(
        _moe_up_body, grid_spec=up_spec,
        out_shape=jax.ShapeDtypeStruct((p, f), BF16),
        compiler_params=_cparams(("parallel", "arbitrary")),
        name="moe_up",
    )(tile_expert, n_valid, x_sorted, w_gate, w_up)
    down_spec = pltpu.PrefetchScalarGridSpec(
        num_scalar_prefetch=2,
        grid=(d // tn, nt),
        in_specs=[pl.BlockSpec((tm, f), lambda j, t, te, nv: (t, 0)),
                  pl.BlockSpec((None, f, tn), lambda j, t, te, nv: (te[t], 0, j)),
                  pl.BlockSpec((tm, 1), lambda j, t, te, nv: (t, 0))],
        out_specs=pl.BlockSpec((tm, tn), lambda j, t, te, nv: (t, j)),
        scratch_shapes=[pltpu.VMEM((f, tn), BF16)],
    )
    return pl.pallas_call(
        _moe_down_body, grid_spec=down_spec,
        out_shape=jax.ShapeDtypeStruct((p, d), F32),
        compiler_params=_cparams(("parallel", "arbitrary")),
        name="moe_down",
    )(tile_expert, n_valid, act, w_down, row_w)


def moe_routing(top_i, top_w, n_experts, tm):
    m = top_i.shape[0]
    pairs = m * TOP_K
    p = pairs + n_experts * tm
    flat_e = top_i.reshape(pairs)
    onehot = (flat_e[:, None] == jnp.arange(n_experts, dtype=jnp.int32)[None, :]).astype(jnp.int32)
    rank = jnp.sum((jnp.cumsum(onehot, axis=0) - onehot) * onehot, axis=1)
    counts = jnp.sum(onehot, axis=0)
    padded = ((counts + tm - 1) // tm) * tm
    ends = jnp.cumsum(padded)
    starts = ends - padded
    dest = starts[flat_e] + rank
    row_token = jnp.zeros((p,), jnp.int32).at[dest].set(jnp.arange(pairs, dtype=jnp.int32) // TOP_K)
    row_w = jnp.zeros((p,), F32).at[dest].set(top_w.reshape(pairs)).reshape(p, 1)
    tile_start = jnp.arange(p // tm, dtype=jnp.int32) * tm
    tile_expert = jnp.minimum(jnp.sum((tile_start[:, None] >= ends[None, :]).astype(jnp.int32), axis=1),
                              n_experts - 1).astype(jnp.int32)
    n_valid = (ends[-1] // tm).astype(jnp.int32).reshape(1)
    return row_token, row_w, dest.reshape(m, TOP_K), tile_expert, n_valid


def _rotary_tables(inv_freq, seq, blocks, scale):
    ang = jnp.arange(seq, dtype=F32)[:, None] * inv_freq[None, :]
    cos, sin, zero = jnp.cos(ang) * scale, jnp.sin(ang) * scale, jnp.zeros_like(ang)
    rep = lambda a, b: jnp.tile(jnp.concatenate([a, b], axis=1), (1, blocks))
    return rep(cos, cos), rep(-sin, zero), rep(zero, sin), inv_freq.shape[0]


def _layer_even(hb, batch, seq, w_in, conv_w, conv_b, dt_bias, a_log, d_skip, ssd_norm_w,
                lam_q1, lam_k1, lam_q2, lam_k2, diff_norm_w, w_out, lambda_init):
    zx_cols = SSD_INNER + SSD_XBC
    qkv_lo = zx_cols + SSD_HEADS
    zx = project([hb], w_in, F32, 1024, 512, n=zx_cols)
    w_dt = jnp.zeros((D_MODEL, LANES), BF16).at[:, :SSD_HEADS].set(w_in[:, zx_cols:qkv_lo].astype(BF16))
    dt_raw = project([hb], w_dt, F32, 1024, LANES)
    w_qkv = w_in[:, qkv_lo:].astype(BF16)
    inv_freq = ROPE_THETA ** (-jnp.arange(0, DIFF_QK_DIM, 2, dtype=F32) / DIFF_QK_DIM)
    rot_q = _rotary_tables(inv_freq, seq, 2, DIFF_QK_DIM ** -0.5 * math.log2(math.e))
    rot_k = _rotary_tables(inv_freq, seq, 2, 1.0)
    q = project([hb], w_qkv, BF16, 1024, 512, n=DIFF_WIDTH, rot=rot_q, seq=seq)
    k = project([hb], w_qkv, BF16, 1024, 512, col_lo=DIFF_WIDTH, n=DIFF_WIDTH, rot=rot_k, seq=seq)
    v = project([hb], w_qkv, BF16, 1024, 512, col_lo=2 * DIFF_WIDTH, n=DIFF_WIDTH)
    y = ssd_mixer(zx, dt_raw, conv_w, conv_b, dt_bias, -jnp.exp(a_log), jnp.repeat(d_skip, SSD_HEAD_DIM),
                  ssd_norm_w, batch=batch)
    lam = (jnp.exp(jnp.sum(lam_q1 * lam_k1)) - jnp.exp(jnp.sum(lam_q2 * lam_k2)) + lambda_init)
    o = diff_attention(q, k, v, lam, diff_norm_w, 1.0 - lambda_init, batch=batch)
    return project([y, o], w_out.astype(BF16), F32, 512, 512)


def _layer_odd_mixer(hb, batch, seq, w_in, w_out):
    inv_freq = 1.0 / (ROPE_THETA ** jnp.linspace(0.0, 1.0, RET_QK_DIM // 2, dtype=F32))
    rot_q = _rotary_tables(inv_freq, seq, 1, 1.0)
    rot_k = _rotary_tables(inv_freq, seq, 1, RET_QK_DIM ** -0.5)
    q = project([hb], w_in, BF16, 1024, 512, n=RET_QK_WIDTH, rot=rot_q, seq=seq)
    k = project([hb], w_in, BF16, 1024, 512, col_lo=RET_QK_WIDTH, n=RET_QK_WIDTH, rot=rot_k, seq=seq)
    vg = project([hb], w_in, BF16, 1024, 512, col_lo=2 * RET_QK_WIDTH, n=2 * RET_V_WIDTH)
    o = retention_mixer(q, k, vg, batch=batch)
    return project([o], w_out.astype(BF16), F32, 512, 512)


def _moe(h, hb, w_router, w_gate, w_up, w_down):
    top_i, top_w = moe_router(h, w_router)
    row_token, row_w, pos, tile_expert, n_valid = moe_routing(top_i, top_w, N_EXPERTS, MOE_TM)
    x_sorted = jnp.take(hb, row_token, axis=0)
    y = moe_experts(x_sorted, tile_expert, n_valid, row_w, w_gate, w_up, w_down)
    return jnp.take(y, pos[:, 0], axis=0) + jnp.take(y, pos[:, 1], axis=0)


def kernel(x, ab_w_in, ab_conv_w, ab_conv_b, ab_dt_bias, ab_a_log, ab_d_skip, ab_ssd_norm_w, ab_lam_q1, ab_lam_k1, ab_lam_q2, ab_lam_k2, ab_diff_norm_w, ab_w_out, ffn_w_gate, ffn_w_up, ffn_w_down, c_w_in, c_w_out, moe_w_router, moe_w_gate, moe_w_up, moe_w_down, ln_mix_g, ln_mix_b, ln_ffn_g, ln_ffn_b):
    batch, seq, d = x.shape
    m = batch * seq
    h = x.reshape(m, d)
    hb = h.astype(BF16)
    lambda_init = 0.8 - 0.6 * math.exp(-0.3 * 0)
    mix = _layer_even(hb, batch, seq, ab_w_in[0], ab_conv_w[0], ab_conv_b[0], ab_dt_bias[0], ab_a_log[0],
                      ab_d_skip[0], ab_ssd_norm_w[0], ab_lam_q1[0], ab_lam_k1[0], ab_lam_q2[0], ab_lam_k2[0],
                      ab_diff_norm_w[0], ab_w_out[0], lambda_init)
    h, hb = deepnorm_ln(h, mix, ln_mix_g[0], ln_mix_b[0])
    act = swiglu_up(hb, ffn_w_gate[0], ffn_w_up[0], 1024, 256)
    ff = matmul(act, ffn_w_down[0].astype(BF16), F32, 512, 256)
    h, hb = deepnorm_ln(h, ff, ln_ffn_g[0], ln_ffn_b[0])
    mix = _layer_odd_mixer(hb, batch, seq, c_w_in[0], c_w_out[0])
    h, hb = deepnorm_ln(h, mix, ln_mix_g[1], ln_mix_b[1])
    ff = _moe(h, hb, moe_w_router[0], moe_w_gate[0], moe_w_up[0], moe_w_down[0])
    h, _ = deepnorm_ln(h, ff, ln_ffn_g[1], ln_ffn_b[1])
    return h.reshape(batch, seq, d)
```

```python
import functools
import math

import jax
import jax.numpy as jnp
from jax import lax
from jax.experimental import pallas as pl
from jax.experimental.pallas import tpu as pltpu

F32 = jnp.float32
BF16 = jnp.bfloat16
HIGHEST = lax.Precision.HIGHEST

D_MODEL = 4096
DEPTH = 2
CHUNK = 64
SSD_HEADS = 64
SSD_HEAD_DIM = 64
SSD_INNER = SSD_HEADS * SSD_HEAD_DIM
SSD_GROUPS = 8
SSD_STATE = 128
SSD_CONV = 4
SSD_BC = SSD_GROUPS * SSD_STATE
SSD_XBC = SSD_INNER + 2 * SSD_BC
DIFF_HEADS = 32
DIFF_QK_DIM = 64
DIFF_V_DIM = 2 * DIFF_QK_DIM
DIFF_WIDTH = DIFF_HEADS * DIFF_V_DIM
ROPE_THETA = 10000.0
RET_HEADS = 16
RET_QK_DIM = D_MODEL // RET_HEADS
RET_V_DIM = 2 * RET_QK_DIM
RET_QK_WIDTH = RET_HEADS * RET_QK_DIM
RET_V_WIDTH = RET_HEADS * RET_V_DIM
FFN_DIM = 11008
N_EXPERTS = 8
TOP_K = 2
EXPERT_DIM = 4096
DEEPNORM_ALPHA = (2 * DEPTH) ** 0.25
LN_EPS = 1e-5
RMS_EPS = 1e-5

LANES = 128
SUBLANES = 8
VMEM_LIMIT_BYTES = 56 * 1024 * 1024

SSD_BLOCK = 128
RET_BLOCK = 256
ATTN_BLOCK = 256
ATTN_HEADS_PER_STEP = 4
MOE_TM = 512


def _cparams(sem):
    return pltpu.CompilerParams(dimension_semantics=sem, vmem_limit_bytes=VMEM_LIMIT_BYTES)


def _silu(x):
    return x * (1.0 / (1.0 + jnp.exp(-x)))


def _weight_bf16(w_ref, wb_ref, first):
    if wb_ref is None:
        return w_ref[...]

    @pl.when(first)
    def _():
        wb_ref[...] = w_ref[...].astype(BF16)

    return wb_ref[...]


def _rotate(acc, cos_ref, sa_ref, sb_ref, half):
    tn = acc.shape[1]
    reps = tn // cos_ref.shape[1]
    tile = lambda r: jnp.tile(r[...], (1, reps))
    return (acc * tile(cos_ref) + pltpu.roll(acc, tn - half, 1) * tile(sa_ref)
            + pltpu.roll(acc, half, 1) * tile(sb_ref))


def _proj_body(*refs, n_a, rot_half, cast):
    a_refs = refs[:n_a]
    w_ref = refs[n_a]
    pos = n_a + 1
    if rot_half:
        cos_ref, sa_ref, sb_ref = refs[pos:pos + 3]
        pos += 3
    o_ref = refs[pos]
    wb_ref = refs[pos + 1] if cast else None
    w = _weight_bf16(w_ref, wb_ref, pl.program_id(1) == 0)
    k0 = 0
    acc = None
    for a_ref in a_refs:
        ka = a_ref.shape[1]
        part = jnp.dot(a_ref[...], w[k0:k0 + ka, :], preferred_element_type=F32)
        acc = part if acc is None else acc + part
        k0 += ka
    if rot_half:
        acc = _rotate(acc, cos_ref, sa_ref, sb_ref, rot_half)
    o_ref[...] = acc.astype(o_ref.dtype)


def project(a_list, w, out_dtype, tm, tn, *, col_lo=0, n=None, rot=None, seq=None):
    m = a_list[0].shape[0]
    k = w.shape[0]
    n = w.shape[1] - col_lo if n is None else n
    assert m % tm == 0 and n % tn == 0 and col_lo % tn == 0 and sum(a.shape[1] for a in a_list) == k
    off = col_lo // tn
    cast = w.dtype != BF16
    in_specs = [pl.BlockSpec((tm, a.shape[1]), lambda j, i: (i, 0)) for a in a_list]
    in_specs.append(pl.BlockSpec((k, tn), lambda j, i: (0, j + off)))
    args = list(a_list) + [w]
    rot_half = 0
    if rot is not None:
        cos, sa, sb, rot_half = rot
        nrow = seq // tm
        tab = pl.BlockSpec((tm, cos.shape[1]), lambda j, i: (i % nrow, 0))
        in_specs += [tab, tab, tab]
        args += [cos, sa, sb]
    return pl.pallas_call(
        functools.partial(_proj_body, n_a=len(a_list), rot_half=rot_half, cast=cast),
        grid=(n // tn, m // tm),
        in_specs=in_specs,
        out_specs=pl.BlockSpec((tm, tn), lambda j, i: (i, j)),
        out_shape=jax.ShapeDtypeStruct((m, n), out_dtype),
        scratch_shapes=[pltpu.VMEM((k, tn), BF16)] if cast else [],
        compiler_params=_cparams(("parallel", "arbitrary")),
        name="project",
    )(*args)


def _gate_up(a, wg_ref, wu_ref, wb_ref, first):
    tn = wg_ref.shape[1]

    @pl.when(first)
    def _():
        wb_ref[:, :tn] = wg_ref[...].astype(BF16)
        wb_ref[:, tn:] = wu_ref[...].astype(BF16)

    gu = jnp.dot(a, wb_ref[...], preferred_element_type=F32)
    return _silu(gu[:, :tn]) * gu[:, tn:]


def _swiglu_body(a_ref, wg_ref, wu_ref, o_ref, wb_ref):
    act = _gate_up(a_ref[...], wg_ref, wu_ref, wb_ref, pl.program_id(1) == 0)
    o_ref[...] = act.astype(o_ref.dtype)


def swiglu_up(a, wg, wu, tm, tn):
    m, k = a.shape
    n = wg.shape[1]
    assert m % tm == 0 and n % tn == 0
    wspec = pl.BlockSpec((k, tn), lambda j, i: (0, j))
    return pl.pallas_call(
        _swiglu_body,
        grid=(n // tn, m // tm),
        in_specs=[pl.BlockSpec((tm, k), lambda j, i: (i, 0)), wspec, wspec],
        out_specs=pl.BlockSpec((tm, tn), lambda j, i: (i, j)),
        out_shape=jax.ShapeDtypeStruct((m, n), BF16),
        scratch_shapes=[pltpu.VMEM((k, 2 * tn), BF16)],
        compiler_params=_cparams(("parallel", "arbitrary")),
        name="swiglu_up",
    )(a, wg, wu)


def _mm_body(a_ref, w_ref, o_ref):
    o_ref[...] = jnp.dot(a_ref[...], w_ref[...], preferred_element_type=F32).astype(o_ref.dtype)


def matmul(a, w, out_dtype, tm, tn):
    m, k = a.shape
    n = w.shape[1]
    assert m % tm == 0 and n % tn == 0, (m, n, tm, tn)
    return pl.pallas_call(
        _mm_body,
        grid=(m // tm, n // tn),
        in_specs=[pl.BlockSpec((tm, k), lambda i, j: (i, 0)),
                  pl.BlockSpec((k, tn), lambda i, j: (0, j))],
        out_specs=pl.BlockSpec((tm, tn), lambda i, j: (i, j)),
        out_shape=jax.ShapeDtypeStruct((m, n), out_dtype),
        compiler_params=_cparams(("parallel", "arbitrary")),
        name="matmul",
    )(a, w)


def _ln_body(h_ref, y_ref, g_ref, b_ref, o_ref, ob_ref):
    x = DEEPNORM_ALPHA * h_ref[...] + y_ref[...]
    mu = jnp.mean(x, axis=-1, keepdims=True)
    xc = x - mu
    var = jnp.mean(xc * xc, axis=-1, keepdims=True)
    out = xc * lax.rsqrt(var + LN_EPS) * g_ref[...] + b_ref[...]
    o_ref[...] = out
    ob_ref[...] = out.astype(BF16)


def deepnorm_ln(h, y, g, b, tm=256):
    m, d = h.shape
    row = pl.BlockSpec((tm, d), lambda i: (i, 0))
    vec = pl.BlockSpec((1, d), lambda i: (0, 0))
    return pl.pallas_call(
        _ln_body,
        grid=(m // tm,),
        in_specs=[row, row, vec, vec],
        out_specs=[row, row],
        out_shape=[jax.ShapeDtypeStruct((m, d), F32), jax.ShapeDtypeStruct((m, d), BF16)],
        compiler_params=_cparams(("parallel",)),
        name="deepnorm_ln",
    )(h, y, g.reshape(1, d), b.reshape(1, d))


def _row_copy(src_ref, src_row, dst_ref, dst_row, sem):
    return pltpu.make_async_copy(src_ref.at[pl.ds(src_row, 1), :], dst_ref.at[pl.ds(dst_row, 1), :], sem)


def _start_rows(src_ref, idx_ref, base, n, dst_ref, dst_base, sem):
    def body(r, carry):
        _row_copy(src_ref, idx_ref[base + r], dst_ref, dst_base + r, sem).start()
        return carry

    lax.fori_loop(0, n, body, 0, unroll=8)


def _wait_rows(src_ref, n, dst_ref, dst_base, sem):
    def body(r, carry):
        _row_copy(src_ref, 0, dst_ref, dst_base + r, sem).wait()
        return carry

    lax.fori_loop(0, n, body, 0, unroll=8)


def _gather_body(tok_ref, src_ref, o_ref, buf_ref, sem_ref):
    rows = o_ref.shape[0]
    i = pl.program_id(0)
    slot = i % 2

    @pl.when(i == 0)
    def _():
        _start_rows(src_ref, tok_ref, 0, rows, buf_ref.at[0], 0, sem_ref.at[0])

    @pl.when(i + 1 < pl.num_programs(0))
    def _():
        _start_rows(src_ref, tok_ref, (i + 1) * rows, rows, buf_ref.at[1 - slot], 0, sem_ref.at[1 - slot])

    _wait_rows(src_ref, rows, buf_ref.at[slot], 0, sem_ref.at[slot])
    o_ref[...] = buf_ref[slot].astype(o_ref.dtype)


def gather_rows(src, row_idx, out_dtype, rows=256):
    p = row_idx.shape[0]
    d = src.shape[1]
    assert p % rows == 0
    spec = pltpu.PrefetchScalarGridSpec(
        num_scalar_prefetch=1,
        grid=(p // rows,),
        in_specs=[pl.BlockSpec(memory_space=pl.ANY)],
        out_specs=pl.BlockSpec((rows, d), lambda i, tok: (i, 0)),
        scratch_shapes=[pltpu.VMEM((2, rows, d), src.dtype), pltpu.SemaphoreType.DMA((2,))],
    )
    return pl.pallas_call(
        _gather_body, grid_spec=spec,
        out_shape=jax.ShapeDtypeStruct((p, d), out_dtype),
        compiler_params=_cparams(("arbitrary",)),
        name="gather_rows",
    )(row_idx, src)


def _combine_ln_body(p0_ref, p1_ref, h_ref, y_ref, g_ref, b_ref, o_ref, buf_ref, sem_ref):
    rows = h_ref.shape[0]
    i = pl.program_id(0)
    slot = i % 2

    def start(step, s):
        _start_rows(y_ref, p0_ref, step * rows, rows, buf_ref.at[s], 0, sem_ref.at[s])
        _start_rows(y_ref, p1_ref, step * rows, rows, buf_ref.at[s], rows, sem_ref.at[s])

    @pl.when(i == 0)
    def _():
        start(0, 0)

    @pl.when(i + 1 < pl.num_programs(0))
    def _():
        start(i + 1, 1 - slot)

    _wait_rows(y_ref, 2 * rows, buf_ref.at[slot], 0, sem_ref.at[slot])
    x = DEEPNORM_ALPHA * h_ref[...] + (buf_ref[slot, 0:rows, :] + buf_ref[slot, rows:2 * rows, :])
    mu = jnp.mean(x, axis=-1, keepdims=True)
    xc = x - mu
    var = jnp.mean(xc * xc, axis=-1, keepdims=True)
    o_ref[...] = xc * lax.rsqrt(var + LN_EPS) * g_ref[...] + b_ref[...]


def combine_deepnorm_ln(h, y, pos0, pos1, g, b, rows=128):
    m, d = h.shape
    assert m % rows == 0
    row = pl.BlockSpec((rows, d), lambda i, p0, p1: (i, 0))
    vec = pl.BlockSpec((1, d), lambda i, p0, p1: (0, 0))
    spec = pltpu.PrefetchScalarGridSpec(
        num_scalar_prefetch=2,
        grid=(m // rows,),
        in_specs=[row, pl.BlockSpec(memory_space=pl.ANY), vec, vec],
        out_specs=row,
        scratch_shapes=[pltpu.VMEM((2, 2 * rows, d), y.dtype), pltpu.SemaphoreType.DMA((2,))],
    )
    return pl.pallas_call(
        _combine_ln_body, grid_spec=spec,
        out_shape=jax.ShapeDtypeStruct((m, d), F32),
        compiler_params=_cparams(("arbitrary",)),
        name="combine_deepnorm_ln",
    )(pos0, pos1, h, y, g.reshape(1, d), b.reshape(1, d))


def _conv_silu(raw_ref, stage_ref, w_ref, b_ref, first):
    blk = raw_ref.shape[0]
    kk = w_ref.shape[0]

    @pl.when(first)
    def _():
        stage_ref[0:SUBLANES, :] = jnp.zeros((SUBLANES, stage_ref.shape[1]), F32)

    stage_ref[SUBLANES:SUBLANES + blk, :] = raw_ref[...]
    y = b_ref[...]
    for i in range(kk):
        y = y + w_ref[i:i + 1, :] * stage_ref[pl.ds(SUBLANES - (kk - 1) + i, blk), :]
    stage_ref[0:SUBLANES, :] = stage_ref[blk:blk + SUBLANES, :]
    return _silu(y)


def _ssd_body(z_ref, xr_ref, br_ref, cr_ref, dtr_ref, cwx_ref, cwb_ref, cwc_ref, cbx_ref, cbb_ref, cbc_ref,
              dtb_ref, aneg_ref, dskip_ref, nw_ref, o_ref,
              state_ref, y_ref, sx_ref, sb_ref, sc_ref, *, heads, head_dim, groups, n_state):
    blk = xr_ref.shape[0]
    hpg = heads // groups
    gw = hpg * head_dim
    inner = heads * head_dim
    first = pl.program_id(1) == 0

    @pl.when(first)
    def _():
        state_ref[...] = jnp.zeros_like(state_ref)

    xs = _conv_silu(xr_ref, sx_ref, cwx_ref, cbx_ref, first)
    bmat = _conv_silu(br_ref, sb_ref, cwb_ref, cbb_ref, first).astype(BF16)
    cmat = _conv_silu(cr_ref, sc_ref, cwc_ref, cbc_ref, first).astype(BF16)

    row = lax.broadcasted_iota(jnp.int32, (blk, blk), 0)
    col = lax.broadcasted_iota(jnp.int32, (blk, blk), 1)
    causal = row >= col
    eh = lax.broadcasted_iota(jnp.int32, (heads, inner), 0)
    ec = lax.broadcasted_iota(jnp.int32, (heads, inner), 1)
    expand = (ec // head_dim == eh).astype(F32)

    dtp = dtr_ref[:, :heads] + dtb_ref[...]
    dt = jnp.maximum(dtp, 0.0) + jnp.log(1.0 + jnp.exp(-jnp.abs(dtp)))
    a = dt * aneg_ref[...]
    cs = jnp.dot(causal.astype(F32), a, preferred_element_type=F32, precision=HIGHEST)
    cs_t = cs.T
    cs_x = jnp.dot(cs, expand, preferred_element_type=F32, precision=HIGHEST)
    dt_x = jnp.dot(dt, expand, preferred_element_type=F32, precision=HIGHEST)
    xdt = xs * dt_x
    last_x = cs_x[blk - 1:blk, :]
    x_in = (xdt * jnp.exp(last_x - cs_x)).astype(BF16)
    x_b = xdt.astype(BF16)
    grow = jnp.exp(cs_x)

    for g in range(groups):
        bg = bmat[:, g * n_state:(g + 1) * n_state]
        cg = cmat[:, g * n_state:(g + 1) * n_state]
        cb = lax.dot_general(cg, bg, (((1,), (1,)), ((), ())), preferred_element_type=F32)
        st = state_ref[g]
        y_off = jnp.dot(cg, st.astype(BF16), preferred_element_type=F32)
        y_ref[:, g * gw:(g + 1) * gw] = y_off * grow[:, g * gw:(g + 1) * gw]
        for r in range(hpg):
            h = g * hpg + r
            seg = cs[:, h:h + 1] - cs_t[h:h + 1, :]
            dec = jnp.exp(jnp.where(causal, seg, -jnp.inf))
            mh = (cb * dec).astype(BF16)
            yh = jnp.dot(mh, x_b[:, h * head_dim:(h + 1) * head_dim], preferred_element_type=F32)
            y_ref[:, h * head_dim:(h + 1) * head_dim] += yh
        new = lax.dot_general(bg, x_in[:, g * gw:(g + 1) * gw], (((0,), (0,)), ((), ())),
                              preferred_element_type=F32)
        state_ref[g] = st * jnp.exp(last_x[:, g * gw:(g + 1) * gw]) + new

    y = y_ref[...] + dskip_ref[...] * xs
    y = y * _silu(z_ref[...])
    for g in range(groups):
        yg = y[:, g * gw:(g + 1) * gw]
        ms = jnp.mean(yg * yg, axis=-1, keepdims=True)
        o_ref[:, g * gw:(g + 1) * gw] = (yg * lax.rsqrt(ms + RMS_EPS)
                                         * nw_ref[:, g * gw:(g + 1) * gw]).astype(o_ref.dtype)


def ssd_mixer(zx, dt_raw, conv_w, conv_b, dt_bias, a_neg, d_skip_x, norm_w, *, batch, heads=SSD_HEADS,
              head_dim=SSD_HEAD_DIM, groups=SSD_GROUPS, n_state=SSD_STATE, blk=SSD_BLOCK):
    m = zx.shape[0]
    inner = heads * head_dim
    bc = groups * n_state
    seq = m // batch
    nblk = seq // blk
    gw = inner // groups
    kk = conv_w.shape[0]
    rows = lambda w, cb: pl.BlockSpec((blk, w), lambda b, c: (b * nblk + c, cb))
    full = lambda r, w: pl.BlockSpec((r, w), lambda b, c: (0, 0))
    body = functools.partial(_ssd_body, heads=heads, head_dim=head_dim, groups=groups, n_state=n_state)
    cw = (conv_w[:, :inner], conv_w[:, inner:inner + bc], conv_w[:, inner + bc:])
    cb = tuple(t.reshape(1, -1) for t in (conv_b[:inner], conv_b[inner:inner + bc], conv_b[inner + bc:]))
    return pl.pallas_call(
        body,
        grid=(batch, nblk),
        in_specs=[rows(inner, 0), rows(inner, 1), rows(bc, 2 * inner // bc), rows(bc, 2 * inner // bc + 1),
                  rows(dt_raw.shape[1], 0),
                  full(kk, inner), full(kk, bc), full(kk, bc), full(1, inner), full(1, bc), full(1, bc),
                  full(1, heads), full(1, heads), full(1, inner), full(1, inner)],
        out_specs=rows(inner, 0),
        out_shape=jax.ShapeDtypeStruct((m, inner), BF16),
        scratch_shapes=[pltpu.VMEM((groups, n_state, gw), F32), pltpu.VMEM((blk, inner), F32),
                        pltpu.VMEM((blk + SUBLANES, inner), F32), pltpu.VMEM((blk + SUBLANES, bc), F32),
                        pltpu.VMEM((blk + SUBLANES, bc), F32)],
        compiler_params=_cparams(("parallel", "arbitrary")),
        name="ssd_mixer",
    )(zx, zx, zx, zx, dt_raw, *cw, *cb, dt_bias.reshape(1, heads), a_neg.reshape(1, heads),
      d_skip_x.reshape(1, inner), norm_w.reshape(1, inner))


def _attn_body(lam_ref, q_ref, k_ref, v_ref, nw_ref, o_ref, acc_ref, *, post_scale, chunk, hd):
    t = q_ref.shape[0]
    hps = q_ref.shape[1] // hd
    half = hd // 2
    i = pl.program_id(2)
    sub = lax.broadcasted_iota(jnp.int32, (hd, t), 0)
    qs_t = []
    for hh in range(hps):
        qt = q_ref[:, hh * hd:(hh + 1) * hd].astype(F32).T
        qs_t.append(jnp.concatenate([jnp.where(sub < half, qt, 0.0), jnp.where(sub >= half, qt, 0.0)],
                                    axis=1).astype(BF16))
    acc_ref[...] = jnp.zeros_like(acc_ref)

    def scores(j, hh):
        start = pl.multiple_of(j * t, t)
        kb = k_ref[pl.ds(start, t), hh * hd:(hh + 1) * hd]
        return jnp.dot(kb, qs_t[hh], preferred_element_type=F32)

    def step(j, carry, last):
        start = pl.multiple_of(j * t, t)
        out = []
        for hh in range(hps):
            m, l, s = carry[hh]
            s_next = s if last else scores(j + 1, hh)
            vb = v_ref[pl.ds(start, t), hh * hd:(hh + 1) * hd]
            if last:
                kk = lax.broadcasted_iota(jnp.int32, (t, 2 * t), 0)
                qq = lax.broadcasted_iota(jnp.int32, (t, 2 * t), 1)
                qq = jnp.where(qq >= t, qq - t, qq)
                s = jnp.where(kk // chunk <= qq // chunk, s, -jnp.inf)
            m_new = jnp.maximum(m, jnp.max(s, axis=0, keepdims=True))
            alpha = jnp.exp2(m - m_new)
            p = jnp.exp2(s - m_new)
            l = alpha * l + jnp.sum(p, axis=0, keepdims=True)
            pv = lax.dot_general(vb, p.astype(BF16), (((0,), (0,)), ((), ())),
                                 preferred_element_type=F32)
            acc_ref[hh] = alpha * acc_ref[hh] + pv
            out.append((m_new, l, s_next))
        return tuple(out)

    init = tuple((jnp.full((1, 2 * t), -jnp.inf, F32), jnp.zeros((1, 2 * t), F32), scores(0, hh))
                 for hh in range(hps))
    carry = lax.fori_loop(0, i, lambda j, c: step(j, c, False), init)
    carry = step(i, carry, True)
    for hh in range(hps):
        o = acc_ref[hh] * (1.0 / carry[hh][1])
        out = o[:, :t] - lam_ref[0] * o[:, t:]
        ms = jnp.mean(out * out, axis=0, keepdims=True)
        out = out * lax.rsqrt(ms + RMS_EPS) * (nw_ref[...] * post_scale)
        o_ref[:, hh * hd:(hh + 1) * hd] = out.T.astype(o_ref.dtype)


def diff_attention(q, k, v, lam, norm_w, post_scale, *, batch, heads=DIFF_HEADS, blk=ATTN_BLOCK,
                   chunk=CHUNK, hps=ATTN_HEADS_PER_STEP):
    m, width = q.shape
    hd = width // heads
    seq = m // batch
    nq = seq // blk
    q3, k3, v3 = (t.reshape(batch, seq, width) for t in (q, k, v))
    body = functools.partial(_attn_body, post_scale=post_scale, chunk=chunk, hd=hd)
    out = pl.pallas_call(
        body,
        grid=(batch, heads // hps, nq),
        in_specs=[pl.BlockSpec(memory_space=pltpu.SMEM),
                  pl.BlockSpec((None, blk, hps * hd), lambda b, h, i: (b, i, h)),
                  pl.BlockSpec((None, seq, hps * hd), lambda b, h, i: (b, 0, h)),
                  pl.BlockSpec((None, seq, hps * hd), lambda b, h, i: (b, 0, h)),
                  pl.BlockSpec((hd, 1), lambda b, h, i: (0, 0))],
        out_specs=pl.BlockSpec((None, blk, hps * hd), lambda b, h, i: (b, i, h)),
        out_shape=jax.ShapeDtypeStruct((batch, seq, width), BF16),
        scratch_shapes=[pltpu.VMEM((hps, hd, 2 * blk), F32)],
        compiler_params=_cparams(("parallel", "parallel", "arbitrary")),
        name="diff_attention",
    )(lam.reshape(1).astype(F32), q3, k3, v3, norm_w.reshape(hd, 1))
    return out.reshape(m, width)


def _ret_body(q_ref, k_ref, v_ref, g_ref, dmat_ref, qd_ref, kd_ref, bd_ref, o_ref, state_ref):
    @pl.when(pl.program_id(2) == 0)
    def _():
        state_ref[...] = jnp.zeros_like(state_ref)

    qb = q_ref[...].astype(BF16)
    k = k_ref[...]
    vb = v_ref[...].astype(BF16)
    s = lax.dot_general(qb, k.astype(BF16), (((1,), (1,)), ((), ())), preferred_element_type=F32)
    y = jnp.dot((s * dmat_ref[...]).astype(BF16), vb, preferred_element_type=F32)
    st = state_ref[...]
    y = y + jnp.dot(qb, st.astype(BF16), preferred_element_type=F32) * qd_ref[...]
    kdec = (k.astype(F32) * kd_ref[...]).astype(BF16)
    state_ref[...] = st * bd_ref[...] + lax.dot_general(kdec, vb, (((0,), (0,)), ((), ())),
                                                        preferred_element_type=F32)
    mu = jnp.mean(y, axis=-1, keepdims=True)
    yc = y - mu
    var = jnp.mean(yc * yc, axis=-1, keepdims=True)
    o_ref[...] = (_silu(g_ref[...].astype(F32)) * (yc * lax.rsqrt(var + LN_EPS))).astype(o_ref.dtype)


def _retention_tables(heads, blk, chunk):
    log_g = jnp.log1p(-(2.0 ** (-5.0 - jnp.arange(heads, dtype=F32))))
    idx = jnp.arange(blk, dtype=F32)
    n, mm = idx[:, None], idx[None, :]
    cn, cm = jnp.floor(n / chunk), jnp.floor(mm / chunk)
    dist = jnp.where(cn == cm, jnp.abs(n - mm), n - mm)
    dmat = jnp.where((cm <= cn)[None], jnp.exp(dist[None] * log_g[:, None, None]), 0.0)
    qd = jnp.exp((idx + 1.0)[None, :, None] * log_g[:, None, None])
    kd = jnp.exp((blk - 1.0 - idx)[None, :, None] * log_g[:, None, None])
    bd = jnp.exp(blk * log_g)[:, None, None]
    return dmat, qd, kd, bd


def retention_mixer(q, k, vg, *, batch, heads=RET_HEADS, blk=RET_BLOCK, chunk=CHUNK):
    m = q.shape[0]
    dk = q.shape[1] // heads
    dv = vg.shape[1] // (2 * heads)
    seq = m // batch
    nblk = seq // blk
    dmat, qd, kd, bd = _retention_tables(heads, blk, chunk)
    rows = lambda w, off: pl.BlockSpec((blk, w), lambda b, h, c: (b * nblk + c, h + off))
    return pl.pallas_call(
        _ret_body,
        grid=(batch, heads, nblk),
        in_specs=[rows(dk, 0), rows(dk, 0), rows(dv, 0), rows(dv, heads),
                  pl.BlockSpec((None, blk, blk), lambda b, h, c: (h, 0, 0)),
                  pl.BlockSpec((None, blk, 1), lambda b, h, c: (h, 0, 0)),
                  pl.BlockSpec((None, blk, 1), lambda b, h, c: (h, 0, 0)),
                  pl.BlockSpec((None, 1, 1), lambda b, h, c: (h, 0, 0))],
        out_specs=rows(dv, 0),
        out_shape=jax.ShapeDtypeStruct((m, heads * dv), BF16),
        scratch_shapes=[pltpu.VMEM((dk, dv), F32)],
        compiler_params=_cparams(("parallel", "parallel", "arbitrary")),
        name="retention_mixer",
    )(q, k, vg, vg, dmat, qd, kd, bd)


def _router_body(h_ref, w_ref, idx_ref, wt_ref, *, n_experts):
    logits = jnp.dot(h_ref[...], w_ref[...], preferred_element_type=F32, precision=HIGHEST)
    lane = lax.broadcasted_iota(jnp.int32, logits.shape, 1)
    logits = jnp.where(lane < n_experts, logits, -jnp.inf)
    v1 = jnp.max(logits, axis=-1, keepdims=True)
    i1 = jnp.min(jnp.where(logits == v1, lane, LANES), axis=-1, keepdims=True)
    rest = jnp.where(lane == i1, -jnp.inf, logits)
    v2 = jnp.max(rest, axis=-1, keepdims=True)
    i2 = jnp.min(jnp.where(rest == v2, lane, LANES), axis=-1, keepdims=True)
    e2 = jnp.exp(v2 - v1)
    w1 = 1.0 / (1.0 + e2)
    w2 = e2 / (1.0 + e2)
    idx_ref[...] = jnp.where(lane == 0, i1, jnp.where(lane == 1, i2, 0))
    wt_ref[...] = jnp.where(lane == 0, w1, jnp.where(lane == 1, w2, 0.0))


def moe_router(h, w_router, tm=512):
    m, d = h.shape
    n_experts = w_router.shape[1]
    w_pad = jnp.zeros((d, LANES), F32).at[:, :n_experts].set(w_router)
    row = pl.BlockSpec((tm, LANES), lambda i: (i, 0))
    idx, wt = pl.pallas_call(
        functools.partial(_router_body, n_experts=n_experts),
        grid=(m // tm,),
        in_specs=[pl.BlockSpec((tm, d), lambda i: (i, 0)), pl.BlockSpec((d, LANES), lambda i: (0, 0))],
        out_specs=[row, row],
        out_shape=[jax.ShapeDtypeStruct((m, LANES), jnp.int32), jax.ShapeDtypeStruct((m, LANES), F32)],
        compiler_params=_cparams(("parallel",)),
        name="moe_router",
    )(h, w_pad)
    return idx[:, :TOP_K], wt[:, :TOP_K]


def _group_start(te_ref, t):
    return jnp.logical_or(t == 0, te_ref[t] != te_ref[jnp.maximum(t - 1, 0)])


def _moe_up_body(te_ref, nv_ref, x_ref, wg_ref, wu_ref, o_ref, wb_ref):
    t = pl.program_id(1)

    @pl.when(t < nv_ref[0])
    def _():
        act = _gate_up(x_ref[...], wg_ref, wu_ref, wb_ref, _group_start(te_ref, t))
        o_ref[...] = act.astype(o_ref.dtype)

    @pl.when(t >= nv_ref[0])
    def _():
        o_ref[...] = jnp.zeros_like(o_ref)


def _moe_down_body(te_ref, nv_ref, a_ref, w_ref, rw_ref, o_ref, wb_ref):
    t = pl.program_id(1)

    @pl.when(t < nv_ref[0])
    def _():
        w = _weight_bf16(w_ref, wb_ref, _group_start(te_ref, t))
        y = jnp.dot(a_ref[...], w, preferred_element_type=F32)
        o_ref[...] = (y * rw_ref[...]).astype(o_ref.dtype)

    @pl.when(t >= nv_ref[0])
    def _():
        o_ref[...] = jnp.zeros_like(o_ref)


def moe_experts(x_sorted, tile_expert, n_valid, row_w, w_gate, w_up, w_down, tm=MOE_TM, tn=512):
    p, d = x_sorted.shape
    f = w_gate.shape[2]
    nt = p // tm
    up_spec = pltpu.PrefetchScalarGridSpec(
        num_scalar_prefetch=2,
        grid=(f // tn, nt),
        in_specs=[pl.BlockSpec((tm, d), lambda j, t, te, nv: (t, 0)),
                  pl.BlockSpec((None, d, tn), lambda j, t, te, nv: (te[t], 0, j)),
                  pl.BlockSpec((None, d, tn), lambda j, t, te, nv: (te[t], 0, j))],
        out_specs=pl.BlockSpec((tm, tn), lambda j, t, te, nv: (t, j)),
        scratch_shapes=[pltpu.VMEM((d, 2 * tn), BF16)],
    )
    act = pl.pallas_call(
        _moe_up_body, grid_spec=up_spec,
        out_shape=jax.ShapeDtypeStruct((p, f), BF16),
        compiler_params=_cparams(("parallel", "arbitrary")),
        name="moe_up",
    )(tile_expert, n_valid, x_sorted, w_gate, w_up)
    down_spec = pltpu.PrefetchScalarGridSpec(
        num_scalar_prefetch=2,
        grid=(d // tn, nt),
        in_specs=[pl.BlockSpec((tm, f), lambda j, t, te, nv: (t, 0)),
                  pl.BlockSpec((None, f, tn), lambda j, t, te, nv: (te[t], 0, j)),
                  pl.BlockSpec((tm, 1), lambda j, t, te, nv: (t, 0))],
        out_specs=pl.BlockSpec((tm, tn), lambda j, t, te, nv: (t, j)),
        scratch_shapes=[pltpu.VMEM((f, tn), BF16)],
    )
    return pl.pallas_call(
        _moe_down_body, grid_spec=down_spec,
        out_shape=jax.ShapeDtypeStruct((p, d), F32),
        compiler_params=_cparams(("parallel", "arbitrary")),
        name="moe_down",
    )(tile_expert, n_valid, act, w_down, row_w)


def moe_routing(top_i, top_w, n_experts, tm):
    m = top_i.shape[0]
    pairs = m * TOP_K
    p = pairs + n_experts * tm
    flat_e = top_i.reshape(pairs)
    onehot = (flat_e[:, None] == jnp.arange(n_experts, dtype=jnp.int32)[None, :]).astype(jnp.int32)
    rank = jnp.sum((jnp.cumsum(onehot, axis=0) - onehot) * onehot, axis=1)
    counts = jnp.sum(onehot, axis=0)
    padded = ((counts + tm - 1) // tm) * tm
    ends = jnp.cumsum(padded)
    starts = ends - padded
    dest = starts[flat_e] + rank
    row_token = jnp.zeros((p,), jnp.int32).at[dest].set(jnp.arange(pairs, dtype=jnp.int32) // TOP_K)
    row_w = jnp.zeros((p,), F32).at[dest].set(top_w.reshape(pairs)).reshape(p, 1)
    tile_start = jnp.arange(p // tm, dtype=jnp.int32) * tm
    tile_expert = jnp.minimum(jnp.sum((tile_start[:, None] >= ends[None, :]).astype(jnp.int32), axis=1),
                              n_experts - 1).astype(jnp.int32)
    n_valid = (ends[-1] // tm).astype(jnp.int32).reshape(1)
    return row_token, row_w, dest.reshape(m, TOP_K), tile_expert, n_valid


def _rotary_tables(inv_freq, seq, blocks, scale):
    ang = jnp.arange(seq, dtype=F32)[:, None] * inv_freq[None, :]
    cos, sin, zero = jnp.cos(ang) * scale, jnp.sin(ang) * scale, jnp.zeros_like(ang)
    rep = lambda a, b: jnp.tile(jnp.concatenate([a, b], axis=1), (1, blocks))
    return rep(cos, cos), rep(-sin, zero), rep(zero, sin), inv_freq.shape[0]


def _layer_even(hb, batch, seq, w_in, conv_w, conv_b, dt_bias, a_log, d_skip, ssd_norm_w,
                lam_q1, lam_k1, lam_q2, lam_k2, diff_norm_w, w_out, lambda_init):
    zx_cols = SSD_INNER + SSD_XBC
    qkv_lo = zx_cols + SSD_HEADS
    zx = project([hb], w_in, F32, 1024, 512, n=zx_cols)
    w_dt = jnp.zeros((D_MODEL, LANES), BF16).at[:, :SSD_HEADS].set(w_in[:, zx_cols:qkv_lo].astype(BF16))
    dt_raw = project([hb], w_dt, F32, 1024, LANES)
    w_qkv = w_in[:, qkv_lo:].astype(BF16)
    inv_freq = ROPE_THETA ** (-jnp.arange(0, DIFF_QK_DIM, 2, dtype=F32) / DIFF_QK_DIM)
    rot_q = _rotary_tables(inv_freq, seq, 2, DIFF_QK_DIM ** -0.5 * math.log2(math.e))
    rot_k = _rotary_tables(inv_freq, seq, 2, 1.0)
    q = project([hb], w_qkv, BF16, 1024, 512, n=DIFF_WIDTH, rot=rot_q, seq=seq)
    k = project([hb], w_qkv, BF16, 1024, 512, col_lo=DIFF_WIDTH, n=DIFF_WIDTH, rot=rot_k, seq=seq)
    v = project([hb], w_qkv, BF16, 1024, 512, col_lo=2 * DIFF_WIDTH, n=DIFF_WIDTH)
    y = ssd_mixer(zx, dt_raw, conv_w, conv_b, dt_bias, -jnp.exp(a_log), jnp.repeat(d_skip, SSD_HEAD_DIM),
                  ssd_norm_w, batch=batch)
    lam = (jnp.exp(jnp.sum(lam_q1 * lam_k1)) - jnp.exp(jnp.sum(lam_q2 * lam_k2)) + lambda_init)
    o = diff_attention(q, k, v, lam, diff_norm_w, 1.0 - lambda_init, batch=batch)
    return project([y, o], w_out.astype(BF16), F32, 512, 512)


def _layer_odd_mixer(hb, batch, seq, w_in, w_out):
    inv_freq = 1.0 / (ROPE_THETA ** jnp.linspace(0.0, 1.0, RET_QK_DIM // 2, dtype=F32))
    rot_q = _rotary_tables(inv_freq, seq, 1, 1.0)
    rot_k = _rotary_tables(inv_freq, seq, 1, RET_QK_DIM ** -0.5)
    q = project([hb], w_in, BF16, 1024, 512, n=RET_QK_WIDTH, rot=rot_q, seq=seq)
    k = project([hb], w_in, BF16, 1024, 512, col_lo=RET_QK_WIDTH, n=RET_QK_WIDTH, rot=rot_k, seq=seq)
    vg = project([hb], w_in, BF16, 1024, 512, col_lo=2 * RET_QK_WIDTH, n=2 * RET_V_WIDTH)
    o = retention_mixer(q, k, vg, batch=batch)
    return project([o], w_out.astype(BF16), F32, 512, 512)


def _moe_ln(h, w_router, w_gate, w_up, w_down, ln_g, ln_b):
    top_i, top_w = moe_router(h, w_router)
    row_token, row_w, pos, tile_expert, n_valid = moe_routing(top_i, top_w, N_EXPERTS, MOE_TM)
    x_sorted = gather_rows(h, row_token, BF16)
    y = moe_experts(x_sorted, tile_expert, n_valid, row_w, w_gate, w_up, w_down)
    return combine_deepnorm_ln(h, y, pos[:, 0], pos[:, 1], ln_g, ln_b)


def kernel(x, ab_w_in, ab_conv_w, ab_conv_b, ab_dt_bias, ab_a_log, ab_d_skip, ab_ssd_norm_w, ab_lam_q1, ab_lam_k1, ab_lam_q2, ab_lam_k2, ab_diff_norm_w, ab_w_out, ffn_w_gate, ffn_w_up, ffn_w_down, c_w_in, c_w_out, moe_w_router, moe_w_gate, moe_w_up, moe_w_down, ln_mix_g, ln_mix_b, ln_ffn_g, ln_ffn_b):
    batch, seq, d = x.shape
    m = batch * seq
    h = x.reshape(m, d)
    hb = h.astype(BF16)
    lambda_init = 0.8 - 0.6 * math.exp(-0.3 * 0)
    mix = _layer_even(hb, batch, seq, ab_w_in[0], ab_conv_w[0], ab_conv_b[0], ab_dt_bias[0], ab_a_log[0],
                      ab_d_skip[0], ab_ssd_norm_w[0], ab_lam_q1[0], ab_lam_k1[0], ab_lam_q2[0], ab_lam_k2[0],
                      ab_diff_norm_w[0], ab_w_out[0], lambda_init)
    h, hb = deepnorm_ln(h, mix, ln_mix_g[0], ln_mix_b[0])
    act = swiglu_up(hb, ffn_w_gate[0], ffn_w_up[0], 1024, 256)
    ff = matmul(act, ffn_w_down[0].astype(BF16), F32, 512, 256)
    h, hb = deepnorm_ln(h, ff, ln_ffn_g[0], ln_ffn_b[0])
    mix = _layer_odd_mixer(hb, batch, seq, c_w_in[0], c_w_out[0])
    h, _ = deepnorm_ln(h, mix, ln_mix_g[1], ln_mix_b[1])
    h = _moe_ln(h, moe_w_router[0], moe_w_gate[0], moe_w_up[0], moe_w_down[0], ln_ffn_g[1], ln_ffn_b[1])
    return h.reshape(batch, seq, d)
```

```python
import functools
import math

import jax
import jax.numpy as jnp
from jax import lax
from jax.experimental import pallas as pl
from jax.experimental.pallas import tpu as pltpu

F32 = jnp.float32
BF16 = jnp.bfloat16
HIGHEST = lax.Precision.HIGHEST

D_MODEL = 4096
DEPTH = 2
CHUNK = 64
SSD_HEADS = 64
SSD_HEAD_DIM = 64
SSD_INNER = SSD_HEADS * SSD_HEAD_DIM
SSD_GROUPS = 8
SSD_STATE = 128
SSD_CONV = 4
SSD_BC = SSD_GROUPS * SSD_STATE
SSD_XBC = SSD_INNER + 2 * SSD_BC
DIFF_HEADS = 32
DIFF_QK_DIM = 64
DIFF_V_DIM = 2 * DIFF_QK_DIM
DIFF_WIDTH = DIFF_HEADS * DIFF_V_DIM
ROPE_THETA = 10000.0
RET_HEADS = 16
RET_QK_DIM = D_MODEL // RET_HEADS
RET_V_DIM = 2 * RET_QK_DIM
RET_QK_WIDTH = RET_HEADS * RET_QK_DIM
RET_V_WIDTH = RET_HEADS * RET_V_DIM
FFN_DIM = 11008
N_EXPERTS = 8
TOP_K = 2
EXPERT_DIM = 4096
DEEPNORM_ALPHA = (2 * DEPTH) ** 0.25
LN_EPS = 1e-5
RMS_EPS = 1e-5

LANES = 128
SUBLANES = 8
VMEM_LIMIT_BYTES = 56 * 1024 * 1024

SSD_BLOCK = 128
RET_BLOCK = 256
RET_HEADS_PER_STEP = 2
ATTN_BLOCK = 256
ATTN_HEADS_PER_STEP = 4
MOE_TM = 512


def _cparams(sem):
    return pltpu.CompilerParams(dimension_semantics=sem, vmem_limit_bytes=VMEM_LIMIT_BYTES)


def _silu(x):
    return x * (1.0 / (1.0 + jnp.exp(-x)))


SPLIT_ROWS = 512


def _row_splits(tm):
    return (slice(0, tm // 2), slice(tm // 2, tm)) if tm >= SPLIT_ROWS else (slice(0, tm),)


def _weight_bf16(w_ref, wb_ref, first):
    if wb_ref is None:
        return w_ref

    @pl.when(first)
    def _():
        wb_ref[...] = w_ref[...].astype(BF16)

    return wb_ref


def _rotate(acc, cos_ref, sa_ref, sb_ref, half, rows):
    tn = acc.shape[1]
    reps = tn // cos_ref.shape[1]
    tile = lambda r: jnp.tile(r[rows, :], (1, reps))
    return (acc * tile(cos_ref) + pltpu.roll(acc, tn - half, 1) * tile(sa_ref)
            + pltpu.roll(acc, half, 1) * tile(sb_ref))


REALIGN_ROWS = 512


def _weight_bf16_shifted(w_ref, wn_ref, wb_ref, first, shift):
    k, tn = w_ref.shape
    width = tn + wn_ref.shape[1]

    @pl.when(first)
    def _():
        for r0 in range(0, k, REALIGN_ROWS):
            rows = slice(r0, min(r0 + REALIGN_ROWS, k))
            window = jnp.concatenate([w_ref[rows, :], wn_ref[rows, :]], axis=1)
            wb_ref[rows, :] = pltpu.roll(window, width - shift, 1)[:, :tn].astype(BF16)

    return wb_ref


def _proj_body(*refs, n_a, rot_half, cast, shift):
    a_refs = refs[:n_a]
    w_ref = refs[n_a]
    pos = n_a + 1
    if shift:
        wn_ref = refs[pos]
        pos += 1
    if rot_half:
        cos_ref, sa_ref, sb_ref = refs[pos:pos + 3]
        pos += 3
    o_ref = refs[pos]
    wb_ref = refs[pos + 1] if cast else None
    first = pl.program_id(1) == 0
    if shift:
        w = _weight_bf16_shifted(w_ref, wn_ref, wb_ref, first, shift)
    else:
        w = _weight_bf16(w_ref, wb_ref, first)
    for rows in _row_splits(o_ref.shape[0]):
        k0 = 0
        acc = None
        for a_ref in a_refs:
            ka = a_ref.shape[1]
            part = jnp.dot(a_ref[rows, :], w[k0:k0 + ka, :], preferred_element_type=F32)
            acc = part if acc is None else acc + part
            k0 += ka
        if rot_half:
            acc = _rotate(acc, cos_ref, sa_ref, sb_ref, rot_half, rows)
        o_ref[rows, :] = acc.astype(o_ref.dtype)


def project(a_list, w, out_dtype, tm, tn, *, col_lo=0, n=None, rot=None, seq=None):
    m = a_list[0].shape[0]
    k = w.shape[0]
    n = w.shape[1] - col_lo if n is None else n
    shift = col_lo % LANES
    base = col_lo - shift
    cast = w.dtype != BF16
    assert m % tm == 0 and n % tn == 0 and base % tn == 0 and sum(a.shape[1] for a in a_list) == k
    assert cast or not shift
    off = base // tn
    in_specs = [pl.BlockSpec((tm, a.shape[1]), lambda j, i: (i, 0)) for a in a_list]
    in_specs.append(pl.BlockSpec((k, tn), lambda j, i: (0, j + off)))
    args = list(a_list) + [w]
    if shift:
        lanes_per_tile = tn // LANES
        in_specs.append(pl.BlockSpec((k, LANES), lambda j, i: (0, (j + off + 1) * lanes_per_tile)))
        args.append(w)
    rot_half = 0
    if rot is not None:
        cos, sa, sb, rot_half = rot
        nrow = seq // tm
        tab = pl.BlockSpec((tm, cos.shape[1]), lambda j, i: (i % nrow, 0))
        in_specs += [tab, tab, tab]
        args += [cos, sa, sb]
    return pl.pallas_call(
        functools.partial(_proj_body, n_a=len(a_list), rot_half=rot_half, cast=cast, shift=shift),
        grid=(n // tn, m // tm),
        in_specs=in_specs,
        out_specs=pl.BlockSpec((tm, tn), lambda j, i: (i, j)),
        out_shape=jax.ShapeDtypeStruct((m, n), out_dtype),
        scratch_shapes=[pltpu.VMEM((k, tn), BF16)] if cast else [],
        compiler_params=_cparams(("parallel", "arbitrary")),
        name="project",
    )(*args)


def _gate_up(a_ref, wg_ref, wu_ref, wb_ref, o_ref, first):
    tn = wg_ref.shape[1]

    @pl.when(first)
    def _():
        wb_ref[:, :tn] = wg_ref[...].astype(BF16)
        wb_ref[:, tn:] = wu_ref[...].astype(BF16)

    for rows in _row_splits(o_ref.shape[0]):
        gu = jnp.dot(a_ref[rows, :], wb_ref[...], preferred_element_type=F32)
        o_ref[rows, :] = (_silu(gu[:, :tn]) * gu[:, tn:]).astype(o_ref.dtype)


def _swiglu_body(a_ref, wg_ref, wu_ref, o_ref, wb_ref):
    _gate_up(a_ref, wg_ref, wu_ref, wb_ref, o_ref, pl.program_id(1) == 0)


def swiglu_up(a, wg, wu, tm, tn):
    m, k = a.shape
    n = wg.shape[1]
    assert m % tm == 0 and n % tn == 0
    wspec = pl.BlockSpec((k, tn), lambda j, i: (0, j))
    return pl.pallas_call(
        _swiglu_body,
        grid=(n // tn, m // tm),
        in_specs=[pl.BlockSpec((tm, k), lambda j, i: (i, 0)), wspec, wspec],
        out_specs=pl.BlockSpec((tm, tn), lambda j, i: (i, j)),
        out_shape=jax.ShapeDtypeStruct((m, n), BF16),
        scratch_shapes=[pltpu.VMEM((k, 2 * tn), BF16)],
        compiler_params=_cparams(("parallel", "arbitrary")),
        name="swiglu_up",
    )(a, wg, wu)


def _mm_body(a_ref, w_ref, o_ref):
    for rows in _row_splits(o_ref.shape[0]):
        o_ref[rows, :] = jnp.dot(a_ref[rows, :], w_ref[...], preferred_element_type=F32).astype(o_ref.dtype)


def matmul(a, w, out_dtype, tm, tn):
    m, k = a.shape
    n = w.shape[1]
    assert m % tm == 0 and n % tn == 0, (m, n, tm, tn)
    return pl.pallas_call(
        _mm_body,
        grid=(m // tm, n // tn),
        in_specs=[pl.BlockSpec((tm, k), lambda i, j: (i, 0)),
                  pl.BlockSpec((k, tn), lambda i, j: (0, j))],
        out_specs=pl.BlockSpec((tm, tn), lambda i, j: (i, j)),
        out_shape=jax.ShapeDtypeStruct((m, n), out_dtype),
        compiler_params=_cparams(("parallel", "arbitrary")),
        name="matmul",
    )(a, w)


def _ln_body(h_ref, y_ref, g_ref, b_ref, o_ref, ob_ref):
    x = DEEPNORM_ALPHA * h_ref[...] + y_ref[...]
    mu = jnp.mean(x, axis=-1, keepdims=True)
    xc = x - mu
    var = jnp.mean(xc * xc, axis=-1, keepdims=True)
    out = xc * lax.rsqrt(var + LN_EPS) * g_ref[...] + b_ref[...]
    o_ref[...] = out
    ob_ref[...] = out.astype(BF16)


def deepnorm_ln(h, y, g, b, tm=256):
    m, d = h.shape
    row = pl.BlockSpec((tm, d), lambda i: (i, 0))
    vec = pl.BlockSpec((1, d), lambda i: (0, 0))
    return pl.pallas_call(
        _ln_body,
        grid=(m // tm,),
        in_specs=[row, row, vec, vec],
        out_specs=[row, row],
        out_shape=[jax.ShapeDtypeStruct((m, d), F32), jax.ShapeDtypeStruct((m, d), BF16)],
        compiler_params=_cparams(("parallel",)),
        name="deepnorm_ln",
    )(h, y, g.reshape(1, d), b.reshape(1, d))


def _row_copy(src_ref, src_row, dst_ref, dst_row, sem):
    return pltpu.make_async_copy(src_ref.at[pl.ds(src_row, 1), :], dst_ref.at[pl.ds(dst_row, 1), :], sem)


def _start_rows(src_ref, idx_ref, base, n, dst_ref, dst_base, sem):
    def body(r, carry):
        _row_copy(src_ref, idx_ref[base + r], dst_ref, dst_base + r, sem).start()
        return carry

    lax.fori_loop(0, n, body, 0, unroll=8)


def _wait_rows(src_ref, n, dst_ref, dst_base, sem):
    def body(r, carry):
        _row_copy(src_ref, 0, dst_ref, dst_base + r, sem).wait()
        return carry

    lax.fori_loop(0, n, body, 0, unroll=8)


def _gather_body(tok_ref, src_ref, o_ref, buf_ref, sem_ref):
    rows = o_ref.shape[0]
    i = pl.program_id(0)
    slot = i % 2

    @pl.when(i == 0)
    def _():
        _start_rows(src_ref, tok_ref, 0, rows, buf_ref.at[0], 0, sem_ref.at[0])

    @pl.when(i + 1 < pl.num_programs(0))
    def _():
        _start_rows(src_ref, tok_ref, (i + 1) * rows, rows, buf_ref.at[1 - slot], 0, sem_ref.at[1 - slot])

    _wait_rows(src_ref, rows, buf_ref.at[slot], 0, sem_ref.at[slot])
    o_ref[...] = buf_ref[slot].astype(o_ref.dtype)


def gather_rows(src, row_idx, out_dtype, rows=256):
    p = row_idx.shape[0]
    d = src.shape[1]
    assert p % rows == 0
    spec = pltpu.PrefetchScalarGridSpec(
        num_scalar_prefetch=1,
        grid=(p // rows,),
        in_specs=[pl.BlockSpec(memory_space=pl.ANY)],
        out_specs=pl.BlockSpec((rows, d), lambda i, tok: (i, 0)),
        scratch_shapes=[pltpu.VMEM((2, rows, d), src.dtype), pltpu.SemaphoreType.DMA((2,))],
    )
    return pl.pallas_call(
        _gather_body, grid_spec=spec,
        out_shape=jax.ShapeDtypeStruct((p, d), out_dtype),
        compiler_params=_cparams(("arbitrary",)),
        name="gather_rows",
    )(row_idx, src)


def _combine_ln_body(p0_ref, p1_ref, h_ref, y_ref, g_ref, b_ref, o_ref, buf_ref, sem_ref):
    rows = h_ref.shape[0]
    i = pl.program_id(0)
    slot = i % 2

    def start(step, s):
        _start_rows(y_ref, p0_ref, step * rows, rows, buf_ref.at[s], 0, sem_ref.at[s])
        _start_rows(y_ref, p1_ref, step * rows, rows, buf_ref.at[s], rows, sem_ref.at[s])

    @pl.when(i == 0)
    def _():
        start(0, 0)

    @pl.when(i + 1 < pl.num_programs(0))
    def _():
        start(i + 1, 1 - slot)

    _wait_rows(y_ref, 2 * rows, buf_ref.at[slot], 0, sem_ref.at[slot])
    x = DEEPNORM_ALPHA * h_ref[...] + (buf_ref[slot, 0:rows, :] + buf_ref[slot, rows:2 * rows, :])
    mu = jnp.mean(x, axis=-1, keepdims=True)
    xc = x - mu
    var = jnp.mean(xc * xc, axis=-1, keepdims=True)
    o_ref[...] = xc * lax.rsqrt(var + LN_EPS) * g_ref[...] + b_ref[...]


def combine_deepnorm_ln(h, y, pos0, pos1, g, b, rows=128):
    m, d = h.shape
    assert m % rows == 0
    row = pl.BlockSpec((rows, d), lambda i, p0, p1: (i, 0))
    vec = pl.BlockSpec((1, d), lambda i, p0, p1: (0, 0))
    spec = pltpu.PrefetchScalarGridSpec(
        num_scalar_prefetch=2,
        grid=(m // rows,),
        in_specs=[row, pl.BlockSpec(memory_space=pl.ANY), vec, vec],
        out_specs=row,
        scratch_shapes=[pltpu.VMEM((2, 2 * rows, d), y.dtype), pltpu.SemaphoreType.DMA((2,))],
    )
    return pl.pallas_call(
        _combine_ln_body, grid_spec=spec,
        out_shape=jax.ShapeDtypeStruct((m, d), F32),
        compiler_params=_cparams(("arbitrary",)),
        name="combine_deepnorm_ln",
    )(pos0, pos1, h, y, g.reshape(1, d), b.reshape(1, d))


def _conv_silu(raw_ref, stage_ref, w_ref, b_ref, first):
    blk = raw_ref.shape[0]
    kk = w_ref.shape[0]

    @pl.when(first)
    def _():
        stage_ref[0:SUBLANES, :] = jnp.zeros((SUBLANES, stage_ref.shape[1]), F32)

    stage_ref[SUBLANES:SUBLANES + blk, :] = raw_ref[...]
    y = b_ref[...]
    for i in range(kk):
        y = y + w_ref[i:i + 1, :] * stage_ref[pl.ds(SUBLANES - (kk - 1) + i, blk), :]
    stage_ref[0:SUBLANES, :] = stage_ref[blk:blk + SUBLANES, :]
    return _silu(y)


def _ssd_body(z_ref, xr_ref, br_ref, cr_ref, dtr_ref, cwx_ref, cwb_ref, cwc_ref, cbx_ref, cbb_ref, cbc_ref,
              dtb_ref, aneg_ref, dskip_ref, nw_ref, o_ref,
              state_ref, y_ref, sx_ref, sb_ref, sc_ref, *, heads, head_dim, groups, n_state):
    blk = xr_ref.shape[0]
    hpg = heads // groups
    gw = hpg * head_dim
    inner = heads * head_dim
    first = pl.program_id(1) == 0

    @pl.when(first)
    def _():
        state_ref[...] = jnp.zeros_like(state_ref)

    xs = _conv_silu(xr_ref, sx_ref, cwx_ref, cbx_ref, first)
    bmat = _conv_silu(br_ref, sb_ref, cwb_ref, cbb_ref, first).astype(BF16)
    cmat = _conv_silu(cr_ref, sc_ref, cwc_ref, cbc_ref, first).astype(BF16)

    row = lax.broadcasted_iota(jnp.int32, (blk, blk), 0)
    col = lax.broadcasted_iota(jnp.int32, (blk, blk), 1)
    causal = row >= col
    eh = lax.broadcasted_iota(jnp.int32, (heads, inner), 0)
    ec = lax.broadcasted_iota(jnp.int32, (heads, inner), 1)
    expand = (ec // head_dim == eh).astype(F32)

    dtp = dtr_ref[:, :heads] + dtb_ref[...]
    dt = jnp.maximum(dtp, 0.0) + jnp.log(1.0 + jnp.exp(-jnp.abs(dtp)))
    a = dt * aneg_ref[...]
    cs = jnp.dot(causal.astype(F32), a, preferred_element_type=F32, precision=HIGHEST)
    cs_t = cs.T
    cs_x = jnp.dot(cs, expand, preferred_element_type=F32, precision=HIGHEST)
    dt_x = jnp.dot(dt, expand, preferred_element_type=F32, precision=HIGHEST)
    xdt = xs * dt_x
    last_x = cs_x[blk - 1:blk, :]
    x_in = (xdt * jnp.exp(last_x - cs_x)).astype(BF16)
    x_b = xdt.astype(BF16)
    grow = jnp.exp(cs_x)

    for g in range(groups):
        bg = bmat[:, g * n_state:(g + 1) * n_state]
        cg = cmat[:, g * n_state:(g + 1) * n_state]
        cb = lax.dot_general(cg, bg, (((1,), (1,)), ((), ())), preferred_element_type=F32)
        st = state_ref[g]
        y_off = jnp.dot(cg, st.astype(BF16), preferred_element_type=F32)
        y_ref[:, g * gw:(g + 1) * gw] = y_off * grow[:, g * gw:(g + 1) * gw]
        for r in range(hpg):
            h = g * hpg + r
            seg = cs[:, h:h + 1] - cs_t[h:h + 1, :]
            dec = jnp.exp(jnp.where(causal, seg, -jnp.inf))
            mh = (cb * dec).astype(BF16)
            yh = jnp.dot(mh, x_b[:, h * head_dim:(h + 1) * head_dim], preferred_element_type=F32)
            y_ref[:, h * head_dim:(h + 1) * head_dim] += yh
        new = lax.dot_general(bg, x_in[:, g * gw:(g + 1) * gw], (((0,), (0,)), ((), ())),
                              preferred_element_type=F32)
        state_ref[g] = st * jnp.exp(last_x[:, g * gw:(g + 1) * gw]) + new

    y = y_ref[...] + dskip_ref[...] * xs
    y = y * _silu(z_ref[...])
    for g in range(groups):
        yg = y[:, g * gw:(g + 1) * gw]
        ms = jnp.mean(yg * yg, axis=-1, keepdims=True)
        o_ref[:, g * gw:(g + 1) * gw] = (yg * lax.rsqrt(ms + RMS_EPS)
                                         * nw_ref[:, g * gw:(g + 1) * gw]).astype(o_ref.dtype)


def ssd_mixer(zx, dt_raw, conv_w, conv_b, dt_bias, a_neg, d_skip_x, norm_w, *, batch, heads=SSD_HEADS,
              head_dim=SSD_HEAD_DIM, groups=SSD_GROUPS, n_state=SSD_STATE, blk=SSD_BLOCK):
    m = zx.shape[0]
    inner = heads * head_dim
    bc = groups * n_state
    seq = m // batch
    nblk = seq // blk
    gw = inner // groups
    kk = conv_w.shape[0]
    rows = lambda w, cb: pl.BlockSpec((blk, w), lambda b, c: (b * nblk + c, cb))
    full = lambda r, w: pl.BlockSpec((r, w), lambda b, c: (0, 0))
    body = functools.partial(_ssd_body, heads=heads, head_dim=head_dim, groups=groups, n_state=n_state)
    cw = (conv_w[:, :inner], conv_w[:, inner:inner + bc], conv_w[:, inner + bc:])
    cb = tuple(t.reshape(1, -1) for t in (conv_b[:inner], conv_b[inner:inner + bc], conv_b[inner + bc:]))
    return pl.pallas_call(
        body,
        grid=(batch, nblk),
        in_specs=[rows(inner, 0), rows(inner, 1), rows(bc, 2 * inner // bc), rows(bc, 2 * inner // bc + 1),
                  rows(dt_raw.shape[1], 0),
                  full(kk, inner), full(kk, bc), full(kk, bc), full(1, inner), full(1, bc), full(1, bc),
                  full(1, heads), full(1, heads), full(1, inner), full(1, inner)],
        out_specs=rows(inner, 0),
        out_shape=jax.ShapeDtypeStruct((m, inner), BF16),
        scratch_shapes=[pltpu.VMEM((groups, n_state, gw), F32), pltpu.VMEM((blk, inner), F32),
                        pltpu.VMEM((blk + SUBLANES, inner), F32), pltpu.VMEM((blk + SUBLANES, bc), F32),
                        pltpu.VMEM((blk + SUBLANES, bc), F32)],
        compiler_params=_cparams(("parallel", "arbitrary")),
        name="ssd_mixer",
    )(zx, zx, zx, zx, dt_raw, *cw, *cb, dt_bias.reshape(1, heads), a_neg.reshape(1, heads),
      d_skip_x.reshape(1, inner), norm_w.reshape(1, inner))


def _attn_body(lam_ref, q_ref, k_ref, v_ref, nw_ref, o_ref, acc_ref, *, post_scale, chunk, hd):
    t = q_ref.shape[0]
    hps = q_ref.shape[1] // hd
    half = hd // 2
    i = pl.program_id(2)
    sub = lax.broadcasted_iota(jnp.int32, (hd, t), 0)
    qs_t = []
    for hh in range(hps):
        qt = q_ref[:, hh * hd:(hh + 1) * hd].astype(F32).T
        qs_t.append(jnp.concatenate([jnp.where(sub < half, qt, 0.0), jnp.where(sub >= half, qt, 0.0)],
                                    axis=1).astype(BF16))
    acc_ref[...] = jnp.zeros_like(acc_ref)

    def scores(j, hh):
        start = pl.multiple_of(j * t, t)
        kb = k_ref[pl.ds(start, t), hh * hd:(hh + 1) * hd]
        return jnp.dot(kb, qs_t[hh], preferred_element_type=F32)

    def step(j, carry, last):
        start = pl.multiple_of(j * t, t)
        out = []
        for hh in range(hps):
            m, l, s = carry[hh]
            s_next = s if last else scores(j + 1, hh)
            vb = v_ref[pl.ds(start, t), hh * hd:(hh + 1) * hd]
            if last:
                kk = lax.broadcasted_iota(jnp.int32, (t, 2 * t), 0)
                qq = lax.broadcasted_iota(jnp.int32, (t, 2 * t), 1)
                qq = jnp.where(qq >= t, qq - t, qq)
                s = jnp.where(kk // chunk <= qq // chunk, s, -jnp.inf)
            m_new = jnp.maximum(m, jnp.max(s, axis=0, keepdims=True))
            alpha = jnp.exp2(m - m_new)
            p = jnp.exp2(s - m_new)
            l = alpha * l + jnp.sum(p, axis=0, keepdims=True)
            pv = lax.dot_general(vb, p.astype(BF16), (((0,), (0,)), ((), ())),
                                 preferred_element_type=F32)
            acc_ref[hh] = alpha * acc_ref[hh] + pv
            out.append((m_new, l, s_next))
        return tuple(out)

    init = tuple((jnp.full((1, 2 * t), -jnp.inf, F32), jnp.zeros((1, 2 * t), F32), scores(0, hh))
                 for hh in range(hps))
    carry = lax.fori_loop(0, i, lambda j, c: step(j, c, False), init)
    carry = step(i, carry, True)
    for hh in range(hps):
        o = acc_ref[hh] * (1.0 / carry[hh][1])
        out = o[:, :t] - lam_ref[0] * o[:, t:]
        ms = jnp.mean(out * out, axis=0, keepdims=True)
        out = out * lax.rsqrt(ms + RMS_EPS) * (nw_ref[...] * post_scale)
        o_ref[:, hh * hd:(hh + 1) * hd] = out.T.astype(o_ref.dtype)


def diff_attention(q, k, v, lam, norm_w, post_scale, *, batch, heads=DIFF_HEADS, blk=ATTN_BLOCK,
                   chunk=CHUNK, hps=ATTN_HEADS_PER_STEP):
    m, width = q.shape
    hd = width // heads
    seq = m // batch
    nq = seq // blk
    q3, k3, v3 = (t.reshape(batch, seq, width) for t in (q, k, v))
    body = functools.partial(_attn_body, post_scale=post_scale, chunk=chunk, hd=hd)
    out = pl.pallas_call(
        body,
        grid=(batch, heads // hps, nq),
        in_specs=[pl.BlockSpec(memory_space=pltpu.SMEM),
                  pl.BlockSpec((None, blk, hps * hd), lambda b, h, i: (b, i, h)),
                  pl.BlockSpec((None, seq, hps * hd), lambda b, h, i: (b, 0, h)),
                  pl.BlockSpec((None, seq, hps * hd), lambda b, h, i: (b, 0, h)),
                  pl.BlockSpec((hd, 1), lambda b, h, i: (0, 0))],
        out_specs=pl.BlockSpec((None, blk, hps * hd), lambda b, h, i: (b, i, h)),
        out_shape=jax.ShapeDtypeStruct((batch, seq, width), BF16),
        scratch_shapes=[pltpu.VMEM((hps, hd, 2 * blk), F32)],
        compiler_params=_cparams(("parallel", "parallel", "arbitrary")),
        name="diff_attention",
    )(lam.reshape(1).astype(F32), q3, k3, v3, norm_w.reshape(hd, 1))
    return out.reshape(m, width)


def _ret_body(q_ref, k_ref, v_ref, g_ref, dmat_ref, qd_ref, kd_ref, bd_ref, o_ref, state_ref):
    hps = dmat_ref.shape[0]
    dk = q_ref.shape[1] // hps
    dv = v_ref.shape[1] // hps

    @pl.when(pl.program_id(2) == 0)
    def _():
        state_ref[...] = jnp.zeros_like(state_ref)

    for hh in range(hps):
        qb = q_ref[:, hh * dk:(hh + 1) * dk].astype(BF16)
        k = k_ref[:, hh * dk:(hh + 1) * dk]
        vb = v_ref[:, hh * dv:(hh + 1) * dv].astype(BF16)
        s = lax.dot_general(qb, k.astype(BF16), (((1,), (1,)), ((), ())), preferred_element_type=F32)
        y = jnp.dot((s * dmat_ref[hh]).astype(BF16), vb, preferred_element_type=F32)
        st = state_ref[hh]
        y = y + jnp.dot(qb, st.astype(BF16), preferred_element_type=F32) * qd_ref[hh]
        kdec = (k.astype(F32) * kd_ref[hh]).astype(BF16)
        state_ref[hh] = st * bd_ref[hh] + lax.dot_general(kdec, vb, (((0,), (0,)), ((), ())),
                                                          preferred_element_type=F32)
        mu = jnp.mean(y, axis=-1, keepdims=True)
        yc = y - mu
        var = jnp.mean(yc * yc, axis=-1, keepdims=True)
        gate = _silu(g_ref[:, hh * dv:(hh + 1) * dv].astype(F32))
        o_ref[:, hh * dv:(hh + 1) * dv] = (gate * (yc * lax.rsqrt(var + LN_EPS))).astype(o_ref.dtype)


def _retention_tables(heads, blk, chunk):
    log_g = jnp.log1p(-(2.0 ** (-5.0 - jnp.arange(heads, dtype=F32))))
    idx = jnp.arange(blk, dtype=F32)
    n, mm = idx[:, None], idx[None, :]
    cn, cm = jnp.floor(n / chunk), jnp.floor(mm / chunk)
    dist = jnp.where(cn == cm, jnp.abs(n - mm), n - mm)
    dmat = jnp.where((cm <= cn)[None], jnp.exp(dist[None] * log_g[:, None, None]), 0.0)
    qd = jnp.exp((idx + 1.0)[None, :, None] * log_g[:, None, None])
    kd = jnp.exp((blk - 1.0 - idx)[None, :, None] * log_g[:, None, None])
    bd = jnp.exp(blk * log_g)[:, None, None]
    return dmat, qd, kd, bd


def retention_mixer(q, k, vg, *, batch, heads=RET_HEADS, blk=RET_BLOCK, chunk=CHUNK, hps=RET_HEADS_PER_STEP):
    m = q.shape[0]
    dk = q.shape[1] // heads
    dv = vg.shape[1] // (2 * heads)
    seq = m // batch
    nblk = seq // blk
    groups = heads // hps
    dmat, qd, kd, bd = _retention_tables(heads, blk, chunk)
    rows = lambda w, off: pl.BlockSpec((blk, hps * w), lambda b, h, c: (b * nblk + c, h + off))
    table = lambda r, w: pl.BlockSpec((hps, r, w), lambda b, h, c: (h, 0, 0))
    return pl.pallas_call(
        _ret_body,
        grid=(batch, groups, nblk),
        in_specs=[rows(dk, 0), rows(dk, 0), rows(dv, 0), rows(dv, groups),
                  table(blk, blk), table(blk, 1), table(blk, 1), table(1, 1)],
        out_specs=rows(dv, 0),
        out_shape=jax.ShapeDtypeStruct((m, heads * dv), BF16),
        scratch_shapes=[pltpu.VMEM((hps, dk, dv), F32)],
        compiler_params=_cparams(("parallel", "parallel", "arbitrary")),
        name="retention_mixer",
    )(q, k, vg, vg, dmat, qd, kd, bd)


def _router_body(h_ref, w_ref, idx_ref, wt_ref, *, n_experts):
    logits = jnp.dot(h_ref[...], w_ref[...], preferred_element_type=F32, precision=HIGHEST)
    lane = lax.broadcasted_iota(jnp.int32, logits.shape, 1)
    logits = jnp.where(lane < n_experts, logits, -jnp.inf)
    v1 = jnp.max(logits, axis=-1, keepdims=True)
    i1 = jnp.min(jnp.where(logits == v1, lane, LANES), axis=-1, keepdims=True)
    rest = jnp.where(lane == i1, -jnp.inf, logits)
    v2 = jnp.max(rest, axis=-1, keepdims=True)
    i2 = jnp.min(jnp.where(rest == v2, lane, LANES), axis=-1, keepdims=True)
    e2 = jnp.exp(v2 - v1)
    w1 = 1.0 / (1.0 + e2)
    w2 = e2 / (1.0 + e2)
    idx_ref[...] = jnp.where(lane == 0, i1, jnp.where(lane == 1, i2, 0))
    wt_ref[...] = jnp.where(lane == 0, w1, jnp.where(lane == 1, w2, 0.0))


def moe_router(h, w_router, tm=512):
    m, d = h.shape
    n_experts = w_router.shape[1]
    w_pad = jnp.zeros((d, LANES), F32).at[:, :n_experts].set(w_router)
    row = pl.BlockSpec((tm, LANES), lambda i: (i, 0))
    idx, wt = pl.pallas_call(
        functools.partial(_router_body, n_experts=n_experts),
        grid=(m // tm,),
        in_specs=[pl.BlockSpec((tm, d), lambda i: (i, 0)), pl.BlockSpec((d, LANES), lambda i: (0, 0))],
        out_specs=[row, row],
        out_shape=[jax.ShapeDtypeStruct((m, LANES), jnp.int32), jax.ShapeDtypeStruct((m, LANES), F32)],
        compiler_params=_cparams(("parallel",)),
        name="moe_router",
    )(h, w_pad)
    return idx[:, :TOP_K], wt[:, :TOP_K]


def _group_start(te_ref, t):
    return jnp.logical_or(t == 0, te_ref[t] != te_ref[jnp.maximum(t - 1, 0)])


def _moe_up_body(te_ref, nv_ref, x_ref, wg_ref, wu_ref, o_ref, wb_ref):
    t = pl.program_id(1)

    @pl.when(t < nv_ref[0])
    def _():
        _gate_up(x_ref, wg_ref, wu_ref, wb_ref, o_ref, _group_start(te_ref, t))

    @pl.when(t >= nv_ref[0])
    def _():
        o_ref[...] = jnp.zeros_like(o_ref)


def _moe_down_body(te_ref, nv_ref, a_ref, w_ref, rw_ref, o_ref, wb_ref):
    t = pl.program_id(1)

    @pl.when(t < nv_ref[0])
    def _():
        w = _weight_bf16(w_ref, wb_ref, _group_start(te_ref, t))
        for rows in _row_splits(o_ref.shape[0]):
            y = jnp.dot(a_ref[rows, :], w[...], preferred_element_type=F32)
            o_ref[rows, :] = (y * rw_ref[rows, :]).astype(o_ref.dtype)

    @pl.when(t >= nv_ref[0])
    def _():
        o_ref[...] = jnp.zeros_like(o_ref)


def moe_experts(x_sorted, tile_expert, n_valid, row_w, w_gate, w_up, w_down, tm=MOE_TM, tn=512):
    p, d = x_sorted.shape
    f = w_gate.shape[2]
    nt = p // tm
    up_spec = pltpu.PrefetchScalarGridSpec(
        num_scalar_prefetch=2,
        grid=(f // tn, nt),
        in_specs=[pl.BlockSpec((tm, d), lambda j, t, te, nv: (t, 0)),
                  pl.BlockSpec((None, d, tn), lambda j, t, te, nv: (te[t], 0, j)),
                  pl.BlockSpec((None, d, tn), lambda j, t, te, nv: (te[t], 0, j))],
        out_specs=pl.BlockSpec((tm, tn), lambda j, t, te, nv: (t, j)),
        scratch_shapes=[pltpu.VMEM((d, 2 * tn), BF16)],
    )
    act = pl.pallas_call(
        _moe_up_body, grid_spec=up_spec,
        out_shape=jax.ShapeDtypeStruct((p, f), BF16),
        compiler_params=_cparams(("parallel", "arbitrary")),
        name="moe_up",
    )(tile_expert, n_valid, x_sorted, w_gate, w_up)
    down_spec = pltpu.PrefetchScalarGridSpec(
        num_scalar_prefetch=2,
        grid=(d // tn, nt),
        in_specs=[pl.BlockSpec((tm, f), lambda j, t, te, nv: (t, 0)),
                  pl.BlockSpec((None, f, tn), lambda j, t, te, nv: (te[t], 0, j)),
                  pl.BlockSpec((tm, 1), lambda j, t, te, nv: (t, 0))],
        out_specs=pl.BlockSpec((tm, tn), lambda j, t, te, nv: (t, j)),
        scratch_shapes=[pltpu.VMEM((f, tn), BF16)],
    )
    return pl.pallas_call(
        _moe_down_body, grid_spec=down_spec,
        out_shape=jax.ShapeDtypeStruct((p, d), F32),
        compiler_params=_cparams(("parallel", "arbitrary")),
        name="moe_down",
    )(tile_expert, n_valid, act, w_down, row_w)


def moe_routing(top_i, top_w, n_experts, tm):
    m = top_i.shape[0]
    pairs = m * TOP_K
    p = pairs + n_experts * tm
    flat_e = top_i.reshape(pairs)
    onehot = (flat_e[:, None] == jnp.arange(n_experts, dtype=jnp.int32)[None, :]).astype(jnp.int32)
    rank = jnp.sum((jnp.cumsum(onehot, axis=0) - onehot) * onehot, axis=1)
    counts = jnp.sum(onehot, axis=0)
    padded = ((counts + tm - 1) // tm) * tm
    ends = jnp.cumsum(padded)
    starts = ends - padded
    dest = starts[flat_e] + rank
    row_token = jnp.zeros((p,), jnp.int32).at[dest].set(jnp.arange(pairs, dtype=jnp.int32) // TOP_K)
    row_w = jnp.zeros((p,), F32).at[dest].set(top_w.reshape(pairs)).reshape(p, 1)
    tile_start = jnp.arange(p // tm, dtype=jnp.int32) * tm
    tile_expert = jnp.minimum(jnp.sum((tile_start[:, None] >= ends[None, :]).astype(jnp.int32), axis=1),
                              n_experts - 1).astype(jnp.int32)
    n_valid = (ends[-1] // tm).astype(jnp.int32).reshape(1)
    return row_token, row_w, dest.reshape(m, TOP_K), tile_expert, n_valid


def _rotary_tables(inv_freq, seq, blocks, scale):
    ang = jnp.arange(seq, dtype=F32)[:, None] * inv_freq[None, :]
    cos, sin, zero = jnp.cos(ang) * scale, jnp.sin(ang) * scale, jnp.zeros_like(ang)
    rep = lambda a, b: jnp.tile(jnp.concatenate([a, b], axis=1), (1, blocks))
    return rep(cos, cos), rep(-sin, zero), rep(zero, sin), inv_freq.shape[0]


def _layer_even(hb, batch, seq, w_in, conv_w, conv_b, dt_bias, a_log, d_skip, ssd_norm_w,
                lam_q1, lam_k1, lam_q2, lam_k2, diff_norm_w, w_out, lambda_init):
    zx_cols = SSD_INNER + SSD_XBC
    qkv_lo = zx_cols + SSD_HEADS
    zx = project([hb], w_in, F32, 1024, 512, n=zx_cols)
    dt_raw = project([hb], w_in, F32, 1024, LANES, col_lo=zx_cols, n=LANES)
    inv_freq = ROPE_THETA ** (-jnp.arange(0, DIFF_QK_DIM, 2, dtype=F32) / DIFF_QK_DIM)
    rot_q = _rotary_tables(inv_freq, seq, 2, DIFF_QK_DIM ** -0.5 * math.log2(math.e))
    rot_k = _rotary_tables(inv_freq, seq, 2, 1.0)
    q = project([hb], w_in, BF16, 1024, 512, col_lo=qkv_lo, n=DIFF_WIDTH, rot=rot_q, seq=seq)
    k = project([hb], w_in, BF16, 1024, 512, col_lo=qkv_lo + DIFF_WIDTH, n=DIFF_WIDTH, rot=rot_k, seq=seq)
    v = project([hb], w_in, BF16, 1024, 512, col_lo=qkv_lo + 2 * DIFF_WIDTH, n=DIFF_WIDTH)
    y = ssd_mixer(zx, dt_raw, conv_w, conv_b, dt_bias, -jnp.exp(a_log), jnp.repeat(d_skip, SSD_HEAD_DIM),
                  ssd_norm_w, batch=batch)
    lam = (jnp.exp(jnp.sum(lam_q1 * lam_k1)) - jnp.exp(jnp.sum(lam_q2 * lam_k2)) + lambda_init)
    o = diff_attention(q, k, v, lam, diff_norm_w, 1.0 - lambda_init, batch=batch)
    return project([y, o], w_out.astype(BF16), F32, 512, 512)


def _layer_odd_mixer(hb, batch, seq, w_in, w_out):
    inv_freq = 1.0 / (ROPE_THETA ** jnp.linspace(0.0, 1.0, RET_QK_DIM // 2, dtype=F32))
    rot_q = _rotary_tables(inv_freq, seq, 1, 1.0)
    rot_k = _rotary_tables(inv_freq, seq, 1, RET_QK_DIM ** -0.5)
    q = project([hb], w_in, BF16, 1024, 512, n=RET_QK_WIDTH, rot=rot_q, seq=seq)
    k = project([hb], w_in, BF16, 1024, 512, col_lo=RET_QK_WIDTH, n=RET_QK_WIDTH, rot=rot_k, seq=seq)
    vg = project([hb], w_in, BF16, 1024, 512, col_lo=2 * RET_QK_WIDTH, n=2 * RET_V_WIDTH)
    o = retention_mixer(q, k, vg, batch=batch)
    return project([o], w_out.astype(BF16), F32, 512, 512)


def _moe_ln(h, w_router, w_gate, w_up, w_down, ln_g, ln_b):
    top_i, top_w = moe_router(h, w_router)
    row_token, row_w, pos, tile_expert, n_valid = moe_routing(top_i, top_w, N_EXPERTS, MOE_TM)
    x_sorted = gather_rows(h, row_token, BF16)
    y = moe_experts(x_sorted, tile_expert, n_valid, row_w, w_gate, w_up, w_down)
    return combine_deepnorm_ln(h, y, pos[:, 0], pos[:, 1], ln_g, ln_b)


def kernel(x, ab_w_in, ab_conv_w, ab_conv_b, ab_dt_bias, ab_a_log, ab_d_skip, ab_ssd_norm_w, ab_lam_q1, ab_lam_k1, ab_lam_q2, ab_lam_k2, ab_diff_norm_w, ab_w_out, ffn_w_gate, ffn_w_up, ffn_w_down, c_w_in, c_w_out, moe_w_router, moe_w_gate, moe_w_up, moe_w_down, ln_mix_g, ln_mix_b, ln_ffn_g, ln_ffn_b):
    batch, seq, d = x.shape
    m = batch * seq
    h = x.reshape(m, d)
    hb = h.astype(BF16)
    lambda_init = 0.8 - 0.6 * math.exp(-0.3 * 0)
    mix = _layer_even(hb, batch, seq, ab_w_in[0], ab_conv_w[0], ab_conv_b[0], ab_dt_bias[0], ab_a_log[0],
                      ab_d_skip[0], ab_ssd_norm_w[0], ab_lam_q1[0], ab_lam_k1[0], ab_lam_q2[0], ab_lam_k2[0],
                      ab_diff_norm_w[0], ab_w_out[0], lambda_init)
    h, hb = deepnorm_ln(h, mix, ln_mix_g[0], ln_mix_b[0])
    act = swiglu_up(hb, ffn_w_gate[0], ffn_w_up[0], 1024, 256)
    ff = matmul(act, ffn_w_down[0].astype(BF16), F32, 512, 256)
    h, hb = deepnorm_ln(h, ff, ln_ffn_g[0], ln_ffn_b[0])
    mix = _layer_odd_mixer(hb, batch, seq, c_w_in[0], c_w_out[0])
    h, _ = deepnorm_ln(h, mix, ln_mix_g[1], ln_mix_b[1])
    h = _moe_ln(h, moe_w_router[0], moe_w_gate[0], moe_w_up[0], moe_w_down[0], ln_ffn_g[1], ln_ffn_b[1])
    return h.reshape(batch, seq, d)
```

```python
import functools
import math

import jax
import jax.numpy as jnp
from jax import lax
from jax.experimental import pallas as pl
from jax.experimental.pallas import tpu as pltpu

F32 = jnp.float32
BF16 = jnp.bfloat16
HIGHEST = lax.Precision.HIGHEST

D_MODEL = 4096
DEPTH = 2
CHUNK = 64
SSD_HEADS = 64
SSD_HEAD_DIM = 64
SSD_INNER = SSD_HEADS * SSD_HEAD_DIM
SSD_GROUPS = 8
SSD_STATE = 128
SSD_CONV = 4
SSD_BC = SSD_GROUPS * SSD_STATE
SSD_XBC = SSD_INNER + 2 * SSD_BC
DIFF_HEADS = 32
DIFF_QK_DIM = 64
DIFF_V_DIM = 2 * DIFF_QK_DIM
DIFF_WIDTH = DIFF_HEADS * DIFF_V_DIM
ROPE_THETA = 10000.0
RET_HEADS = 16
RET_QK_DIM = D_MODEL // RET_HEADS
RET_V_DIM = 2 * RET_QK_DIM
RET_QK_WIDTH = RET_HEADS * RET_QK_DIM
RET_V_WIDTH = RET_HEADS * RET_V_DIM
FFN_DIM = 11008
N_EXPERTS = 8
TOP_K = 2
EXPERT_DIM = 4096
DEEPNORM_ALPHA = (2 * DEPTH) ** 0.25
LN_EPS = 1e-5
RMS_EPS = 1e-5

LANES = 128
SUBLANES = 8
VMEM_LIMIT_BYTES = 56 * 1024 * 1024

SSD_BLOCK = 128
RET_BLOCK = 256
RET_HEADS_PER_STEP = 2
ATTN_BLOCK = 256
ATTN_HEADS_PER_STEP = 4
MOE_TM = 512


def _cparams(sem):
    return pltpu.CompilerParams(dimension_semantics=sem, vmem_limit_bytes=VMEM_LIMIT_BYTES)


def _silu(x):
    return x * (1.0 / (1.0 + jnp.exp(-x)))


SPLIT_ROWS = 512


def _row_splits(tm):
    return (slice(0, tm // 2), slice(tm // 2, tm)) if tm >= SPLIT_ROWS else (slice(0, tm),)


def _weight_bf16(w_ref, wb_ref, first):
    if wb_ref is None:
        return w_ref

    @pl.when(first)
    def _():
        wb_ref[...] = w_ref[...].astype(BF16)

    return wb_ref


def _rotate(acc, cos_ref, sa_ref, sb_ref, half, rows):
    tn = acc.shape[1]
    reps = tn // cos_ref.shape[1]
    tile = lambda r: jnp.tile(r[rows, :], (1, reps))
    return (acc * tile(cos_ref) + pltpu.roll(acc, tn - half, 1) * tile(sa_ref)
            + pltpu.roll(acc, half, 1) * tile(sb_ref))


REALIGN_ROWS = 512


def _weight_bf16_shifted(w_ref, wn_ref, wb_ref, first, shift):
    k, tn = w_ref.shape
    width = tn + wn_ref.shape[1]

    @pl.when(first)
    def _():
        for r0 in range(0, k, REALIGN_ROWS):
            rows = slice(r0, min(r0 + REALIGN_ROWS, k))
            window = jnp.concatenate([w_ref[rows, :], wn_ref[rows, :]], axis=1)
            wb_ref[rows, :] = pltpu.roll(window, width - shift, 1)[:, :tn].astype(BF16)

    return wb_ref


def _proj_body(*refs, n_a, rot_half, cast, shift):
    a_refs = refs[:n_a]
    w_ref = refs[n_a]
    pos = n_a + 1
    if shift:
        wn_ref = refs[pos]
        pos += 1
    if rot_half:
        cos_ref, sa_ref, sb_ref = refs[pos:pos + 3]
        pos += 3
    o_ref = refs[pos]
    wb_ref = refs[pos + 1] if cast else None
    first = pl.program_id(1) == 0
    if shift:
        w = _weight_bf16_shifted(w_ref, wn_ref, wb_ref, first, shift)
    else:
        w = _weight_bf16(w_ref, wb_ref, first)
    for rows in _row_splits(o_ref.shape[0]):
        k0 = 0
        acc = None
        for a_ref in a_refs:
            ka = a_ref.shape[1]
            part = jnp.dot(a_ref[rows, :], w[k0:k0 + ka, :], preferred_element_type=F32)
            acc = part if acc is None else acc + part
            k0 += ka
        if rot_half:
            acc = _rotate(acc, cos_ref, sa_ref, sb_ref, rot_half, rows)
        o_ref[rows, :] = acc.astype(o_ref.dtype)


def project(a_list, w, out_dtype, tm, tn, *, col_lo=0, n=None, rot=None, seq=None):
    m = a_list[0].shape[0]
    k, width = w.shape[-2:]
    n = width - col_lo if n is None else n
    shift = col_lo % LANES
    base = col_lo - shift
    cast = w.dtype != BF16
    assert m % tm == 0 and n % tn == 0 and base % tn == 0 and sum(a.shape[1] for a in a_list) == k
    assert cast or not shift
    assert w.ndim == 2 or w.shape[0] == 1
    off = base // tn
    lead = (None,) * (w.ndim - 2)
    wspec = lambda cols, col_of: pl.BlockSpec(lead + (k, cols), lambda j, i: (0,) * (w.ndim - 1) + (col_of(j),))
    in_specs = [pl.BlockSpec((tm, a.shape[1]), lambda j, i: (i, 0)) for a in a_list]
    in_specs.append(wspec(tn, lambda j: j + off))
    args = list(a_list) + [w]
    if shift:
        lanes_per_tile = tn // LANES
        in_specs.append(wspec(LANES, lambda j: (j + off + 1) * lanes_per_tile))
        args.append(w)
    rot_half = 0
    if rot is not None:
        cos, sa, sb, rot_half = rot
        nrow = seq // tm
        tab = pl.BlockSpec((tm, cos.shape[1]), lambda j, i: (i % nrow, 0))
        in_specs += [tab, tab, tab]
        args += [cos, sa, sb]
    return pl.pallas_call(
        functools.partial(_proj_body, n_a=len(a_list), rot_half=rot_half, cast=cast, shift=shift),
        grid=(n // tn, m // tm),
        in_specs=in_specs,
        out_specs=pl.BlockSpec((tm, tn), lambda j, i: (i, j)),
        out_shape=jax.ShapeDtypeStruct((m, n), out_dtype),
        scratch_shapes=[pltpu.VMEM((k, tn), BF16)] if cast else [],
        compiler_params=_cparams(("parallel", "arbitrary")),
        name="project",
    )(*args)


def _gate_up(a_ref, wg_ref, wu_ref, wb_ref, o_ref, first):
    tn = wg_ref.shape[1]

    @pl.when(first)
    def _():
        wb_ref[:, :tn] = wg_ref[...].astype(BF16)
        wb_ref[:, tn:] = wu_ref[...].astype(BF16)

    for rows in _row_splits(o_ref.shape[0]):
        gu = jnp.dot(a_ref[rows, :], wb_ref[...], preferred_element_type=F32)
        o_ref[rows, :] = (_silu(gu[:, :tn]) * gu[:, tn:]).astype(o_ref.dtype)


def _swiglu_body(a_ref, wg_ref, wu_ref, o_ref, wb_ref):
    _gate_up(a_ref, wg_ref, wu_ref, wb_ref, o_ref, pl.program_id(1) == 0)


def swiglu_up(a, wg, wu, tm, tn):
    m, k = a.shape
    n = wg.shape[1]
    assert m % tm == 0 and n % tn == 0
    wspec = pl.BlockSpec((k, tn), lambda j, i: (0, j))
    return pl.pallas_call(
        _swiglu_body,
        grid=(n // tn, m // tm),
        in_specs=[pl.BlockSpec((tm, k), lambda j, i: (i, 0)), wspec, wspec],
        out_specs=pl.BlockSpec((tm, tn), lambda j, i: (i, j)),
        out_shape=jax.ShapeDtypeStruct((m, n), BF16),
        scratch_shapes=[pltpu.VMEM((k, 2 * tn), BF16)],
        compiler_params=_cparams(("parallel", "arbitrary")),
        name="swiglu_up",
    )(a, wg, wu)


def _mm_body(a_ref, w_ref, o_ref):
    for rows in _row_splits(o_ref.shape[0]):
        o_ref[rows, :] = jnp.dot(a_ref[rows, :], w_ref[...], preferred_element_type=F32).astype(o_ref.dtype)


def matmul(a, w, out_dtype, tm, tn):
    m, k = a.shape
    n = w.shape[1]
    assert m % tm == 0 and n % tn == 0, (m, n, tm, tn)
    return pl.pallas_call(
        _mm_body,
        grid=(m // tm, n // tn),
        in_specs=[pl.BlockSpec((tm, k), lambda i, j: (i, 0)),
                  pl.BlockSpec((k, tn), lambda i, j: (0, j))],
        out_specs=pl.BlockSpec((tm, tn), lambda i, j: (i, j)),
        out_shape=jax.ShapeDtypeStruct((m, n), out_dtype),
        compiler_params=_cparams(("parallel", "arbitrary")),
        name="matmul",
    )(a, w)


def _ln_body(h_ref, y_ref, g_ref, b_ref, o_ref, ob_ref):
    x = DEEPNORM_ALPHA * h_ref[...] + y_ref[...]
    mu = jnp.mean(x, axis=-1, keepdims=True)
    xc = x - mu
    var = jnp.mean(xc * xc, axis=-1, keepdims=True)
    out = xc * lax.rsqrt(var + LN_EPS) * g_ref[...] + b_ref[...]
    o_ref[...] = out
    ob_ref[...] = out.astype(BF16)


def deepnorm_ln(h, y, g, b, tm=256):
    m, d = h.shape
    row = pl.BlockSpec((tm, d), lambda i: (i, 0))
    vec = pl.BlockSpec((1, d), lambda i: (0, 0))
    return pl.pallas_call(
        _ln_body,
        grid=(m // tm,),
        in_specs=[row, row, vec, vec],
        out_specs=[row, row],
        out_shape=[jax.ShapeDtypeStruct((m, d), F32), jax.ShapeDtypeStruct((m, d), BF16)],
        compiler_params=_cparams(("parallel",)),
        name="deepnorm_ln",
    )(h, y, g.reshape(1, d), b.reshape(1, d))


def _row_copy(src_ref, src_row, dst_ref, dst_row, sem):
    return pltpu.make_async_copy(src_ref.at[pl.ds(src_row, 1), :], dst_ref.at[pl.ds(dst_row, 1), :], sem)


def _start_rows(src_ref, idx_ref, base, n, dst_ref, dst_base, sem):
    def body(r, carry):
        _row_copy(src_ref, idx_ref[base + r], dst_ref, dst_base + r, sem).start()
        return carry

    lax.fori_loop(0, n, body, 0, unroll=8)


def _wait_rows(src_ref, n, dst_ref, dst_base, sem):
    def body(r, carry):
        _row_copy(src_ref, 0, dst_ref, dst_base + r, sem).wait()
        return carry

    lax.fori_loop(0, n, body, 0, unroll=8)


def _gather_body(tok_ref, src_ref, o_ref, buf_ref, sem_ref):
    rows = o_ref.shape[0]
    i = pl.program_id(0)
    slot = i % 2

    @pl.when(i == 0)
    def _():
        _start_rows(src_ref, tok_ref, 0, rows, buf_ref.at[0], 0, sem_ref.at[0])

    @pl.when(i + 1 < pl.num_programs(0))
    def _():
        _start_rows(src_ref, tok_ref, (i + 1) * rows, rows, buf_ref.at[1 - slot], 0, sem_ref.at[1 - slot])

    _wait_rows(src_ref, rows, buf_ref.at[slot], 0, sem_ref.at[slot])
    o_ref[...] = buf_ref[slot].astype(o_ref.dtype)


def gather_rows(src, row_idx, out_dtype, rows=256):
    p = row_idx.shape[0]
    d = src.shape[1]
    assert p % rows == 0
    spec = pltpu.PrefetchScalarGridSpec(
        num_scalar_prefetch=1,
        grid=(p // rows,),
        in_specs=[pl.BlockSpec(memory_space=pl.ANY)],
        out_specs=pl.BlockSpec((rows, d), lambda i, tok: (i, 0)),
        scratch_shapes=[pltpu.VMEM((2, rows, d), src.dtype), pltpu.SemaphoreType.DMA((2,))],
    )
    return pl.pallas_call(
        _gather_body, grid_spec=spec,
        out_shape=jax.ShapeDtypeStruct((p, d), out_dtype),
        compiler_params=_cparams(("arbitrary",)),
        name="gather_rows",
    )(row_idx, src)


def _combine_ln_body(p0_ref, p1_ref, h_ref, y_ref, g_ref, b_ref, o_ref, buf_ref, sem_ref):
    rows = h_ref.shape[0]
    i = pl.program_id(0)
    slot = i % 2

    def start(step, s):
        _start_rows(y_ref, p0_ref, step * rows, rows, buf_ref.at[s], 0, sem_ref.at[s])
        _start_rows(y_ref, p1_ref, step * rows, rows, buf_ref.at[s], rows, sem_ref.at[s])

    @pl.when(i == 0)
    def _():
        start(0, 0)

    @pl.when(i + 1 < pl.num_programs(0))
    def _():
        start(i + 1, 1 - slot)

    _wait_rows(y_ref, 2 * rows, buf_ref.at[slot], 0, sem_ref.at[slot])
    x = DEEPNORM_ALPHA * h_ref[...] + (buf_ref[slot, 0:rows, :] + buf_ref[slot, rows:2 * rows, :])
    mu = jnp.mean(x, axis=-1, keepdims=True)
    xc = x - mu
    var = jnp.mean(xc * xc, axis=-1, keepdims=True)
    o_ref[...] = xc * lax.rsqrt(var + LN_EPS) * g_ref[...] + b_ref[...]


def combine_deepnorm_ln(h, y, pos0, pos1, g, b, rows=128):
    m, d = h.shape
    assert m % rows == 0
    row = pl.BlockSpec((rows, d), lambda i, p0, p1: (i, 0))
    vec = pl.BlockSpec((1, d), lambda i, p0, p1: (0, 0))
    spec = pltpu.PrefetchScalarGridSpec(
        num_scalar_prefetch=2,
        grid=(m // rows,),
        in_specs=[row, pl.BlockSpec(memory_space=pl.ANY), vec, vec],
        out_specs=row,
        scratch_shapes=[pltpu.VMEM((2, 2 * rows, d), y.dtype), pltpu.SemaphoreType.DMA((2,))],
    )
    return pl.pallas_call(
        _combine_ln_body, grid_spec=spec,
        out_shape=jax.ShapeDtypeStruct((m, d), F32),
        compiler_params=_cparams(("arbitrary",)),
        name="combine_deepnorm_ln",
    )(pos0, pos1, h, y, g.reshape(1, d), b.reshape(1, d))


def _conv_silu(raw_ref, stage_ref, w_ref, b_ref, first):
    blk = raw_ref.shape[0]
    kk = w_ref.shape[0]

    @pl.when(first)
    def _():
        stage_ref[0:SUBLANES, :] = jnp.zeros((SUBLANES, stage_ref.shape[1]), F32)

    stage_ref[SUBLANES:SUBLANES + blk, :] = raw_ref[...]
    y = b_ref[...]
    for i in range(kk):
        y = y + w_ref[i:i + 1, :] * stage_ref[pl.ds(SUBLANES - (kk - 1) + i, blk), :]
    stage_ref[0:SUBLANES, :] = stage_ref[blk:blk + SUBLANES, :]
    return _silu(y)


def _ssd_body(z_ref, xr_ref, br_ref, cr_ref, dtr_ref, cwx_ref, cwb_ref, cwc_ref, cbx_ref, cbb_ref, cbc_ref,
              dtb_ref, aneg_ref, dskip_ref, nw_ref, o_ref,
              state_ref, y_ref, sx_ref, sb_ref, sc_ref, *, heads, head_dim, groups, n_state):
    blk = xr_ref.shape[0]
    hpg = heads // groups
    gw = hpg * head_dim
    inner = heads * head_dim
    first = pl.program_id(1) == 0

    @pl.when(first)
    def _():
        state_ref[...] = jnp.zeros_like(state_ref)

    xs = _conv_silu(xr_ref, sx_ref, cwx_ref, cbx_ref, first)
    bmat = _conv_silu(br_ref, sb_ref, cwb_ref, cbb_ref, first).astype(BF16)
    cmat = _conv_silu(cr_ref, sc_ref, cwc_ref, cbc_ref, first).astype(BF16)

    row = lax.broadcasted_iota(jnp.int32, (blk, blk), 0)
    col = lax.broadcasted_iota(jnp.int32, (blk, blk), 1)
    causal = row >= col
    eh = lax.broadcasted_iota(jnp.int32, (heads, inner), 0)
    ec = lax.broadcasted_iota(jnp.int32, (heads, inner), 1)
    expand = (ec // head_dim == eh).astype(F32)

    dtp = dtr_ref[:, :heads] + dtb_ref[...]
    dt = jnp.maximum(dtp, 0.0) + jnp.log(1.0 + jnp.exp(-jnp.abs(dtp)))
    a = dt * aneg_ref[...]
    cs = jnp.dot(causal.astype(F32), a, preferred_element_type=F32, precision=HIGHEST)
    cs_t = cs.T
    cs_x = jnp.dot(cs, expand, preferred_element_type=F32, precision=HIGHEST)
    dt_x = jnp.dot(dt, expand, preferred_element_type=F32, precision=HIGHEST)
    xdt = xs * dt_x
    last_x = cs_x[blk - 1:blk, :]
    x_in = (xdt * jnp.exp(last_x - cs_x)).astype(BF16)
    x_b = xdt.astype(BF16)
    grow = jnp.exp(cs_x)

    for g in range(groups):
        bg = bmat[:, g * n_state:(g + 1) * n_state]
        cg = cmat[:, g * n_state:(g + 1) * n_state]
        cb = lax.dot_general(cg, bg, (((1,), (1,)), ((), ())), preferred_element_type=F32)
        st = state_ref[g]
        y_off = jnp.dot(cg, st.astype(BF16), preferred_element_type=F32)
        y_ref[:, g * gw:(g + 1) * gw] = y_off * grow[:, g * gw:(g + 1) * gw]
        for r in range(hpg):
            h = g * hpg + r
            seg = cs[:, h:h + 1] - cs_t[h:h + 1, :]
            dec = jnp.exp(jnp.where(causal, seg, -jnp.inf))
            mh = (cb * dec).astype(BF16)
            yh = jnp.dot(mh, x_b[:, h * head_dim:(h + 1) * head_dim], preferred_element_type=F32)
            y_ref[:, h * head_dim:(h + 1) * head_dim] += yh
        new = lax.dot_general(bg, x_in[:, g * gw:(g + 1) * gw], (((0,), (0,)), ((), ())),
                              preferred_element_type=F32)
        state_ref[g] = st * jnp.exp(last_x[:, g * gw:(g + 1) * gw]) + new

    y = y_ref[...] + dskip_ref[...] * xs
    y = y * _silu(z_ref[...])
    for g in range(groups):
        yg = y[:, g * gw:(g + 1) * gw]
        ms = jnp.mean(yg * yg, axis=-1, keepdims=True)
        o_ref[:, g * gw:(g + 1) * gw] = (yg * lax.rsqrt(ms + RMS_EPS)
                                         * nw_ref[:, g * gw:(g + 1) * gw]).astype(o_ref.dtype)


def ssd_mixer(zx, dt_raw, conv_w, conv_b, dt_bias, a_neg, d_skip_x, norm_w, *, batch, heads=SSD_HEADS,
              head_dim=SSD_HEAD_DIM, groups=SSD_GROUPS, n_state=SSD_STATE, blk=SSD_BLOCK):
    m = zx.shape[0]
    inner = heads * head_dim
    bc = groups * n_state
    seq = m // batch
    nblk = seq // blk
    gw = inner // groups
    kk = conv_w.shape[0]
    rows = lambda w, cb: pl.BlockSpec((blk, w), lambda b, c: (b * nblk + c, cb))
    full = lambda r, w: pl.BlockSpec((r, w), lambda b, c: (0, 0))
    body = functools.partial(_ssd_body, heads=heads, head_dim=head_dim, groups=groups, n_state=n_state)
    cw = (conv_w[:, :inner], conv_w[:, inner:inner + bc], conv_w[:, inner + bc:])
    cb = tuple(t.reshape(1, -1) for t in (conv_b[:inner], conv_b[inner:inner + bc], conv_b[inner + bc:]))
    return pl.pallas_call(
        body,
        grid=(batch, nblk),
        in_specs=[rows(inner, 0), rows(inner, 1), rows(bc, 2 * inner // bc), rows(bc, 2 * inner // bc + 1),
                  rows(dt_raw.shape[1], 0),
                  full(kk, inner), full(kk, bc), full(kk, bc), full(1, inner), full(1, bc), full(1, bc),
                  full(1, heads), full(1, heads), full(1, inner), full(1, inner)],
        out_specs=rows(inner, 0),
        out_shape=jax.ShapeDtypeStruct((m, inner), BF16),
        scratch_shapes=[pltpu.VMEM((groups, n_state, gw), F32), pltpu.VMEM((blk, inner), F32),
                        pltpu.VMEM((blk + SUBLANES, inner), F32), pltpu.VMEM((blk + SUBLANES, bc), F32),
                        pltpu.VMEM((blk + SUBLANES, bc), F32)],
        compiler_params=_cparams(("parallel", "arbitrary")),
        name="ssd_mixer",
    )(zx, zx, zx, zx, dt_raw, *cw, *cb, dt_bias.reshape(1, heads), a_neg.reshape(1, heads),
      d_skip_x.reshape(1, inner), norm_w.reshape(1, inner))


def _attn_body(lam_ref, q_ref, k_ref, v_ref, nw_ref, o_ref, acc_ref, *, post_scale, chunk, hd):
    t = q_ref.shape[0]
    hps = q_ref.shape[1] // hd
    half = hd // 2
    i = pl.program_id(2)
    sub = lax.broadcasted_iota(jnp.int32, (hd, t), 0)
    qs_t = []
    for hh in range(hps):
        qt = q_ref[:, hh * hd:(hh + 1) * hd].astype(F32).T
        qs_t.append(jnp.concatenate([jnp.where(sub < half, qt, 0.0), jnp.where(sub >= half, qt, 0.0)],
                                    axis=1).astype(BF16))
    acc_ref[...] = jnp.zeros_like(acc_ref)

    def scores(j, hh):
        start = pl.multiple_of(j * t, t)
        kb = k_ref[pl.ds(start, t), hh * hd:(hh + 1) * hd]
        return jnp.dot(kb, qs_t[hh], preferred_element_type=F32)

    def step(j, carry, last):
        start = pl.multiple_of(j * t, t)
        out = []
        for hh in range(hps):
            m, l, s = carry[hh]
            s_next = s if last else scores(j + 1, hh)
            vb = v_ref[pl.ds(start, t), hh * hd:(hh + 1) * hd]
            if last:
                kk = lax.broadcasted_iota(jnp.int32, (t, 2 * t), 0)
                qq = lax.broadcasted_iota(jnp.int32, (t, 2 * t), 1)
                qq = jnp.where(qq >= t, qq - t, qq)
                s = jnp.where(kk // chunk <= qq // chunk, s, -jnp.inf)
            m_new = jnp.maximum(m, jnp.max(s, axis=0, keepdims=True))
            alpha = jnp.exp2(m - m_new)
            p = jnp.exp2(s - m_new)
            l = alpha * l + jnp.sum(p, axis=0, keepdims=True)
            pv = lax.dot_general(vb, p.astype(BF16), (((0,), (0,)), ((), ())),
                                 preferred_element_type=F32)
            acc_ref[hh] = alpha * acc_ref[hh] + pv
            out.append((m_new, l, s_next))
        return tuple(out)

    init = tuple((jnp.full((1, 2 * t), -jnp.inf, F32), jnp.zeros((1, 2 * t), F32), scores(0, hh))
                 for hh in range(hps))
    carry = lax.fori_loop(0, i, lambda j, c: step(j, c, False), init)
    carry = step(i, carry, True)
    for hh in range(hps):
        o = acc_ref[hh] * (1.0 / carry[hh][1])
        out = o[:, :t] - lam_ref[0] * o[:, t:]
        ms = jnp.mean(out * out, axis=0, keepdims=True)
        out = out * lax.rsqrt(ms + RMS_EPS) * (nw_ref[...] * post_scale)
        o_ref[:, hh * hd:(hh + 1) * hd] = out.T.astype(o_ref.dtype)


def diff_attention(q, k, v, lam, norm_w, post_scale, *, batch, heads=DIFF_HEADS, blk=ATTN_BLOCK,
                   chunk=CHUNK, hps=ATTN_HEADS_PER_STEP):
    m, width = q.shape
    hd = width // heads
    seq = m // batch
    nq = seq // blk
    q3, k3, v3 = (t.reshape(batch, seq, width) for t in (q, k, v))
    body = functools.partial(_attn_body, post_scale=post_scale, chunk=chunk, hd=hd)
    out = pl.pallas_call(
        body,
        grid=(batch, heads // hps, nq),
        in_specs=[pl.BlockSpec(memory_space=pltpu.SMEM),
                  pl.BlockSpec((None, blk, hps * hd), lambda b, h, i: (b, i, h)),
                  pl.BlockSpec((None, seq, hps * hd), lambda b, h, i: (b, 0, h)),
                  pl.BlockSpec((None, seq, hps * hd), lambda b, h, i: (b, 0, h)),
                  pl.BlockSpec((hd, 1), lambda b, h, i: (0, 0))],
        out_specs=pl.BlockSpec((None, blk, hps * hd), lambda b, h, i: (b, i, h)),
        out_shape=jax.ShapeDtypeStruct((batch, seq, width), BF16),
        scratch_shapes=[pltpu.VMEM((hps, hd, 2 * blk), F32)],
        compiler_params=_cparams(("parallel", "parallel", "arbitrary")),
        name="diff_attention",
    )(lam.reshape(1).astype(F32), q3, k3, v3, norm_w.reshape(hd, 1))
    return out.reshape(m, width)


def _ret_body(q_ref, k_ref, v_ref, g_ref, dmat_ref, qd_ref, kd_ref, bd_ref, o_ref, state_ref):
    hps = dmat_ref.shape[0]
    dk = q_ref.shape[1] // hps
    dv = v_ref.shape[1] // hps

    @pl.when(pl.program_id(2) == 0)
    def _():
        state_ref[...] = jnp.zeros_like(state_ref)

    for hh in range(hps):
        qb = q_ref[:, hh * dk:(hh + 1) * dk].astype(BF16)
        k = k_ref[:, hh * dk:(hh + 1) * dk]
        vb = v_ref[:, hh * dv:(hh + 1) * dv].astype(BF16)
        s = lax.dot_general(qb, k.astype(BF16), (((1,), (1,)), ((), ())), preferred_element_type=F32)
        y = jnp.dot((s * dmat_ref[hh]).astype(BF16), vb, preferred_element_type=F32)
        st = state_ref[hh]
        y = y + jnp.dot(qb, st.astype(BF16), preferred_element_type=F32) * qd_ref[hh]
        kdec = (k.astype(F32) * kd_ref[hh]).astype(BF16)
        state_ref[hh] = st * bd_ref[hh] + lax.dot_general(kdec, vb, (((0,), (0,)), ((), ())),
                                                          preferred_element_type=F32)
        mu = jnp.mean(y, axis=-1, keepdims=True)
        yc = y - mu
        var = jnp.mean(yc * yc, axis=-1, keepdims=True)
        gate = _silu(g_ref[:, hh * dv:(hh + 1) * dv].astype(F32))
        o_ref[:, hh * dv:(hh + 1) * dv] = (gate * (yc * lax.rsqrt(var + LN_EPS))).astype(o_ref.dtype)


def _retention_tables(heads, blk, chunk):
    log_g = jnp.log1p(-(2.0 ** (-5.0 - jnp.arange(heads, dtype=F32))))
    idx = jnp.arange(blk, dtype=F32)
    n, mm = idx[:, None], idx[None, :]
    cn, cm = jnp.floor(n / chunk), jnp.floor(mm / chunk)
    dist = jnp.where(cn == cm, jnp.abs(n - mm), n - mm)
    dmat = jnp.where((cm <= cn)[None], jnp.exp(dist[None] * log_g[:, None, None]), 0.0)
    qd = jnp.exp((idx + 1.0)[None, :, None] * log_g[:, None, None])
    kd = jnp.exp((blk - 1.0 - idx)[None, :, None] * log_g[:, None, None])
    bd = jnp.exp(blk * log_g)[:, None, None]
    return dmat, qd, kd, bd


def retention_mixer(q, k, vg, *, batch, heads=RET_HEADS, blk=RET_BLOCK, chunk=CHUNK, hps=RET_HEADS_PER_STEP):
    m = q.shape[0]
    dk = q.shape[1] // heads
    dv = vg.shape[1] // (2 * heads)
    seq = m // batch
    nblk = seq // blk
    groups = heads // hps
    dmat, qd, kd, bd = _retention_tables(heads, blk, chunk)
    rows = lambda w, off: pl.BlockSpec((blk, hps * w), lambda b, h, c: (b * nblk + c, h + off))
    table = lambda r, w: pl.BlockSpec((hps, r, w), lambda b, h, c: (h, 0, 0))
    return pl.pallas_call(
        _ret_body,
        grid=(batch, groups, nblk),
        in_specs=[rows(dk, 0), rows(dk, 0), rows(dv, 0), rows(dv, groups),
                  table(blk, blk), table(blk, 1), table(blk, 1), table(1, 1)],
        out_specs=rows(dv, 0),
        out_shape=jax.ShapeDtypeStruct((m, heads * dv), BF16),
        scratch_shapes=[pltpu.VMEM((hps, dk, dv), F32)],
        compiler_params=_cparams(("parallel", "parallel", "arbitrary")),
        name="retention_mixer",
    )(q, k, vg, vg, dmat, qd, kd, bd)


def _router_body(h_ref, w_ref, idx_ref, wt_ref, *, n_experts):
    logits = jnp.dot(h_ref[...], w_ref[...], preferred_element_type=F32, precision=HIGHEST)
    lane = lax.broadcasted_iota(jnp.int32, logits.shape, 1)
    logits = jnp.where(lane < n_experts, logits, -jnp.inf)
    v1 = jnp.max(logits, axis=-1, keepdims=True)
    i1 = jnp.min(jnp.where(logits == v1, lane, LANES), axis=-1, keepdims=True)
    rest = jnp.where(lane == i1, -jnp.inf, logits)
    v2 = jnp.max(rest, axis=-1, keepdims=True)
    i2 = jnp.min(jnp.where(rest == v2, lane, LANES), axis=-1, keepdims=True)
    e2 = jnp.exp(v2 - v1)
    w1 = 1.0 / (1.0 + e2)
    w2 = e2 / (1.0 + e2)
    idx_ref[...] = jnp.where(lane == 0, i1, jnp.where(lane == 1, i2, 0))
    wt_ref[...] = jnp.where(lane == 0, w1, jnp.where(lane == 1, w2, 0.0))


def moe_router(h, w_router, tm=512):
    m, d = h.shape
    n_experts = w_router.shape[1]
    w_pad = jnp.zeros((d, LANES), F32).at[:, :n_experts].set(w_router)
    row = pl.BlockSpec((tm, LANES), lambda i: (i, 0))
    idx, wt = pl.pallas_call(
        functools.partial(_router_body, n_experts=n_experts),
        grid=(m // tm,),
        in_specs=[pl.BlockSpec((tm, d), lambda i: (i, 0)), pl.BlockSpec((d, LANES), lambda i: (0, 0))],
        out_specs=[row, row],
        out_shape=[jax.ShapeDtypeStruct((m, LANES), jnp.int32), jax.ShapeDtypeStruct((m, LANES), F32)],
        compiler_params=_cparams(("parallel",)),
        name="moe_router",
    )(h, w_pad)
    return idx[:, :TOP_K], wt[:, :TOP_K]


def _expert_weights(gord_ref, gexp_ref, meta_ref, w_refs, wf_refs, sem_ref, wb_ref):
    j, t = pl.program_id(0), pl.program_id(1)
    tn = wf_refs[0].shape[2]
    n_groups = meta_ref[1]
    g = gord_ref[t]

    def copies(expert, col, slot):
        start = pl.multiple_of(col * tn, tn)
        return [pltpu.make_async_copy(w_ref.at[expert, :, pl.ds(start, tn)], wf_ref.at[slot], sem_ref.at[slot])
                for w_ref, wf_ref in zip(w_refs, wf_refs)]

    @pl.when(jnp.logical_or(t == 0, g != gord_ref[jnp.maximum(t - 1, 0)]))
    def _():
        seq_no = j * n_groups + g
        slot = seq_no % 2

        @pl.when(seq_no == 0)
        def _():
            for c in copies(gexp_ref[0], 0, 0):
                c.start()

        for c in copies(0, 0, slot):
            c.wait()
        more = g + 1 < n_groups

        @pl.when(jnp.logical_or(more, j + 1 < pl.num_programs(0)))
        def _():
            for c in copies(gexp_ref[jnp.where(more, g + 1, 0)], jnp.where(more, j, j + 1), 1 - slot):
                c.start()

        for i, wf_ref in enumerate(wf_refs):
            wb_ref[:, i * tn:(i + 1) * tn] = wf_ref[slot].astype(BF16)


def _moe_up_body(gord_ref, gexp_ref, meta_ref, x_ref, wg_ref, wu_ref, o_ref, wgf_ref, wuf_ref, sem_ref, wb_ref):
    tn = o_ref.shape[1]

    @pl.when(pl.program_id(1) < meta_ref[0])
    def _():
        _expert_weights(gord_ref, gexp_ref, meta_ref, (wg_ref, wu_ref), (wgf_ref, wuf_ref), sem_ref, wb_ref)
        for rows in _row_splits(o_ref.shape[0]):
            gu = jnp.dot(x_ref[rows, :], wb_ref[...], preferred_element_type=F32)
            o_ref[rows, :] = (_silu(gu[:, :tn]) * gu[:, tn:]).astype(o_ref.dtype)

    @pl.when(pl.program_id(1) >= meta_ref[0])
    def _():
        o_ref[...] = jnp.zeros_like(o_ref)


def _moe_down_body(gord_ref, gexp_ref, meta_ref, a_ref, w_ref, rw_ref, o_ref, wf_ref, sem_ref, wb_ref):
    @pl.when(pl.program_id(1) < meta_ref[0])
    def _():
        _expert_weights(gord_ref, gexp_ref, meta_ref, (w_ref,), (wf_ref,), sem_ref, wb_ref)
        for rows in _row_splits(o_ref.shape[0]):
            y = jnp.dot(a_ref[rows, :], wb_ref[...], preferred_element_type=F32)
            o_ref[rows, :] = (y * rw_ref[rows, :]).astype(o_ref.dtype)

    @pl.when(pl.program_id(1) >= meta_ref[0])
    def _():
        o_ref[...] = jnp.zeros_like(o_ref)


def moe_experts(x_sorted, group_of_tile, group_expert, meta, row_w, w_gate, w_up, w_down, tm=MOE_TM, tn=512):
    p, d = x_sorted.shape
    f = w_gate.shape[2]
    nt = p // tm
    hbm = pl.BlockSpec(memory_space=pl.ANY)
    fetch = lambda k: pltpu.VMEM((2, k, tn), F32)
    up_spec = pltpu.PrefetchScalarGridSpec(
        num_scalar_prefetch=3,
        grid=(f // tn, nt),
        in_specs=[pl.BlockSpec((tm, d), lambda j, t, *_: (t, 0)), hbm, hbm],
        out_specs=pl.BlockSpec((tm, tn), lambda j, t, *_: (t, j)),
        scratch_shapes=[fetch(d), fetch(d), pltpu.SemaphoreType.DMA((2,)), pltpu.VMEM((d, 2 * tn), BF16)],
    )
    act = pl.pallas_call(
        _moe_up_body, grid_spec=up_spec,
        out_shape=jax.ShapeDtypeStruct((p, f), BF16),
        compiler_params=_cparams(("arbitrary", "arbitrary")),
        name="moe_up",
    )(group_of_tile, group_expert, meta, x_sorted, w_gate, w_up)
    down_spec = pltpu.PrefetchScalarGridSpec(
        num_scalar_prefetch=3,
        grid=(d // tn, nt),
        in_specs=[pl.BlockSpec((tm, f), lambda j, t, *_: (t, 0)), hbm,
                  pl.BlockSpec((tm, 1), lambda j, t, *_: (t, 0))],
        out_specs=pl.BlockSpec((tm, tn), lambda j, t, *_: (t, j)),
        scratch_shapes=[fetch(f), pltpu.SemaphoreType.DMA((2,)), pltpu.VMEM((f, tn), BF16)],
    )
    return pl.pallas_call(
        _moe_down_body, grid_spec=down_spec,
        out_shape=jax.ShapeDtypeStruct((p, d), F32),
        compiler_params=_cparams(("arbitrary", "arbitrary")),
        name="moe_down",
    )(group_of_tile, group_expert, meta, act, w_down, row_w)


def moe_routing(top_i, top_w, n_experts, tm):
    m = top_i.shape[0]
    pairs = m * TOP_K
    p = pairs + n_experts * tm
    flat_e = top_i.reshape(pairs)
    onehot = (flat_e[:, None] == jnp.arange(n_experts, dtype=jnp.int32)[None, :]).astype(jnp.int32)
    rank = jnp.sum((jnp.cumsum(onehot, axis=0) - onehot) * onehot, axis=1)
    counts = jnp.sum(onehot, axis=0)
    padded = ((counts + tm - 1) // tm) * tm
    ends = jnp.cumsum(padded)
    starts = ends - padded
    dest = starts[flat_e] + rank
    row_token = jnp.zeros((p,), jnp.int32).at[dest].set(jnp.arange(pairs, dtype=jnp.int32) // TOP_K)
    row_w = jnp.zeros((p,), F32).at[dest].set(top_w.reshape(pairs)).reshape(p, 1)
    tile_start = jnp.arange(p // tm, dtype=jnp.int32) * tm
    tile_expert = jnp.minimum(jnp.sum((tile_start[:, None] >= ends[None, :]).astype(jnp.int32), axis=1),
                              n_experts - 1).astype(jnp.int32)
    nonempty = (counts > 0).astype(jnp.int32)
    group_of_tile = (jnp.cumsum(nonempty) - 1)[tile_expert].astype(jnp.int32)
    group_expert = jnp.argsort(1 - nonempty, stable=True).astype(jnp.int32)
    meta = jnp.stack([ends[-1] // tm, jnp.sum(nonempty)]).astype(jnp.int32)
    return row_token, row_w, dest.reshape(m, TOP_K), group_of_tile, group_expert, meta


def _rotary_tables(inv_freq, seq, blocks, scale):
    ang = jnp.arange(seq, dtype=F32)[:, None] * inv_freq[None, :]
    cos, sin, zero = jnp.cos(ang) * scale, jnp.sin(ang) * scale, jnp.zeros_like(ang)
    rep = lambda a, b: jnp.tile(jnp.concatenate([a, b], axis=1), (1, blocks))
    return rep(cos, cos), rep(-sin, zero), rep(zero, sin), inv_freq.shape[0]


def _layer_even(hb, batch, seq, w_in, conv_w, conv_b, dt_bias, a_log, d_skip, ssd_norm_w,
                lam_q1, lam_k1, lam_q2, lam_k2, diff_norm_w, w_out, lambda_init):
    zx_cols = SSD_INNER + SSD_XBC
    qkv_lo = zx_cols + SSD_HEADS
    zx = project([hb], w_in, F32, 1024, 512, n=zx_cols)
    dt_raw = project([hb], w_in, F32, 1024, LANES, col_lo=zx_cols, n=LANES)
    inv_freq = ROPE_THETA ** (-jnp.arange(0, DIFF_QK_DIM, 2, dtype=F32) / DIFF_QK_DIM)
    rot_q = _rotary_tables(inv_freq, seq, 2, DIFF_QK_DIM ** -0.5 * math.log2(math.e))
    rot_k = _rotary_tables(inv_freq, seq, 2, 1.0)
    q = project([hb], w_in, BF16, 1024, 512, col_lo=qkv_lo, n=DIFF_WIDTH, rot=rot_q, seq=seq)
    k = project([hb], w_in, BF16, 1024, 512, col_lo=qkv_lo + DIFF_WIDTH, n=DIFF_WIDTH, rot=rot_k, seq=seq)
    v = project([hb], w_in, BF16, 1024, 512, col_lo=qkv_lo + 2 * DIFF_WIDTH, n=DIFF_WIDTH)
    y = ssd_mixer(zx, dt_raw, conv_w, conv_b, dt_bias, -jnp.exp(a_log), jnp.repeat(d_skip, SSD_HEAD_DIM),
                  ssd_norm_w, batch=batch)
    lam = (jnp.exp(jnp.sum(lam_q1 * lam_k1)) - jnp.exp(jnp.sum(lam_q2 * lam_k2)) + lambda_init)
    o = diff_attention(q, k, v, lam, diff_norm_w, 1.0 - lambda_init, batch=batch)
    return project([y, o], w_out.astype(BF16), F32, 512, 512)


def _layer_odd_mixer(hb, batch, seq, w_in, w_out):
    inv_freq = 1.0 / (ROPE_THETA ** jnp.linspace(0.0, 1.0, RET_QK_DIM // 2, dtype=F32))
    rot_q = _rotary_tables(inv_freq, seq, 1, 1.0)
    rot_k = _rotary_tables(inv_freq, seq, 1, RET_QK_DIM ** -0.5)
    q = project([hb], w_in, BF16, 1024, 512, n=RET_QK_WIDTH, rot=rot_q, seq=seq)
    k = project([hb], w_in, BF16, 1024, 512, col_lo=RET_QK_WIDTH, n=RET_QK_WIDTH, rot=rot_k, seq=seq)
    vg = project([hb], w_in, BF16, 1024, 512, col_lo=2 * RET_QK_WIDTH, n=2 * RET_V_WIDTH)
    o = retention_mixer(q, k, vg, batch=batch)
    return project([o], w_out.astype(BF16), F32, 512, 512)


def _moe_ln(h, w_router, w_gate, w_up, w_down, ln_g, ln_b):
    top_i, top_w = moe_router(h, w_router)
    row_token, row_w, pos, group_of_tile, group_expert, meta = moe_routing(top_i, top_w, N_EXPERTS, MOE_TM)
    x_sorted = gather_rows(h, row_token, BF16)
    y = moe_experts(x_sorted, group_of_tile, group_expert, meta, row_w, w_gate, w_up, w_down)
    return combine_deepnorm_ln(h, y, pos[:, 0], pos[:, 1], ln_g, ln_b)


def kernel(x, ab_w_in, ab_conv_w, ab_conv_b, ab_dt_bias, ab_a_log, ab_d_skip, ab_ssd_norm_w, ab_lam_q1, ab_lam_k1, ab_lam_q2, ab_lam_k2, ab_diff_norm_w, ab_w_out, ffn_w_gate, ffn_w_up, ffn_w_down, c_w_in, c_w_out, moe_w_router, moe_w_gate, moe_w_up, moe_w_down, ln_mix_g, ln_mix_b, ln_ffn_g, ln_ffn_b):
    batch, seq, d = x.shape
    m = batch * seq
    h = x.reshape(m, d)
    hb = h.astype(BF16)
    lambda_init = 0.8 - 0.6 * math.exp(-0.3 * 0)
    mix = _layer_even(hb, batch, seq, ab_w_in, ab_conv_w[0], ab_conv_b[0], ab_dt_bias[0], ab_a_log[0],
                      ab_d_skip[0], ab_ssd_norm_w[0], ab_lam_q1[0], ab_lam_k1[0], ab_lam_q2[0], ab_lam_k2[0],
                      ab_diff_norm_w[0], ab_w_out[0], lambda_init)
    h, hb = deepnorm_ln(h, mix, ln_mix_g[0], ln_mix_b[0])
    act = swiglu_up(hb, ffn_w_gate[0], ffn_w_up[0], 1024, 256)
    ff = matmul(act, ffn_w_down[0].astype(BF16), F32, 512, 256)
    h, hb = deepnorm_ln(h, ff, ln_ffn_g[0], ln_ffn_b[0])
    mix = _layer_odd_mixer(hb, batch, seq, c_w_in, c_w_out[0])
    h, _ = deepnorm_ln(h, mix, ln_mix_g[1], ln_mix_b[1])
    h = _moe_ln(h, moe_w_router[0], moe_w_gate[0], moe_w_up[0], moe_w_down[0], ln_ffn_g[1], ln_ffn_b[1])
    return h.reshape(batch, seq, d)
```

```python
import functools
import math

import jax
import jax.numpy as jnp
from jax import lax
from jax.experimental import pallas as pl
from jax.experimental.pallas import tpu as pltpu

F32 = jnp.float32
BF16 = jnp.bfloat16
HIGHEST = lax.Precision.HIGHEST

D_MODEL = 4096
DEPTH = 2
CHUNK = 64
SSD_HEADS = 64
SSD_HEAD_DIM = 64
SSD_INNER = SSD_HEADS * SSD_HEAD_DIM
SSD_GROUPS = 8
SSD_STATE = 128
SSD_CONV = 4
SSD_BC = SSD_GROUPS * SSD_STATE
SSD_XBC = SSD_INNER + 2 * SSD_BC
DIFF_HEADS = 32
DIFF_QK_DIM = 64
DIFF_V_DIM = 2 * DIFF_QK_DIM
DIFF_WIDTH = DIFF_HEADS * DIFF_V_DIM
ROPE_THETA = 10000.0
RET_HEADS = 16
RET_QK_DIM = D_MODEL // RET_HEADS
RET_V_DIM = 2 * RET_QK_DIM
RET_QK_WIDTH = RET_HEADS * RET_QK_DIM
RET_V_WIDTH = RET_HEADS * RET_V_DIM
FFN_DIM = 11008
N_EXPERTS = 8
TOP_K = 2
EXPERT_DIM = 4096
DEEPNORM_ALPHA = (2 * DEPTH) ** 0.25
LN_EPS = 1e-5
RMS_EPS = 1e-5

LANES = 128
SUBLANES = 8
VMEM_LIMIT_BYTES = 56 * 1024 * 1024

SSD_BLOCK = 128
RET_BLOCK = 256
RET_HEADS_PER_STEP = 2
ATTN_BLOCK = 256
ATTN_HEADS_PER_STEP = 4
MOE_TM = 512


def _cparams(sem):
    return pltpu.CompilerParams(dimension_semantics=sem, vmem_limit_bytes=VMEM_LIMIT_BYTES)


def _silu(x):
    return x * (1.0 / (1.0 + jnp.exp(-x)))


SPLIT_ROWS = 512


def _row_splits(tm):
    return (slice(0, tm // 2), slice(tm // 2, tm)) if tm >= SPLIT_ROWS else (slice(0, tm),)


def _weight_bf16(w_ref, wb_ref, first):
    if wb_ref is None:
        return w_ref

    @pl.when(first)
    def _():
        wb_ref[...] = w_ref[...].astype(BF16)

    return wb_ref


def _rotate(acc, cos_ref, sa_ref, sb_ref, half, rows):
    tn = acc.shape[1]
    reps = tn // cos_ref.shape[1]
    tile = lambda r: jnp.tile(r[rows, :], (1, reps))
    return (acc * tile(cos_ref) + pltpu.roll(acc, tn - half, 1) * tile(sa_ref)
            + pltpu.roll(acc, half, 1) * tile(sb_ref))


REALIGN_ROWS = 512


def _weight_bf16_shifted(w_ref, wn_ref, wb_ref, first, shift):
    k, tn = w_ref.shape
    width = tn + wn_ref.shape[1]

    @pl.when(first)
    def _():
        for r0 in range(0, k, REALIGN_ROWS):
            rows = slice(r0, min(r0 + REALIGN_ROWS, k))
            window = jnp.concatenate([w_ref[rows, :], wn_ref[rows, :]], axis=1)
            wb_ref[rows, :] = pltpu.roll(window, width - shift, 1)[:, :tn].astype(BF16)

    return wb_ref


def _proj_body(*refs, n_a, rot_half, cast, shift):
    a_refs = refs[:n_a]
    w_ref = refs[n_a]
    pos = n_a + 1
    if shift:
        wn_ref = refs[pos]
        pos += 1
    if rot_half:
        cos_ref, sa_ref, sb_ref = refs[pos:pos + 3]
        pos += 3
    o_ref = refs[pos]
    wb_ref = refs[pos + 1] if cast else None
    first = pl.program_id(1) == 0
    if shift:
        w = _weight_bf16_shifted(w_ref, wn_ref, wb_ref, first, shift)
    else:
        w = _weight_bf16(w_ref, wb_ref, first)
    for rows in _row_splits(o_ref.shape[0]):
        k0 = 0
        acc = None
        for a_ref in a_refs:
            ka = a_ref.shape[1]
            part = jnp.dot(a_ref[rows, :], w[k0:k0 + ka, :], preferred_element_type=F32)
            acc = part if acc is None else acc + part
            k0 += ka
        if rot_half:
            acc = _rotate(acc, cos_ref, sa_ref, sb_ref, rot_half, rows)
        o_ref[rows, :] = acc.astype(o_ref.dtype)


def project(a_list, w, out_dtype, tm, tn, *, col_lo=0, n=None, rot=None, seq=None):
    m = a_list[0].shape[0]
    k, width = w.shape[-2:]
    n = width - col_lo if n is None else n
    shift = col_lo % LANES
    base = col_lo - shift
    cast = w.dtype != BF16
    assert m % tm == 0 and n % tn == 0 and base % tn == 0 and sum(a.shape[1] for a in a_list) == k
    assert cast or not shift
    assert w.ndim == 2 or w.shape[0] == 1
    off = base // tn
    lead = (None,) * (w.ndim - 2)
    wspec = lambda cols, col_of: pl.BlockSpec(lead + (k, cols), lambda j, i: (0,) * (w.ndim - 1) + (col_of(j),))
    in_specs = [pl.BlockSpec((tm, a.shape[1]), lambda j, i: (i, 0)) for a in a_list]
    in_specs.append(wspec(tn, lambda j: j + off))
    args = list(a_list) + [w]
    if shift:
        lanes_per_tile = tn // LANES
        in_specs.append(wspec(LANES, lambda j: (j + off + 1) * lanes_per_tile))
        args.append(w)
    rot_half = 0
    if rot is not None:
        cos, sa, sb, rot_half = rot
        nrow = seq // tm
        tab = pl.BlockSpec((tm, cos.shape[1]), lambda j, i: (i % nrow, 0))
        in_specs += [tab, tab, tab]
        args += [cos, sa, sb]
    return pl.pallas_call(
        functools.partial(_proj_body, n_a=len(a_list), rot_half=rot_half, cast=cast, shift=shift),
        grid=(n // tn, m // tm),
        in_specs=in_specs,
        out_specs=pl.BlockSpec((tm, tn), lambda j, i: (i, j)),
        out_shape=jax.ShapeDtypeStruct((m, n), out_dtype),
        scratch_shapes=[pltpu.VMEM((k, tn), BF16)] if cast else [],
        compiler_params=_cparams(("parallel", "arbitrary")),
        name="project",
    )(*args)


def _proj_t_body(*refs, rot_half, row_lo):
    a_ref, wt_ref = refs[:2]
    pos = 2
    if rot_half:
        cos_ref, sa_ref, sb_ref = refs[pos:pos + 3]
        pos += 3
    o_ref, wf_ref, sem_ref, wb_ref = refs[pos:pos + 4]
    j = pl.program_id(0)
    tn, k = wf_ref.shape[1:]

    def fetch(col_tile, slot):
        start = pl.multiple_of(row_lo + col_tile * tn, SUBLANES)
        return pltpu.make_async_copy(wt_ref.at[0, pl.ds(start, tn), :], wf_ref.at[slot], sem_ref.at[slot])

    @pl.when(pl.program_id(1) == 0)
    def _():
        slot = j % 2

        @pl.when(j == 0)
        def _():
            fetch(0, 0).start()

        fetch(0, slot).wait()

        @pl.when(j + 1 < pl.num_programs(0))
        def _():
            fetch(j + 1, 1 - slot).start()

        for c0 in range(0, k, REALIGN_ROWS):
            wb_ref[c0:c0 + REALIGN_ROWS, :] = wf_ref[slot, :, c0:c0 + REALIGN_ROWS].T.astype(BF16)

    for rows in _row_splits(o_ref.shape[0]):
        acc = jnp.dot(a_ref[rows, :], wb_ref[...], preferred_element_type=F32)
        if rot_half:
            acc = _rotate(acc, cos_ref, sa_ref, sb_ref, rot_half, rows)
        o_ref[rows, :] = acc.astype(o_ref.dtype)


def project_t(a, wt, out_dtype, tm, tn, *, row_lo=0, n=None, rot=None, seq=None):
    m, k = a.shape
    n = wt.shape[1] - row_lo if n is None else n
    assert m % tm == 0 and n % tn == 0 and row_lo % SUBLANES == 0 and wt.shape[2] == k and k % REALIGN_ROWS == 0
    assert row_lo + n <= wt.shape[1]
    in_specs = [pl.BlockSpec((tm, k), lambda j, i: (i, 0)), pl.BlockSpec(memory_space=pl.ANY)]
    args = [a, wt]
    rot_half = 0
    if rot is not None:
        cos, sa, sb, rot_half = rot
        nrow = seq // tm
        tab = pl.BlockSpec((tm, cos.shape[1]), lambda j, i: (i % nrow, 0))
        in_specs += [tab, tab, tab]
        args += [cos, sa, sb]
    return pl.pallas_call(
        functools.partial(_proj_t_body, rot_half=rot_half, row_lo=row_lo),
        grid=(n // tn, m // tm),
        in_specs=in_specs,
        out_specs=pl.BlockSpec((tm, tn), lambda j, i: (i, j)),
        out_shape=jax.ShapeDtypeStruct((m, n), out_dtype),
        scratch_shapes=[pltpu.VMEM((2, tn, k), F32), pltpu.SemaphoreType.DMA((2,)), pltpu.VMEM((k, tn), BF16)],
        compiler_params=_cparams(("arbitrary", "arbitrary")),
        name="project_t",
    )(*args)


def _gate_up(a_ref, wg_ref, wu_ref, wb_ref, o_ref, first):
    tn = wg_ref.shape[1]

    @pl.when(first)
    def _():
        wb_ref[:, :tn] = wg_ref[...].astype(BF16)
        wb_ref[:, tn:] = wu_ref[...].astype(BF16)

    for rows in _row_splits(o_ref.shape[0]):
        gu = jnp.dot(a_ref[rows, :], wb_ref[...], preferred_element_type=F32)
        o_ref[rows, :] = (_silu(gu[:, :tn]) * gu[:, tn:]).astype(o_ref.dtype)


def _swiglu_body(a_ref, wg_ref, wu_ref, o_ref, wb_ref):
    _gate_up(a_ref, wg_ref, wu_ref, wb_ref, o_ref, pl.program_id(1) == 0)


def swiglu_up(a, wg, wu, tm, tn):
    m, k = a.shape
    n = wg.shape[1]
    assert m % tm == 0 and n % tn == 0
    wspec = pl.BlockSpec((k, tn), lambda j, i: (0, j))
    return pl.pallas_call(
        _swiglu_body,
        grid=(n // tn, m // tm),
        in_specs=[pl.BlockSpec((tm, k), lambda j, i: (i, 0)), wspec, wspec],
        out_specs=pl.BlockSpec((tm, tn), lambda j, i: (i, j)),
        out_shape=jax.ShapeDtypeStruct((m, n), BF16),
        scratch_shapes=[pltpu.VMEM((k, 2 * tn), BF16)],
        compiler_params=_cparams(("parallel", "arbitrary")),
        name="swiglu_up",
    )(a, wg, wu)


def _mm_body(a_ref, w_ref, o_ref):
    for rows in _row_splits(o_ref.shape[0]):
        o_ref[rows, :] = jnp.dot(a_ref[rows, :], w_ref[...], preferred_element_type=F32).astype(o_ref.dtype)


def matmul(a, w, out_dtype, tm, tn):
    m, k = a.shape
    n = w.shape[1]
    assert m % tm == 0 and n % tn == 0, (m, n, tm, tn)
    return pl.pallas_call(
        _mm_body,
        grid=(m // tm, n // tn),
        in_specs=[pl.BlockSpec((tm, k), lambda i, j: (i, 0)),
                  pl.BlockSpec((k, tn), lambda i, j: (0, j))],
        out_specs=pl.BlockSpec((tm, tn), lambda i, j: (i, j)),
        out_shape=jax.ShapeDtypeStruct((m, n), out_dtype),
        compiler_params=_cparams(("parallel", "arbitrary")),
        name="matmul",
    )(a, w)


def _ln_body(h_ref, y_ref, g_ref, b_ref, o_ref, ob_ref):
    x = DEEPNORM_ALPHA * h_ref[...] + y_ref[...]
    mu = jnp.mean(x, axis=-1, keepdims=True)
    xc = x - mu
    var = jnp.mean(xc * xc, axis=-1, keepdims=True)
    out = xc * lax.rsqrt(var + LN_EPS) * g_ref[...] + b_ref[...]
    o_ref[...] = out
    ob_ref[...] = out.astype(BF16)


def deepnorm_ln(h, y, g, b, tm=256):
    m, d = h.shape
    row = pl.BlockSpec((tm, d), lambda i: (i, 0))
    vec = pl.BlockSpec((1, d), lambda i: (0, 0))
    return pl.pallas_call(
        _ln_body,
        grid=(m // tm,),
        in_specs=[row, row, vec, vec],
        out_specs=[row, row],
        out_shape=[jax.ShapeDtypeStruct((m, d), F32), jax.ShapeDtypeStruct((m, d), BF16)],
        compiler_params=_cparams(("parallel",)),
        name="deepnorm_ln",
    )(h, y, g.reshape(1, d), b.reshape(1, d))


def _row_copy(src_ref, src_row, dst_ref, dst_row, sem):
    return pltpu.make_async_copy(src_ref.at[pl.ds(src_row, 1), :], dst_ref.at[pl.ds(dst_row, 1), :], sem)


def _start_rows(src_ref, idx_ref, base, n, dst_ref, dst_base, sem):
    def body(r, carry):
        _row_copy(src_ref, idx_ref[base + r], dst_ref, dst_base + r, sem).start()
        return carry

    lax.fori_loop(0, n, body, 0, unroll=8)


def _wait_rows(src_ref, n, dst_ref, dst_base, sem):
    def body(r, carry):
        _row_copy(src_ref, 0, dst_ref, dst_base + r, sem).wait()
        return carry

    lax.fori_loop(0, n, body, 0, unroll=8)


def _gather_body(tok_ref, src_ref, o_ref, buf_ref, sem_ref):
    rows = o_ref.shape[0]
    i = pl.program_id(0)
    slot = i % 2

    @pl.when(i == 0)
    def _():
        _start_rows(src_ref, tok_ref, 0, rows, buf_ref.at[0], 0, sem_ref.at[0])

    @pl.when(i + 1 < pl.num_programs(0))
    def _():
        _start_rows(src_ref, tok_ref, (i + 1) * rows, rows, buf_ref.at[1 - slot], 0, sem_ref.at[1 - slot])

    _wait_rows(src_ref, rows, buf_ref.at[slot], 0, sem_ref.at[slot])
    o_ref[...] = buf_ref[slot].astype(o_ref.dtype)


def gather_rows(src, row_idx, out_dtype, rows=256):
    p = row_idx.shape[0]
    d = src.shape[1]
    assert p % rows == 0
    spec = pltpu.PrefetchScalarGridSpec(
        num_scalar_prefetch=1,
        grid=(p // rows,),
        in_specs=[pl.BlockSpec(memory_space=pl.ANY)],
        out_specs=pl.BlockSpec((rows, d), lambda i, tok: (i, 0)),
        scratch_shapes=[pltpu.VMEM((2, rows, d), src.dtype), pltpu.SemaphoreType.DMA((2,))],
    )
    return pl.pallas_call(
        _gather_body, grid_spec=spec,
        out_shape=jax.ShapeDtypeStruct((p, d), out_dtype),
        compiler_params=_cparams(("arbitrary",)),
        name="gather_rows",
    )(row_idx, src)


def _combine_ln_body(p0_ref, p1_ref, h_ref, y_ref, g_ref, b_ref, o_ref, buf_ref, sem_ref):
    rows = h_ref.shape[0]
    i = pl.program_id(0)
    slot = i % 2

    def start(step, s):
        _start_rows(y_ref, p0_ref, step * rows, rows, buf_ref.at[s], 0, sem_ref.at[s])
        _start_rows(y_ref, p1_ref, step * rows, rows, buf_ref.at[s], rows, sem_ref.at[s])

    @pl.when(i == 0)
    def _():
        start(0, 0)

    @pl.when(i + 1 < pl.num_programs(0))
    def _():
        start(i + 1, 1 - slot)

    _wait_rows(y_ref, 2 * rows, buf_ref.at[slot], 0, sem_ref.at[slot])
    x = DEEPNORM_ALPHA * h_ref[...] + (buf_ref[slot, 0:rows, :] + buf_ref[slot, rows:2 * rows, :])
    mu = jnp.mean(x, axis=-1, keepdims=True)
    xc = x - mu
    var = jnp.mean(xc * xc, axis=-1, keepdims=True)
    o_ref[...] = xc * lax.rsqrt(var + LN_EPS) * g_ref[...] + b_ref[...]


def combine_deepnorm_ln(h, y, pos0, pos1, g, b, rows=128):
    m, d = h.shape
    assert m % rows == 0
    row = pl.BlockSpec((rows, d), lambda i, p0, p1: (i, 0))
    vec = pl.BlockSpec((1, d), lambda i, p0, p1: (0, 0))
    spec = pltpu.PrefetchScalarGridSpec(
        num_scalar_prefetch=2,
        grid=(m // rows,),
        in_specs=[row, pl.BlockSpec(memory_space=pl.ANY), vec, vec],
        out_specs=row,
        scratch_shapes=[pltpu.VMEM((2, 2 * rows, d), y.dtype), pltpu.SemaphoreType.DMA((2,))],
    )
    return pl.pallas_call(
        _combine_ln_body, grid_spec=spec,
        out_shape=jax.ShapeDtypeStruct((m, d), F32),
        compiler_params=_cparams(("arbitrary",)),
        name="combine_deepnorm_ln",
    )(pos0, pos1, h, y, g.reshape(1, d), b.reshape(1, d))


def _conv_silu(raw_ref, stage_ref, w_ref, b_ref, first):
    blk = raw_ref.shape[0]
    kk = w_ref.shape[0]

    @pl.when(first)
    def _():
        stage_ref[0:SUBLANES, :] = jnp.zeros((SUBLANES, stage_ref.shape[1]), F32)

    stage_ref[SUBLANES:SUBLANES + blk, :] = raw_ref[...]
    y = b_ref[...]
    for i in range(kk):
        y = y + w_ref[i:i + 1, :] * stage_ref[pl.ds(SUBLANES - (kk - 1) + i, blk), :]
    stage_ref[0:SUBLANES, :] = stage_ref[blk:blk + SUBLANES, :]
    return _silu(y)


def _ssd_body(z_ref, xr_ref, br_ref, cr_ref, dtr_ref, cwx_ref, cwb_ref, cwc_ref, cbx_ref, cbb_ref, cbc_ref,
              dtb_ref, aneg_ref, dskip_ref, nw_ref, o_ref,
              state_ref, y_ref, sx_ref, sb_ref, sc_ref, *, heads, head_dim, groups, n_state):
    blk = xr_ref.shape[0]
    hpg = heads // groups
    gw = hpg * head_dim
    inner = heads * head_dim
    first = pl.program_id(1) == 0

    @pl.when(first)
    def _():
        state_ref[...] = jnp.zeros_like(state_ref)

    xs = _conv_silu(xr_ref, sx_ref, cwx_ref, cbx_ref, first)
    bmat = _conv_silu(br_ref, sb_ref, cwb_ref, cbb_ref, first).astype(BF16)
    cmat = _conv_silu(cr_ref, sc_ref, cwc_ref, cbc_ref, first).astype(BF16)

    row = lax.broadcasted_iota(jnp.int32, (blk, blk), 0)
    col = lax.broadcasted_iota(jnp.int32, (blk, blk), 1)
    causal = row >= col
    eh = lax.broadcasted_iota(jnp.int32, (heads, inner), 0)
    ec = lax.broadcasted_iota(jnp.int32, (heads, inner), 1)
    expand = (ec // head_dim == eh).astype(F32)

    dtp = dtr_ref[:, :heads] + dtb_ref[...]
    dt = jnp.maximum(dtp, 0.0) + jnp.log(1.0 + jnp.exp(-jnp.abs(dtp)))
    a = dt * aneg_ref[...]
    cs = jnp.dot(causal.astype(F32), a, preferred_element_type=F32, precision=HIGHEST)
    cs_t = cs.T
    cs_x = jnp.dot(cs, expand, preferred_element_type=F32, precision=HIGHEST)
    dt_x = jnp.dot(dt, expand, preferred_element_type=F32, precision=HIGHEST)
    xdt = xs * dt_x
    last_x = cs_x[blk - 1:blk, :]
    x_in = (xdt * jnp.exp(last_x - cs_x)).astype(BF16)
    x_b = xdt.astype(BF16)
    grow = jnp.exp(cs_x)

    for g in range(groups):
        bg = bmat[:, g * n_state:(g + 1) * n_state]
        cg = cmat[:, g * n_state:(g + 1) * n_state]
        cb = lax.dot_general(cg, bg, (((1,), (1,)), ((), ())), preferred_element_type=F32)
        st = state_ref[g]
        y_off = jnp.dot(cg, st.astype(BF16), preferred_element_type=F32)
        y_ref[:, g * gw:(g + 1) * gw] = y_off * grow[:, g * gw:(g + 1) * gw]
        for r in range(hpg):
            h = g * hpg + r
            seg = cs[:, h:h + 1] - cs_t[h:h + 1, :]
            dec = jnp.exp(jnp.where(causal, seg, -jnp.inf))
            mh = (cb * dec).astype(BF16)
            yh = jnp.dot(mh, x_b[:, h * head_dim:(h + 1) * head_dim], preferred_element_type=F32)
            y_ref[:, h * head_dim:(h + 1) * head_dim] += yh
        new = lax.dot_general(bg, x_in[:, g * gw:(g + 1) * gw], (((0,), (0,)), ((), ())),
                              preferred_element_type=F32)
        state_ref[g] = st * jnp.exp(last_x[:, g * gw:(g + 1) * gw]) + new

    y = y_ref[...] + dskip_ref[...] * xs
    y = y * _silu(z_ref[...])
    for g in range(groups):
        yg = y[:, g * gw:(g + 1) * gw]
        ms = jnp.mean(yg * yg, axis=-1, keepdims=True)
        o_ref[:, g * gw:(g + 1) * gw] = (yg * lax.rsqrt(ms + RMS_EPS)
                                         * nw_ref[:, g * gw:(g + 1) * gw]).astype(o_ref.dtype)


def ssd_mixer(zx, dt_raw, conv_w, conv_b, dt_bias, a_neg, d_skip_x, norm_w, *, batch, heads=SSD_HEADS,
              head_dim=SSD_HEAD_DIM, groups=SSD_GROUPS, n_state=SSD_STATE, blk=SSD_BLOCK):
    m = zx.shape[0]
    inner = heads * head_dim
    bc = groups * n_state
    seq = m // batch
    nblk = seq // blk
    gw = inner // groups
    kk = conv_w.shape[0]
    rows = lambda w, cb: pl.BlockSpec((blk, w), lambda b, c: (b * nblk + c, cb))
    full = lambda r, w: pl.BlockSpec((r, w), lambda b, c: (0, 0))
    body = functools.partial(_ssd_body, heads=heads, head_dim=head_dim, groups=groups, n_state=n_state)
    cw = (conv_w[:, :inner], conv_w[:, inner:inner + bc], conv_w[:, inner + bc:])
    cb = tuple(t.reshape(1, -1) for t in (conv_b[:inner], conv_b[inner:inner + bc], conv_b[inner + bc:]))
    return pl.pallas_call(
        body,
        grid=(batch, nblk),
        in_specs=[rows(inner, 0), rows(inner, 1), rows(bc, 2 * inner // bc), rows(bc, 2 * inner // bc + 1),
                  rows(dt_raw.shape[1], 0),
                  full(kk, inner), full(kk, bc), full(kk, bc), full(1, inner), full(1, bc), full(1, bc),
                  full(1, heads), full(1, heads), full(1, inner), full(1, inner)],
        out_specs=rows(inner, 0),
        out_shape=jax.ShapeDtypeStruct((m, inner), BF16),
        scratch_shapes=[pltpu.VMEM((groups, n_state, gw), F32), pltpu.VMEM((blk, inner), F32),
                        pltpu.VMEM((blk + SUBLANES, inner), F32), pltpu.VMEM((blk + SUBLANES, bc), F32),
                        pltpu.VMEM((blk + SUBLANES, bc), F32)],
        compiler_params=_cparams(("parallel", "arbitrary")),
        name="ssd_mixer",
    )(zx, zx, zx, zx, dt_raw, *cw, *cb, dt_bias.reshape(1, heads), a_neg.reshape(1, heads),
      d_skip_x.reshape(1, inner), norm_w.reshape(1, inner))


def _attn_body(lam_ref, q_ref, k_ref, v_ref, nw_ref, o_ref, acc_ref, *, post_scale, chunk, hd):
    t = q_ref.shape[0]
    hps = q_ref.shape[1] // hd
    half = hd // 2
    i = pl.program_id(2)
    sub = lax.broadcasted_iota(jnp.int32, (hd, t), 0)
    qs_t = []
    for hh in range(hps):
        qt = q_ref[:, hh * hd:(hh + 1) * hd].astype(F32).T
        qs_t.append(jnp.concatenate([jnp.where(sub < half, qt, 0.0), jnp.where(sub >= half, qt, 0.0)],
                                    axis=1).astype(BF16))
    acc_ref[...] = jnp.zeros_like(acc_ref)

    def scores(j, hh):
        start = pl.multiple_of(j * t, t)
        kb = k_ref[pl.ds(start, t), hh * hd:(hh + 1) * hd]
        return jnp.dot(kb, qs_t[hh], preferred_element_type=F32)

    def step(j, carry, last):
        start = pl.multiple_of(j * t, t)
        out = []
        for hh in range(hps):
            m, l, s = carry[hh]
            s_next = s if last else scores(j + 1, hh)
            vb = v_ref[pl.ds(start, t), hh * hd:(hh + 1) * hd]
            if last:
                kk = lax.broadcasted_iota(jnp.int32, (t, 2 * t), 0)
                qq = lax.broadcasted_iota(jnp.int32, (t, 2 * t), 1)
                qq = jnp.where(qq >= t, qq - t, qq)
                s = jnp.where(kk // chunk <= qq // chunk, s, -jnp.inf)
            m_new = jnp.maximum(m, jnp.max(s, axis=0, keepdims=True))
            alpha = jnp.exp2(m - m_new)
            p = jnp.exp2(s - m_new)
            l = alpha * l + jnp.sum(p, axis=0, keepdims=True)
            pv = lax.dot_general(vb, p.astype(BF16), (((0,), (0,)), ((), ())),
                                 preferred_element_type=F32)
            acc_ref[hh] = alpha * acc_ref[hh] + pv
            out.append((m_new, l, s_next))
        return tuple(out)

    init = tuple((jnp.full((1, 2 * t), -jnp.inf, F32), jnp.zeros((1, 2 * t), F32), scores(0, hh))
                 for hh in range(hps))
    carry = lax.fori_loop(0, i, lambda j, c: step(j, c, False), init)
    carry = step(i, carry, True)
    for hh in range(hps):
        o = acc_ref[hh] * (1.0 / carry[hh][1])
        out = o[:, :t] - lam_ref[0] * o[:, t:]
        ms = jnp.mean(out * out, axis=0, keepdims=True)
        out = out * lax.rsqrt(ms + RMS_EPS) * (nw_ref[...] * post_scale)
        o_ref[:, hh * hd:(hh + 1) * hd] = out.T.astype(o_ref.dtype)


def diff_attention(q, k, v, lam, norm_w, post_scale, *, batch, heads=DIFF_HEADS, blk=ATTN_BLOCK,
                   chunk=CHUNK, hps=ATTN_HEADS_PER_STEP):
    m, width = q.shape
    hd = width // heads
    seq = m // batch
    nq = seq // blk
    q3, k3, v3 = (t.reshape(batch, seq, width) for t in (q, k, v))
    body = functools.partial(_attn_body, post_scale=post_scale, chunk=chunk, hd=hd)
    out = pl.pallas_call(
        body,
        grid=(batch, heads // hps, nq),
        in_specs=[pl.BlockSpec(memory_space=pltpu.SMEM),
                  pl.BlockSpec((None, blk, hps * hd), lambda b, h, i: (b, i, h)),
                  pl.BlockSpec((None, seq, hps * hd), lambda b, h, i: (b, 0, h)),
                  pl.BlockSpec((None, seq, hps * hd), lambda b, h, i: (b, 0, h)),
                  pl.BlockSpec((hd, 1), lambda b, h, i: (0, 0))],
        out_specs=pl.BlockSpec((None, blk, hps * hd), lambda b, h, i: (b, i, h)),
        out_shape=jax.ShapeDtypeStruct((batch, seq, width), BF16),
        scratch_shapes=[pltpu.VMEM((hps, hd, 2 * blk), F32)],
        compiler_params=_cparams(("parallel", "parallel", "arbitrary")),
        name="diff_attention",
    )(lam.reshape(1).astype(F32), q3, k3, v3, norm_w.reshape(hd, 1))
    return out.reshape(m, width)


def _ret_body(q_ref, k_ref, v_ref, g_ref, dmat_ref, qd_ref, kd_ref, bd_ref, o_ref, state_ref):
    hps = dmat_ref.shape[0]
    dk = q_ref.shape[1] // hps
    dv = v_ref.shape[1] // hps

    @pl.when(pl.program_id(2) == 0)
    def _():
        state_ref[...] = jnp.zeros_like(state_ref)

    for hh in range(hps):
        qb = q_ref[:, hh * dk:(hh + 1) * dk].astype(BF16)
        k = k_ref[:, hh * dk:(hh + 1) * dk]
        vb = v_ref[:, hh * dv:(hh + 1) * dv].astype(BF16)
        s = lax.dot_general(qb, k.astype(BF16), (((1,), (1,)), ((), ())), preferred_element_type=F32)
        y = jnp.dot((s * dmat_ref[hh]).astype(BF16), vb, preferred_element_type=F32)
        st = state_ref[hh]
        y = y + jnp.dot(qb, st.astype(BF16), preferred_element_type=F32) * qd_ref[hh]
        kdec = (k.astype(F32) * kd_ref[hh]).astype(BF16)
        state_ref[hh] = st * bd_ref[hh] + lax.dot_general(kdec, vb, (((0,), (0,)), ((), ())),
                                                          preferred_element_type=F32)
        mu = jnp.mean(y, axis=-1, keepdims=True)
        yc = y - mu
        var = jnp.mean(yc * yc, axis=-1, keepdims=True)
        gate = _silu(g_ref[:, hh * dv:(hh + 1) * dv].astype(F32))
        o_ref[:, hh * dv:(hh + 1) * dv] = (gate * (yc * lax.rsqrt(var + LN_EPS))).astype(o_ref.dtype)


def _retention_tables(heads, blk, chunk):
    log_g = jnp.log1p(-(2.0 ** (-5.0 - jnp.arange(heads, dtype=F32))))
    idx = jnp.arange(blk, dtype=F32)
    n, mm = idx[:, None], idx[None, :]
    cn, cm = jnp.floor(n / chunk), jnp.floor(mm / chunk)
    dist = jnp.where(cn == cm, jnp.abs(n - mm), n - mm)
    dmat = jnp.where((cm <= cn)[None], jnp.exp(dist[None] * log_g[:, None, None]), 0.0)
    qd = jnp.exp((idx + 1.0)[None, :, None] * log_g[:, None, None])
    kd = jnp.exp((blk - 1.0 - idx)[None, :, None] * log_g[:, None, None])
    bd = jnp.exp(blk * log_g)[:, None, None]
    return dmat, qd, kd, bd


def retention_mixer(q, k, vg, *, batch, heads=RET_HEADS, blk=RET_BLOCK, chunk=CHUNK, hps=RET_HEADS_PER_STEP):
    m = q.shape[0]
    dk = q.shape[1] // heads
    dv = vg.shape[1] // (2 * heads)
    seq = m // batch
    nblk = seq // blk
    groups = heads // hps
    dmat, qd, kd, bd = _retention_tables(heads, blk, chunk)
    rows = lambda w, off: pl.BlockSpec((blk, hps * w), lambda b, h, c: (b * nblk + c, h + off))
    table = lambda r, w: pl.BlockSpec((hps, r, w), lambda b, h, c: (h, 0, 0))
    return pl.pallas_call(
        _ret_body,
        grid=(batch, groups, nblk),
        in_specs=[rows(dk, 0), rows(dk, 0), rows(dv, 0), rows(dv, groups),
                  table(blk, blk), table(blk, 1), table(blk, 1), table(1, 1)],
        out_specs=rows(dv, 0),
        out_shape=jax.ShapeDtypeStruct((m, heads * dv), BF16),
        scratch_shapes=[pltpu.VMEM((hps, dk, dv), F32)],
        compiler_params=_cparams(("parallel", "parallel", "arbitrary")),
        name="retention_mixer",
    )(q, k, vg, vg, dmat, qd, kd, bd)


def _router_body(h_ref, w_ref, idx_ref, wt_ref, *, n_experts):
    logits = jnp.dot(h_ref[...], w_ref[...], preferred_element_type=F32, precision=HIGHEST)
    lane = lax.broadcasted_iota(jnp.int32, logits.shape, 1)
    logits = jnp.where(lane < n_experts, logits, -jnp.inf)
    v1 = jnp.max(logits, axis=-1, keepdims=True)
    i1 = jnp.min(jnp.where(logits == v1, lane, LANES), axis=-1, keepdims=True)
    rest = jnp.where(lane == i1, -jnp.inf, logits)
    v2 = jnp.max(rest, axis=-1, keepdims=True)
    i2 = jnp.min(jnp.where(rest == v2, lane, LANES), axis=-1, keepdims=True)
    e2 = jnp.exp(v2 - v1)
    w1 = 1.0 / (1.0 + e2)
    w2 = e2 / (1.0 + e2)
    idx_ref[...] = jnp.where(lane == 0, i1, jnp.where(lane == 1, i2, 0))
    wt_ref[...] = jnp.where(lane == 0, w1, jnp.where(lane == 1, w2, 0.0))


def moe_router(h, w_router, tm=512):
    m, d = h.shape
    n_experts = w_router.shape[1]
    w_pad = jnp.zeros((d, LANES), F32).at[:, :n_experts].set(w_router)
    row = pl.BlockSpec((tm, LANES), lambda i: (i, 0))
    idx, wt = pl.pallas_call(
        functools.partial(_router_body, n_experts=n_experts),
        grid=(m // tm,),
        in_specs=[pl.BlockSpec((tm, d), lambda i: (i, 0)), pl.BlockSpec((d, LANES), lambda i: (0, 0))],
        out_specs=[row, row],
        out_shape=[jax.ShapeDtypeStruct((m, LANES), jnp.int32), jax.ShapeDtypeStruct((m, LANES), F32)],
        compiler_params=_cparams(("parallel",)),
        name="moe_router",
    )(h, w_pad)
    return idx[:, :TOP_K], wt[:, :TOP_K]


def _expert_weights(gord_ref, gexp_ref, meta_ref, w_refs, wf_refs, sem_ref, wb_ref):
    j, t = pl.program_id(0), pl.program_id(1)
    tn = wf_refs[0].shape[2]
    n_groups = meta_ref[1]
    g = gord_ref[t]

    def copies(expert, col, slot):
        start = pl.multiple_of(col * tn, tn)
        return [pltpu.make_async_copy(w_ref.at[expert, :, pl.ds(start, tn)], wf_ref.at[slot], sem_ref.at[slot])
                for w_ref, wf_ref in zip(w_refs, wf_refs)]

    @pl.when(jnp.logical_or(t == 0, g != gord_ref[jnp.maximum(t - 1, 0)]))
    def _():
        seq_no = j * n_groups + g
        slot = seq_no % 2

        @pl.when(seq_no == 0)
        def _():
            for c in copies(gexp_ref[0], 0, 0):
                c.start()

        for c in copies(0, 0, slot):
            c.wait()
        more = g + 1 < n_groups

        @pl.when(jnp.logical_or(more, j + 1 < pl.num_programs(0)))
        def _():
            for c in copies(gexp_ref[jnp.where(more, g + 1, 0)], jnp.where(more, j, j + 1), 1 - slot):
                c.start()

        for i, wf_ref in enumerate(wf_refs):
            wb_ref[:, i * tn:(i + 1) * tn] = wf_ref[slot].astype(BF16)


def _moe_up_body(gord_ref, gexp_ref, meta_ref, x_ref, wg_ref, wu_ref, o_ref, wgf_ref, wuf_ref, sem_ref, wb_ref):
    tn = o_ref.shape[1]

    @pl.when(pl.program_id(1) < meta_ref[0])
    def _():
        _expert_weights(gord_ref, gexp_ref, meta_ref, (wg_ref, wu_ref), (wgf_ref, wuf_ref), sem_ref, wb_ref)
        for rows in _row_splits(o_ref.shape[0]):
            gu = jnp.dot(x_ref[rows, :], wb_ref[...], preferred_element_type=F32)
            o_ref[rows, :] = (_silu(gu[:, :tn]) * gu[:, tn:]).astype(o_ref.dtype)

    @pl.when(pl.program_id(1) >= meta_ref[0])
    def _():
        o_ref[...] = jnp.zeros_like(o_ref)


def _moe_down_body(gord_ref, gexp_ref, meta_ref, a_ref, w_ref, rw_ref, o_ref, wf_ref, sem_ref, wb_ref):
    @pl.when(pl.program_id(1) < meta_ref[0])
    def _():
        _expert_weights(gord_ref, gexp_ref, meta_ref, (w_ref,), (wf_ref,), sem_ref, wb_ref)
        for rows in _row_splits(o_ref.shape[0]):
            y = jnp.dot(a_ref[rows, :], wb_ref[...], preferred_element_type=F32)
            o_ref[rows, :] = (y * rw_ref[rows, :]).astype(o_ref.dtype)

    @pl.when(pl.program_id(1) >= meta_ref[0])
    def _():
        o_ref[...] = jnp.zeros_like(o_ref)


def moe_experts(x_sorted, group_of_tile, group_expert, meta, row_w, w_gate, w_up, w_down, tm=MOE_TM, tn=512):
    p, d = x_sorted.shape
    f = w_gate.shape[2]
    nt = p // tm
    hbm = pl.BlockSpec(memory_space=pl.ANY)
    fetch = lambda k: pltpu.VMEM((2, k, tn), F32)
    up_spec = pltpu.PrefetchScalarGridSpec(
        num_scalar_prefetch=3,
        grid=(f // tn, nt),
        in_specs=[pl.BlockSpec((tm, d), lambda j, t, *_: (t, 0)), hbm, hbm],
        out_specs=pl.BlockSpec((tm, tn), lambda j, t, *_: (t, j)),
        scratch_shapes=[fetch(d), fetch(d), pltpu.SemaphoreType.DMA((2,)), pltpu.VMEM((d, 2 * tn), BF16)],
    )
    act = pl.pallas_call(
        _moe_up_body, grid_spec=up_spec,
        out_shape=jax.ShapeDtypeStruct((p, f), BF16),
        compiler_params=_cparams(("arbitrary", "arbitrary")),
        name="moe_up",
    )(group_of_tile, group_expert, meta, x_sorted, w_gate, w_up)
    down_spec = pltpu.PrefetchScalarGridSpec(
        num_scalar_prefetch=3,
        grid=(d // tn, nt),
        in_specs=[pl.BlockSpec((tm, f), lambda j, t, *_: (t, 0)), hbm,
                  pl.BlockSpec((tm, 1), lambda j, t, *_: (t, 0))],
        out_specs=pl.BlockSpec((tm, tn), lambda j, t, *_: (t, j)),
        scratch_shapes=[fetch(f), pltpu.SemaphoreType.DMA((2,)), pltpu.VMEM((f, tn), BF16)],
    )
    return pl.pallas_call(
        _moe_down_body, grid_spec=down_spec,
        out_shape=jax.ShapeDtypeStruct((p, d), F32),
        compiler_params=_cparams(("arbitrary", "arbitrary")),
        name="moe_down",
    )(group_of_tile, group_expert, meta, act, w_down, row_w)


def moe_routing(top_i, top_w, n_experts, tm):
    m = top_i.shape[0]
    pairs = m * TOP_K
    p = pairs + n_experts * tm
    flat_e = top_i.reshape(pairs)
    onehot = (flat_e[:, None] == jnp.arange(n_experts, dtype=jnp.int32)[None, :]).astype(jnp.int32)
    rank = jnp.sum((jnp.cumsum(onehot, axis=0) - onehot) * onehot, axis=1)
    counts = jnp.sum(onehot, axis=0)
    padded = ((counts + tm - 1) // tm) * tm
    ends = jnp.cumsum(padded)
    starts = ends - padded
    dest = starts[flat_e] + rank
    row_token = jnp.zeros((p,), jnp.int32).at[dest].set(jnp.arange(pairs, dtype=jnp.int32) // TOP_K)
    row_w = jnp.zeros((p,), F32).at[dest].set(top_w.reshape(pairs)).reshape(p, 1)
    tile_start = jnp.arange(p // tm, dtype=jnp.int32) * tm
    tile_expert = jnp.minimum(jnp.sum((tile_start[:, None] >= ends[None, :]).astype(jnp.int32), axis=1),
                              n_experts - 1).astype(jnp.int32)
    nonempty = (counts > 0).astype(jnp.int32)
    group_of_tile = (jnp.cumsum(nonempty) - 1)[tile_expert].astype(jnp.int32)
    group_expert = jnp.argsort(1 - nonempty, stable=True).astype(jnp.int32)
    meta = jnp.stack([ends[-1] // tm, jnp.sum(nonempty)]).astype(jnp.int32)
    return row_token, row_w, dest.reshape(m, TOP_K), group_of_tile, group_expert, meta


def _rotary_tables(inv_freq, seq, blocks, scale):
    ang = jnp.arange(seq, dtype=F32)[:, None] * inv_freq[None, :]
    cos, sin, zero = jnp.cos(ang) * scale, jnp.sin(ang) * scale, jnp.zeros_like(ang)
    rep = lambda a, b: jnp.tile(jnp.concatenate([a, b], axis=1), (1, blocks))
    return rep(cos, cos), rep(-sin, zero), rep(zero, sin), inv_freq.shape[0]


def _layer_even(hb, batch, seq, w_in, conv_w, conv_b, dt_bias, a_log, d_skip, ssd_norm_w,
                lam_q1, lam_k1, lam_q2, lam_k2, diff_norm_w, w_out, lambda_init):
    zx_cols = SSD_INNER + SSD_XBC
    qkv_lo = zx_cols + SSD_HEADS
    wt = jnp.swapaxes(w_in, 1, 2)
    zx = project_t(hb, wt, F32, 1024, 512, n=zx_cols)
    dt_raw = project_t(hb, wt, F32, 1024, LANES, row_lo=zx_cols, n=LANES)
    inv_freq = ROPE_THETA ** (-jnp.arange(0, DIFF_QK_DIM, 2, dtype=F32) / DIFF_QK_DIM)
    rot_q = _rotary_tables(inv_freq, seq, 2, DIFF_QK_DIM ** -0.5 * math.log2(math.e))
    rot_k = _rotary_tables(inv_freq, seq, 2, 1.0)
    q = project_t(hb, wt, BF16, 1024, 512, row_lo=qkv_lo, n=DIFF_WIDTH, rot=rot_q, seq=seq)
    k = project_t(hb, wt, BF16, 1024, 512, row_lo=qkv_lo + DIFF_WIDTH, n=DIFF_WIDTH, rot=rot_k, seq=seq)
    v = project_t(hb, wt, BF16, 1024, 512, row_lo=qkv_lo + 2 * DIFF_WIDTH, n=DIFF_WIDTH)
    y = ssd_mixer(zx, dt_raw, conv_w, conv_b, dt_bias, -jnp.exp(a_log), jnp.repeat(d_skip, SSD_HEAD_DIM),
                  ssd_norm_w, batch=batch)
    lam = (jnp.exp(jnp.sum(lam_q1 * lam_k1)) - jnp.exp(jnp.sum(lam_q2 * lam_k2)) + lambda_init)
    o = diff_attention(q, k, v, lam, diff_norm_w, 1.0 - lambda_init, batch=batch)
    return project([y, o], w_out.astype(BF16), F32, 512, 512)


def _layer_odd_mixer(hb, batch, seq, w_in, w_out):
    inv_freq = 1.0 / (ROPE_THETA ** jnp.linspace(0.0, 1.0, RET_QK_DIM // 2, dtype=F32))
    rot_q = _rotary_tables(inv_freq, seq, 1, 1.0)
    rot_k = _rotary_tables(inv_freq, seq, 1, RET_QK_DIM ** -0.5)
    q = project([hb], w_in, BF16, 1024, 512, n=RET_QK_WIDTH, rot=rot_q, seq=seq)
    k = project([hb], w_in, BF16, 1024, 512, col_lo=RET_QK_WIDTH, n=RET_QK_WIDTH, rot=rot_k, seq=seq)
    vg = project([hb], w_in, BF16, 1024, 512, col_lo=2 * RET_QK_WIDTH, n=2 * RET_V_WIDTH)
    o = retention_mixer(q, k, vg, batch=batch)
    return project([o], w_out.astype(BF16), F32, 512, 512)


def _moe_ln(h, w_router, w_gate, w_up, w_down, ln_g, ln_b):
    top_i, top_w = moe_router(h, w_router)
    row_token, row_w, pos, group_of_tile, group_expert, meta = moe_routing(top_i, top_w, N_EXPERTS, MOE_TM)
    x_sorted = gather_rows(h, row_token, BF16)
    y = moe_experts(x_sorted, group_of_tile, group_expert, meta, row_w, w_gate, w_up, w_down)
    return combine_deepnorm_ln(h, y, pos[:, 0], pos[:, 1], ln_g, ln_b)


def kernel(x, ab_w_in, ab_conv_w, ab_conv_b, ab_dt_bias, ab_a_log, ab_d_skip, ab_ssd_norm_w, ab_lam_q1, ab_lam_k1, ab_lam_q2, ab_lam_k2, ab_diff_norm_w, ab_w_out, ffn_w_gate, ffn_w_up, ffn_w_down, c_w_in, c_w_out, moe_w_router, moe_w_gate, moe_w_up, moe_w_down, ln_mix_g, ln_mix_b, ln_ffn_g, ln_ffn_b):
    batch, seq, d = x.shape
    m = batch * seq
    h = x.reshape(m, d)
    hb = h.astype(BF16)
    lambda_init = 0.8 - 0.6 * math.exp(-0.3 * 0)
    mix = _layer_even(hb, batch, seq, ab_w_in, ab_conv_w[0], ab_conv_b[0], ab_dt_bias[0], ab_a_log[0],
                      ab_d_skip[0], ab_ssd_norm_w[0], ab_lam_q1[0], ab_lam_k1[0], ab_lam_q2[0], ab_lam_k2[0],
                      ab_diff_norm_w[0], ab_w_out[0], lambda_init)
    h, hb = deepnorm_ln(h, mix, ln_mix_g[0], ln_mix_b[0])
    act = swiglu_up(hb, ffn_w_gate[0], ffn_w_up[0], 1024, 256)
    ff = matmul(act, ffn_w_down[0].astype(BF16), F32, 512, 256)
    h, hb = deepnorm_ln(h, ff, ln_ffn_g[0], ln_ffn_b[0])
    mix = _layer_odd_mixer(hb, batch, seq, c_w_in, c_w_out[0])
    h, _ = deepnorm_ln(h, mix, ln_mix_g[1], ln_mix_b[1])
    h = _moe_ln(h, moe_w_router[0], moe_w_gate[0], moe_w_up[0], moe_w_down[0], ln_ffn_g[1], ln_ffn_b[1])
    return h.reshape(batch, seq, d)
```

```python
import functools
import math

import jax
import jax.numpy as jnp
from jax import lax
from jax.experimental import pallas as pl
from jax.experimental.pallas import tpu as pltpu

F32 = jnp.float32
BF16 = jnp.bfloat16
HIGHEST = lax.Precision.HIGHEST

D_MODEL = 4096
DEPTH = 2
CHUNK = 64
SSD_HEADS = 64
SSD_HEAD_DIM = 64
SSD_INNER = SSD_HEADS * SSD_HEAD_DIM
SSD_GROUPS = 8
SSD_STATE = 128
SSD_CONV = 4
SSD_BC = SSD_GROUPS * SSD_STATE
SSD_XBC = SSD_INNER + 2 * SSD_BC
DIFF_HEADS = 32
DIFF_QK_DIM = 64
DIFF_V_DIM = 2 * DIFF_QK_DIM
DIFF_WIDTH = DIFF_HEADS * DIFF_V_DIM
ROPE_THETA = 10000.0
RET_HEADS = 16
RET_QK_DIM = D_MODEL // RET_HEADS
RET_V_DIM = 2 * RET_QK_DIM
RET_QK_WIDTH = RET_HEADS * RET_QK_DIM
RET_V_WIDTH = RET_HEADS * RET_V_DIM
FFN_DIM = 11008
N_EXPERTS = 8
TOP_K = 2
EXPERT_DIM = 4096
DEEPNORM_ALPHA = (2 * DEPTH) ** 0.25
LN_EPS = 1e-5
RMS_EPS = 1e-5

LANES = 128
SUBLANES = 8
VMEM_LIMIT_BYTES = 56 * 1024 * 1024

SSD_BLOCK = 128
RET_BLOCK = 256
RET_HEADS_PER_STEP = 2
ATTN_BLOCK = 256
ATTN_HEADS_PER_STEP = 4
MOE_TM = 512


def _cparams(sem):
    return pltpu.CompilerParams(dimension_semantics=sem, vmem_limit_bytes=VMEM_LIMIT_BYTES)


def _silu(x):
    return x * (1.0 / (1.0 + jnp.exp(-x)))


SPLIT_ROWS = 512


def _row_splits(tm):
    return (slice(0, tm // 2), slice(tm // 2, tm)) if tm >= SPLIT_ROWS else (slice(0, tm),)


def _rotate(acc, cos_ref, sa_ref, sb_ref, half, rows):
    tn = acc.shape[1]
    reps = tn // cos_ref.shape[1]
    tile = lambda r: jnp.tile(r[rows, :], (1, reps))
    return (acc * tile(cos_ref) + pltpu.roll(acc, tn - half, 1) * tile(sa_ref)
            + pltpu.roll(acc, half, 1) * tile(sb_ref))


REALIGN_ROWS = 512


def _proj_body(*refs):
    *a_refs, w_ref, o_ref = refs
    for rows in _row_splits(o_ref.shape[0]):
        k0 = 0
        acc = None
        for a_ref in a_refs:
            ka = a_ref.shape[1]
            part = jnp.dot(a_ref[rows, :], w_ref[k0:k0 + ka, :], preferred_element_type=F32)
            acc = part if acc is None else acc + part
            k0 += ka
        o_ref[rows, :] = acc.astype(o_ref.dtype)


def project(a_list, w, out_dtype, tm, tn):
    m = a_list[0].shape[0]
    k, n = w.shape
    assert m % tm == 0 and n % tn == 0 and sum(a.shape[1] for a in a_list) == k and w.dtype == BF16
    in_specs = [pl.BlockSpec((tm, a.shape[1]), lambda j, i: (i, 0)) for a in a_list]
    in_specs.append(pl.BlockSpec((k, tn), lambda j, i: (0, j)))
    return pl.pallas_call(
        _proj_body,
        grid=(n // tn, m // tm),
        in_specs=in_specs,
        out_specs=pl.BlockSpec((tm, tn), lambda j, i: (i, j)),
        out_shape=jax.ShapeDtypeStruct((m, n), out_dtype),
        compiler_params=_cparams(("parallel", "arbitrary")),
        name="project",
    )(*a_list, w)


def _column_tile_weights(copies, consume):
    j = pl.program_id(0)

    @pl.when(pl.program_id(1) == 0)
    def _():
        slot = j % 2

        @pl.when(j == 0)
        def _():
            for c in copies(0, 0):
                c.start()

        for c in copies(0, slot):
            c.wait()

        @pl.when(j + 1 < pl.num_programs(0))
        def _():
            for c in copies(j + 1, 1 - slot):
                c.start()

        consume(slot)


def _proj_t_body(*refs, rot_half, lo, transposed):
    a_ref, w_ref = refs[:2]
    pos = 2
    if rot_half:
        cos_ref, sa_ref, sb_ref = refs[pos:pos + 3]
        pos += 3
    o_ref, wf_ref, sem_ref, wb_ref = refs[pos:pos + 4]
    k, tn = wb_ref.shape
    lead = (0,) * (len(w_ref.shape) - 2)

    def copies(col_tile, slot):
        if transposed:
            src = w_ref.at[lead + (pl.ds(pl.multiple_of(lo + col_tile * tn, SUBLANES), tn), slice(None))]
        else:
            src = w_ref.at[lead + (slice(None), pl.ds(pl.multiple_of(lo + col_tile * tn, LANES), tn))]
        return [pltpu.make_async_copy(src, wf_ref.at[slot], sem_ref.at[slot])]

    def consume(slot):
        if transposed:
            for c0 in range(0, k, REALIGN_ROWS):
                wb_ref[c0:c0 + REALIGN_ROWS, :] = wf_ref[slot, :, c0:c0 + REALIGN_ROWS].T.astype(BF16)
        else:
            wb_ref[...] = wf_ref[slot].astype(BF16)

    _column_tile_weights(copies, consume)
    for rows in _row_splits(o_ref.shape[0]):
        acc = jnp.dot(a_ref[rows, :], wb_ref[...], preferred_element_type=F32)
        if rot_half:
            acc = _rotate(acc, cos_ref, sa_ref, sb_ref, rot_half, rows)
        o_ref[rows, :] = acc.astype(o_ref.dtype)


def project_t(a, w, out_dtype, tm, tn, *, row_lo=0, n=None, rot=None, seq=None, transposed=True):
    m, k = a.shape
    feat_axis, k_axis = (1, 2) if transposed else (2, 1)
    n = w.shape[feat_axis] - row_lo if n is None else n
    assert m % tm == 0 and n % tn == 0 and w.shape[k_axis] == k and k % REALIGN_ROWS == 0
    assert row_lo % (SUBLANES if transposed else LANES) == 0 and row_lo + n <= w.shape[feat_axis]
    in_specs = [pl.BlockSpec((tm, k), lambda j, i: (i, 0)), pl.BlockSpec(memory_space=pl.ANY)]
    args = [a, w]
    rot_half = 0
    if rot is not None:
        cos, sa, sb, rot_half = rot
        nrow = seq // tm
        tab = pl.BlockSpec((tm, cos.shape[1]), lambda j, i: (i % nrow, 0))
        in_specs += [tab, tab, tab]
        args += [cos, sa, sb]
    return pl.pallas_call(
        functools.partial(_proj_t_body, rot_half=rot_half, lo=row_lo, transposed=transposed),
        grid=(n // tn, m // tm),
        in_specs=in_specs,
        out_specs=pl.BlockSpec((tm, tn), lambda j, i: (i, j)),
        out_shape=jax.ShapeDtypeStruct((m, n), out_dtype),
        scratch_shapes=[pltpu.VMEM((2, tn, k) if transposed else (2, k, tn), F32),
                        pltpu.SemaphoreType.DMA((2,)), pltpu.VMEM((k, tn), BF16)],
        compiler_params=_cparams(("arbitrary", "arbitrary")),
        name="project_t",
    )(*args)


def _swiglu_body(a_ref, wg_ref, wu_ref, o_ref, wgf_ref, wuf_ref, sem_ref, wb_ref):
    tn = o_ref.shape[1]

    def copies(col_tile, slot):
        cols = pl.ds(pl.multiple_of(col_tile * tn, LANES), tn)
        return [pltpu.make_async_copy(w_ref.at[0, :, cols], wf_ref.at[slot], sem_ref.at[slot])
                for w_ref, wf_ref in ((wg_ref, wgf_ref), (wu_ref, wuf_ref))]

    def consume(slot):
        wb_ref[:, :tn] = wgf_ref[slot].astype(BF16)
        wb_ref[:, tn:] = wuf_ref[slot].astype(BF16)

    _column_tile_weights(copies, consume)
    for rows in _row_splits(o_ref.shape[0]):
        gu = jnp.dot(a_ref[rows, :], wb_ref[...], preferred_element_type=F32)
        o_ref[rows, :] = (_silu(gu[:, :tn]) * gu[:, tn:]).astype(o_ref.dtype)


def swiglu_up(a, wg, wu, tm, tn):
    m, k = a.shape
    n = wg.shape[2]
    assert m % tm == 0 and n % tn == 0 and wg.shape == wu.shape == (1, k, n)
    hbm = pl.BlockSpec(memory_space=pl.ANY)
    return pl.pallas_call(
        _swiglu_body,
        grid=(n // tn, m // tm),
        in_specs=[pl.BlockSpec((tm, k), lambda j, i: (i, 0)), hbm, hbm],
        out_specs=pl.BlockSpec((tm, tn), lambda j, i: (i, j)),
        out_shape=jax.ShapeDtypeStruct((m, n), BF16),
        scratch_shapes=[pltpu.VMEM((2, k, tn), F32), pltpu.VMEM((2, k, tn), F32),
                        pltpu.SemaphoreType.DMA((2,)), pltpu.VMEM((k, 2 * tn), BF16)],
        compiler_params=_cparams(("arbitrary", "arbitrary")),
        name="swiglu_up",
    )(a, wg, wu)


def _mm_body(a_ref, w_ref, o_ref):
    for rows in _row_splits(o_ref.shape[0]):
        o_ref[rows, :] = jnp.dot(a_ref[rows, :], w_ref[...], preferred_element_type=F32).astype(o_ref.dtype)


def matmul(a, w, out_dtype, tm, tn):
    m, k = a.shape
    n = w.shape[1]
    assert m % tm == 0 and n % tn == 0, (m, n, tm, tn)
    return pl.pallas_call(
        _mm_body,
        grid=(m // tm, n // tn),
        in_specs=[pl.BlockSpec((tm, k), lambda i, j: (i, 0)),
                  pl.BlockSpec((k, tn), lambda i, j: (0, j))],
        out_specs=pl.BlockSpec((tm, tn), lambda i, j: (i, j)),
        out_shape=jax.ShapeDtypeStruct((m, n), out_dtype),
        compiler_params=_cparams(("parallel", "arbitrary")),
        name="matmul",
    )(a, w)


def _ln_body(h_ref, y_ref, g_ref, b_ref, o_ref, ob_ref):
    x = DEEPNORM_ALPHA * h_ref[...] + y_ref[...]
    mu = jnp.mean(x, axis=-1, keepdims=True)
    xc = x - mu
    var = jnp.mean(xc * xc, axis=-1, keepdims=True)
    out = xc * lax.rsqrt(var + LN_EPS) * g_ref[...] + b_ref[...]
    o_ref[...] = out
    ob_ref[...] = out.astype(BF16)


def deepnorm_ln(h, y, g, b, tm=256):
    m, d = h.shape
    row = pl.BlockSpec((tm, d), lambda i: (i, 0))
    vec = pl.BlockSpec((1, d), lambda i: (0, 0))
    return pl.pallas_call(
        _ln_body,
        grid=(m // tm,),
        in_specs=[row, row, vec, vec],
        out_specs=[row, row],
        out_shape=[jax.ShapeDtypeStruct((m, d), F32), jax.ShapeDtypeStruct((m, d), BF16)],
        compiler_params=_cparams(("parallel",)),
        name="deepnorm_ln",
    )(h, y, g.reshape(1, d), b.reshape(1, d))


def _row_copy(src_ref, src_row, dst_ref, dst_row, sem):
    return pltpu.make_async_copy(src_ref.at[pl.ds(src_row, 1), :], dst_ref.at[pl.ds(dst_row, 1), :], sem)


def _start_rows(src_ref, idx_ref, base, n, dst_ref, dst_base, sem):
    def body(r, carry):
        _row_copy(src_ref, idx_ref[base + r], dst_ref, dst_base + r, sem).start()
        return carry

    lax.fori_loop(0, n, body, 0, unroll=8)


def _wait_rows(src_ref, n, dst_ref, dst_base, sem):
    def body(r, carry):
        _row_copy(src_ref, 0, dst_ref, dst_base + r, sem).wait()
        return carry

    lax.fori_loop(0, n, body, 0, unroll=8)


def _gather_body(tok_ref, src_ref, o_ref, buf_ref, sem_ref):
    rows = o_ref.shape[0]
    i = pl.program_id(0)
    slot = i % 2

    @pl.when(i == 0)
    def _():
        _start_rows(src_ref, tok_ref, 0, rows, buf_ref.at[0], 0, sem_ref.at[0])

    @pl.when(i + 1 < pl.num_programs(0))
    def _():
        _start_rows(src_ref, tok_ref, (i + 1) * rows, rows, buf_ref.at[1 - slot], 0, sem_ref.at[1 - slot])

    _wait_rows(src_ref, rows, buf_ref.at[slot], 0, sem_ref.at[slot])
    o_ref[...] = buf_ref[slot].astype(o_ref.dtype)


def gather_rows(src, row_idx, out_dtype, rows=256):
    p = row_idx.shape[0]
    d = src.shape[1]
    assert p % rows == 0
    spec = pltpu.PrefetchScalarGridSpec(
        num_scalar_prefetch=1,
        grid=(p // rows,),
        in_specs=[pl.BlockSpec(memory_space=pl.ANY)],
        out_specs=pl.BlockSpec((rows, d), lambda i, tok: (i, 0)),
        scratch_shapes=[pltpu.VMEM((2, rows, d), src.dtype), pltpu.SemaphoreType.DMA((2,))],
    )
    return pl.pallas_call(
        _gather_body, grid_spec=spec,
        out_shape=jax.ShapeDtypeStruct((p, d), out_dtype),
        compiler_params=_cparams(("arbitrary",)),
        name="gather_rows",
    )(row_idx, src)


def _combine_ln_body(p0_ref, p1_ref, h_ref, y_ref, g_ref, b_ref, o_ref, buf_ref, sem_ref):
    rows = h_ref.shape[0]
    i = pl.program_id(0)
    slot = i % 2

    def start(step, s):
        _start_rows(y_ref, p0_ref, step * rows, rows, buf_ref.at[s], 0, sem_ref.at[s])
        _start_rows(y_ref, p1_ref, step * rows, rows, buf_ref.at[s], rows, sem_ref.at[s])

    @pl.when(i == 0)
    def _():
        start(0, 0)

    @pl.when(i + 1 < pl.num_programs(0))
    def _():
        start(i + 1, 1 - slot)

    _wait_rows(y_ref, 2 * rows, buf_ref.at[slot], 0, sem_ref.at[slot])
    x = DEEPNORM_ALPHA * h_ref[...] + (buf_ref[slot, 0:rows, :] + buf_ref[slot, rows:2 * rows, :])
    mu = jnp.mean(x, axis=-1, keepdims=True)
    xc = x - mu
    var = jnp.mean(xc * xc, axis=-1, keepdims=True)
    o_ref[...] = xc * lax.rsqrt(var + LN_EPS) * g_ref[...] + b_ref[...]


def combine_deepnorm_ln(h, y, pos0, pos1, g, b, rows=128):
    m, d = h.shape
    assert m % rows == 0
    row = pl.BlockSpec((rows, d), lambda i, p0, p1: (i, 0))
    vec = pl.BlockSpec((1, d), lambda i, p0, p1: (0, 0))
    spec = pltpu.PrefetchScalarGridSpec(
        num_scalar_prefetch=2,
        grid=(m // rows,),
        in_specs=[row, pl.BlockSpec(memory_space=pl.ANY), vec, vec],
        out_specs=row,
        scratch_shapes=[pltpu.VMEM((2, 2 * rows, d), y.dtype), pltpu.SemaphoreType.DMA((2,))],
    )
    return pl.pallas_call(
        _combine_ln_body, grid_spec=spec,
        out_shape=jax.ShapeDtypeStruct((m, d), F32),
        compiler_params=_cparams(("arbitrary",)),
        name="combine_deepnorm_ln",
    )(pos0, pos1, h, y, g.reshape(1, d), b.reshape(1, d))


def _conv_silu(raw_ref, stage_ref, w_ref, b_ref, first):
    blk = raw_ref.shape[0]
    kk = w_ref.shape[0]

    @pl.when(first)
    def _():
        stage_ref[0:SUBLANES, :] = jnp.zeros((SUBLANES, stage_ref.shape[1]), F32)

    stage_ref[SUBLANES:SUBLANES + blk, :] = raw_ref[...]
    y = b_ref[...]
    for i in range(kk):
        y = y + w_ref[i:i + 1, :] * stage_ref[pl.ds(SUBLANES - (kk - 1) + i, blk), :]
    stage_ref[0:SUBLANES, :] = stage_ref[blk:blk + SUBLANES, :]
    return _silu(y)


def _ssd_body(z_ref, xr_ref, br_ref, cr_ref, dtr_ref, cwx_ref, cwb_ref, cwc_ref, cbx_ref, cbb_ref, cbc_ref,
              dtb_ref, aneg_ref, dskip_ref, nw_ref, o_ref,
              state_ref, y_ref, sx_ref, sb_ref, sc_ref, *, heads, head_dim, groups, n_state):
    blk = xr_ref.shape[0]
    hpg = heads // groups
    gw = hpg * head_dim
    inner = heads * head_dim
    first = pl.program_id(1) == 0

    @pl.when(first)
    def _():
        state_ref[...] = jnp.zeros_like(state_ref)

    xs = _conv_silu(xr_ref, sx_ref, cwx_ref, cbx_ref, first)
    bmat = _conv_silu(br_ref, sb_ref, cwb_ref, cbb_ref, first).astype(BF16)
    cmat = _conv_silu(cr_ref, sc_ref, cwc_ref, cbc_ref, first).astype(BF16)

    row = lax.broadcasted_iota(jnp.int32, (blk, blk), 0)
    col = lax.broadcasted_iota(jnp.int32, (blk, blk), 1)
    causal = row >= col
    eh = lax.broadcasted_iota(jnp.int32, (heads, inner), 0)
    ec = lax.broadcasted_iota(jnp.int32, (heads, inner), 1)
    expand = (ec // head_dim == eh).astype(F32)

    dtp = dtr_ref[:, :heads] + dtb_ref[...]
    dt = jnp.maximum(dtp, 0.0) + jnp.log(1.0 + jnp.exp(-jnp.abs(dtp)))
    a = dt * aneg_ref[...]
    cs = jnp.dot(causal.astype(F32), a, preferred_element_type=F32, precision=HIGHEST)
    cs_t = cs.T
    cs_x = jnp.dot(cs, expand, preferred_element_type=F32, precision=HIGHEST)
    dt_x = jnp.dot(dt, expand, preferred_element_type=F32, precision=HIGHEST)
    xdt = xs * dt_x
    last_x = cs_x[blk - 1:blk, :]
    x_in = (xdt * jnp.exp(last_x - cs_x)).astype(BF16)
    x_b = xdt.astype(BF16)
    grow = jnp.exp(cs_x)

    for g in range(groups):
        bg = bmat[:, g * n_state:(g + 1) * n_state]
        cg = cmat[:, g * n_state:(g + 1) * n_state]
        cb = lax.dot_general(cg, bg, (((1,), (1,)), ((), ())), preferred_element_type=F32)
        st = state_ref[g]
        y_off = jnp.dot(cg, st.astype(BF16), preferred_element_type=F32)
        y_ref[:, g * gw:(g + 1) * gw] = y_off * grow[:, g * gw:(g + 1) * gw]
        for r in range(hpg):
            h = g * hpg + r
            seg = cs[:, h:h + 1] - cs_t[h:h + 1, :]
            dec = jnp.exp(jnp.where(causal, seg, -jnp.inf))
            mh = (cb * dec).astype(BF16)
            yh = jnp.dot(mh, x_b[:, h * head_dim:(h + 1) * head_dim], preferred_element_type=F32)
            y_ref[:, h * head_dim:(h + 1) * head_dim] += yh
        new = lax.dot_general(bg, x_in[:, g * gw:(g + 1) * gw], (((0,), (0,)), ((), ())),
                              preferred_element_type=F32)
        state_ref[g] = st * jnp.exp(last_x[:, g * gw:(g + 1) * gw]) + new

    y = y_ref[...] + dskip_ref[...] * xs
    y = y * _silu(z_ref[...])
    for g in range(groups):
        yg = y[:, g * gw:(g + 1) * gw]
        ms = jnp.mean(yg * yg, axis=-1, keepdims=True)
        o_ref[:, g * gw:(g + 1) * gw] = (yg * lax.rsqrt(ms + RMS_EPS)
                                         * nw_ref[:, g * gw:(g + 1) * gw]).astype(o_ref.dtype)


def ssd_mixer(zx, dt_raw, conv_w, conv_b, dt_bias, a_neg, d_skip_x, norm_w, *, batch, heads=SSD_HEADS,
              head_dim=SSD_HEAD_DIM, groups=SSD_GROUPS, n_state=SSD_STATE, blk=SSD_BLOCK):
    m = zx.shape[0]
    inner = heads * head_dim
    bc = groups * n_state
    seq = m // batch
    nblk = seq // blk
    gw = inner // groups
    kk = conv_w.shape[0]
    rows = lambda w, cb: pl.BlockSpec((blk, w), lambda b, c: (b * nblk + c, cb))
    full = lambda r, w: pl.BlockSpec((r, w), lambda b, c: (0, 0))
    body = functools.partial(_ssd_body, heads=heads, head_dim=head_dim, groups=groups, n_state=n_state)
    cw = (conv_w[:, :inner], conv_w[:, inner:inner + bc], conv_w[:, inner + bc:])
    cb = tuple(t.reshape(1, -1) for t in (conv_b[:inner], conv_b[inner:inner + bc], conv_b[inner + bc:]))
    return pl.pallas_call(
        body,
        grid=(batch, nblk),
        in_specs=[rows(inner, 0), rows(inner, 1), rows(bc, 2 * inner // bc), rows(bc, 2 * inner // bc + 1),
                  rows(dt_raw.shape[1], 0),
                  full(kk, inner), full(kk, bc), full(kk, bc), full(1, inner), full(1, bc), full(1, bc),
                  full(1, heads), full(1, heads), full(1, inner), full(1, inner)],
        out_specs=rows(inner, 0),
        out_shape=jax.ShapeDtypeStruct((m, inner), BF16),
        scratch_shapes=[pltpu.VMEM((groups, n_state, gw), F32), pltpu.VMEM((blk, inner), F32),
                        pltpu.VMEM((blk + SUBLANES, inner), F32), pltpu.VMEM((blk + SUBLANES, bc), F32),
                        pltpu.VMEM((blk + SUBLANES, bc), F32)],
        compiler_params=_cparams(("parallel", "arbitrary")),
        name="ssd_mixer",
    )(zx, zx, zx, zx, dt_raw, *cw, *cb, dt_bias.reshape(1, heads), a_neg.reshape(1, heads),
      d_skip_x.reshape(1, inner), norm_w.reshape(1, inner))


def _attn_body(lam_ref, q_ref, k_ref, v_ref, nw_ref, o_ref, acc_ref, *, post_scale, chunk, hd):
    t = q_ref.shape[0]
    hps = q_ref.shape[1] // hd
    half = hd // 2
    i = pl.program_id(2)
    sub = lax.broadcasted_iota(jnp.int32, (hd, t), 0)
    qs_t = []
    for hh in range(hps):
        qt = q_ref[:, hh * hd:(hh + 1) * hd].astype(F32).T
        qs_t.append(jnp.concatenate([jnp.where(sub < half, qt, 0.0), jnp.where(sub >= half, qt, 0.0)],
                                    axis=1).astype(BF16))
    acc_ref[...] = jnp.zeros_like(acc_ref)

    def scores(j, hh):
        start = pl.multiple_of(j * t, t)
        kb = k_ref[pl.ds(start, t), hh * hd:(hh + 1) * hd]
        return jnp.dot(kb, qs_t[hh], preferred_element_type=F32)

    def step(j, carry, last):
        start = pl.multiple_of(j * t, t)
        out = []
        for hh in range(hps):
            m, l, s = carry[hh]
            s_next = s if last else scores(j + 1, hh)
            vb = v_ref[pl.ds(start, t), hh * hd:(hh + 1) * hd]
            if last:
                kk = lax.broadcasted_iota(jnp.int32, (t, 2 * t), 0)
                qq = lax.broadcasted_iota(jnp.int32, (t, 2 * t), 1)
                qq = jnp.where(qq >= t, qq - t, qq)
                s = jnp.where(kk // chunk <= qq // chunk, s, -jnp.inf)
            m_new = jnp.maximum(m, jnp.max(s, axis=0, keepdims=True))
            alpha = jnp.exp2(m - m_new)
            p = jnp.exp2(s - m_new)
            l = alpha * l + jnp.sum(p, axis=0, keepdims=True)
            pv = lax.dot_general(vb, p.astype(BF16), (((0,), (0,)), ((), ())),
                                 preferred_element_type=F32)
            acc_ref[hh] = alpha * acc_ref[hh] + pv
            out.append((m_new, l, s_next))
        return tuple(out)

    init = tuple((jnp.full((1, 2 * t), -jnp.inf, F32), jnp.zeros((1, 2 * t), F32), scores(0, hh))
                 for hh in range(hps))
    carry = lax.fori_loop(0, i, lambda j, c: step(j, c, False), init)
    carry = step(i, carry, True)
    for hh in range(hps):
        o = acc_ref[hh] * (1.0 / carry[hh][1])
        out = o[:, :t] - lam_ref[0] * o[:, t:]
        ms = jnp.mean(out * out, axis=0, keepdims=True)
        out = out * lax.rsqrt(ms + RMS_EPS) * (nw_ref[...] * post_scale)
        o_ref[:, hh * hd:(hh + 1) * hd] = out.T.astype(o_ref.dtype)


def diff_attention(q, k, v, lam, norm_w, post_scale, *, batch, heads=DIFF_HEADS, blk=ATTN_BLOCK,
                   chunk=CHUNK, hps=ATTN_HEADS_PER_STEP):
    m, width = q.shape
    hd = width // heads
    seq = m // batch
    nq = seq // blk
    q3, k3, v3 = (t.reshape(batch, seq, width) for t in (q, k, v))
    body = functools.partial(_attn_body, post_scale=post_scale, chunk=chunk, hd=hd)
    out = pl.pallas_call(
        body,
        grid=(batch, heads // hps, nq),
        in_specs=[pl.BlockSpec(memory_space=pltpu.SMEM),
                  pl.BlockSpec((None, blk, hps * hd), lambda b, h, i: (b, i, h)),
                  pl.BlockSpec((None, seq, hps * hd), lambda b, h, i: (b, 0, h)),
                  pl.BlockSpec((None, seq, hps * hd), lambda b, h, i: (b, 0, h)),
                  pl.BlockSpec((hd, 1), lambda b, h, i: (0, 0))],
        out_specs=pl.BlockSpec((None, blk, hps * hd), lambda b, h, i: (b, i, h)),
        out_shape=jax.ShapeDtypeStruct((batch, seq, width), BF16),
        scratch_shapes=[pltpu.VMEM((hps, hd, 2 * blk), F32)],
        compiler_params=_cparams(("parallel", "parallel", "arbitrary")),
        name="diff_attention",
    )(lam.reshape(1).astype(F32), q3, k3, v3, norm_w.reshape(hd, 1))
    return out.reshape(m, width)


def _ret_body(q_ref, k_ref, v_ref, g_ref, dmat_ref, qd_ref, kd_ref, bd_ref, o_ref, state_ref):
    hps = dmat_ref.shape[0]
    dk = q_ref.shape[1] // hps
    dv = v_ref.shape[1] // hps

    @pl.when(pl.program_id(2) == 0)
    def _():
        state_ref[...] = jnp.zeros_like(state_ref)

    for hh in range(hps):
        qb = q_ref[:, hh * dk:(hh + 1) * dk].astype(BF16)
        k = k_ref[:, hh * dk:(hh + 1) * dk]
        vb = v_ref[:, hh * dv:(hh + 1) * dv].astype(BF16)
        s = lax.dot_general(qb, k.astype(BF16), (((1,), (1,)), ((), ())), preferred_element_type=F32)
        y = jnp.dot((s * dmat_ref[hh]).astype(BF16), vb, preferred_element_type=F32)
        st = state_ref[hh]
        y = y + jnp.dot(qb, st.astype(BF16), preferred_element_type=F32) * qd_ref[hh]
        kdec = (k.astype(F32) * kd_ref[hh]).astype(BF16)
        state_ref[hh] = st * bd_ref[hh] + lax.dot_general(kdec, vb, (((0,), (0,)), ((), ())),
                                                          preferred_element_type=F32)
        mu = jnp.mean(y, axis=-1, keepdims=True)
        yc = y - mu
        var = jnp.mean(yc * yc, axis=-1, keepdims=True)
        gate = _silu(g_ref[:, hh * dv:(hh + 1) * dv].astype(F32))
        o_ref[:, hh * dv:(hh + 1) * dv] = (gate * (yc * lax.rsqrt(var + LN_EPS))).astype(o_ref.dtype)


def _retention_tables(heads, blk, chunk):
    log_g = jnp.log1p(-(2.0 ** (-5.0 - jnp.arange(heads, dtype=F32))))
    idx = jnp.arange(blk, dtype=F32)
    n, mm = idx[:, None], idx[None, :]
    cn, cm = jnp.floor(n / chunk), jnp.floor(mm / chunk)
    dist = jnp.where(cn == cm, jnp.abs(n - mm), n - mm)
    dmat = jnp.where((cm <= cn)[None], jnp.exp(dist[None] * log_g[:, None, None]), 0.0)
    qd = jnp.exp((idx + 1.0)[None, :, None] * log_g[:, None, None])
    kd = jnp.exp((blk - 1.0 - idx)[None, :, None] * log_g[:, None, None])
    bd = jnp.exp(blk * log_g)[:, None, None]
    return dmat, qd, kd, bd


def retention_mixer(q, k, vg, *, batch, heads=RET_HEADS, blk=RET_BLOCK, chunk=CHUNK, hps=RET_HEADS_PER_STEP):
    m = q.shape[0]
    dk = q.shape[1] // heads
    dv = vg.shape[1] // (2 * heads)
    seq = m // batch
    nblk = seq // blk
    groups = heads // hps
    dmat, qd, kd, bd = _retention_tables(heads, blk, chunk)
    rows = lambda w, off: pl.BlockSpec((blk, hps * w), lambda b, h, c: (b * nblk + c, h + off))
    table = lambda r, w: pl.BlockSpec((hps, r, w), lambda b, h, c: (h, 0, 0))
    return pl.pallas_call(
        _ret_body,
        grid=(batch, groups, nblk),
        in_specs=[rows(dk, 0), rows(dk, 0), rows(dv, 0), rows(dv, groups),
                  table(blk, blk), table(blk, 1), table(blk, 1), table(1, 1)],
        out_specs=rows(dv, 0),
        out_shape=jax.ShapeDtypeStruct((m, heads * dv), BF16),
        scratch_shapes=[pltpu.VMEM((hps, dk, dv), F32)],
        compiler_params=_cparams(("parallel", "parallel", "arbitrary")),
        name="retention_mixer",
    )(q, k, vg, vg, dmat, qd, kd, bd)


def _router_body(h_ref, w_ref, idx_ref, wt_ref, *, n_experts):
    logits = jnp.dot(h_ref[...], w_ref[...], preferred_element_type=F32, precision=HIGHEST)
    lane = lax.broadcasted_iota(jnp.int32, logits.shape, 1)
    logits = jnp.where(lane < n_experts, logits, -jnp.inf)
    v1 = jnp.max(logits, axis=-1, keepdims=True)
    i1 = jnp.min(jnp.where(logits == v1, lane, LANES), axis=-1, keepdims=True)
    rest = jnp.where(lane == i1, -jnp.inf, logits)
    v2 = jnp.max(rest, axis=-1, keepdims=True)
    i2 = jnp.min(jnp.where(rest == v2, lane, LANES), axis=-1, keepdims=True)
    e2 = jnp.exp(v2 - v1)
    w1 = 1.0 / (1.0 + e2)
    w2 = e2 / (1.0 + e2)
    idx_ref[...] = jnp.where(lane == 0, i1, jnp.where(lane == 1, i2, 0))
    wt_ref[...] = jnp.where(lane == 0, w1, jnp.where(lane == 1, w2, 0.0))


def moe_router(h, w_router, tm=512):
    m, d = h.shape
    n_experts = w_router.shape[1]
    w_pad = jnp.zeros((d, LANES), F32).at[:, :n_experts].set(w_router)
    row = pl.BlockSpec((tm, LANES), lambda i: (i, 0))
    idx, wt = pl.pallas_call(
        functools.partial(_router_body, n_experts=n_experts),
        grid=(m // tm,),
        in_specs=[pl.BlockSpec((tm, d), lambda i: (i, 0)), pl.BlockSpec((d, LANES), lambda i: (0, 0))],
        out_specs=[row, row],
        out_shape=[jax.ShapeDtypeStruct((m, LANES), jnp.int32), jax.ShapeDtypeStruct((m, LANES), F32)],
        compiler_params=_cparams(("parallel",)),
        name="moe_router",
    )(h, w_pad)
    return idx[:, :TOP_K], wt[:, :TOP_K]


def _expert_weights(gord_ref, gexp_ref, meta_ref, w_refs, wf_refs, sem_ref, wb_ref):
    j, t = pl.program_id(0), pl.program_id(1)
    tn = wf_refs[0].shape[2]
    n_groups = meta_ref[1]
    g = gord_ref[t]

    def copies(expert, col, slot):
        start = pl.multiple_of(col * tn, tn)
        return [pltpu.make_async_copy(w_ref.at[expert, :, pl.ds(start, tn)], wf_ref.at[slot], sem_ref.at[slot])
                for w_ref, wf_ref in zip(w_refs, wf_refs)]

    @pl.when(jnp.logical_or(t == 0, g != gord_ref[jnp.maximum(t - 1, 0)]))
    def _():
        seq_no = j * n_groups + g
        slot = seq_no % 2

        @pl.when(seq_no == 0)
        def _():
            for c in copies(gexp_ref[0], 0, 0):
                c.start()

        for c in copies(0, 0, slot):
            c.wait()
        more = g + 1 < n_groups

        @pl.when(jnp.logical_or(more, j + 1 < pl.num_programs(0)))
        def _():
            for c in copies(gexp_ref[jnp.where(more, g + 1, 0)], jnp.where(more, j, j + 1), 1 - slot):
                c.start()

        for i, wf_ref in enumerate(wf_refs):
            wb_ref[:, i * tn:(i + 1) * tn] = wf_ref[slot].astype(BF16)


def _moe_up_body(gord_ref, gexp_ref, meta_ref, x_ref, wg_ref, wu_ref, o_ref, wgf_ref, wuf_ref, sem_ref, wb_ref):
    tn = o_ref.shape[1]

    @pl.when(pl.program_id(1) < meta_ref[0])
    def _():
        _expert_weights(gord_ref, gexp_ref, meta_ref, (wg_ref, wu_ref), (wgf_ref, wuf_ref), sem_ref, wb_ref)
        for rows in _row_splits(o_ref.shape[0]):
            gu = jnp.dot(x_ref[rows, :], wb_ref[...], preferred_element_type=F32)
            o_ref[rows, :] = (_silu(gu[:, :tn]) * gu[:, tn:]).astype(o_ref.dtype)

    @pl.when(pl.program_id(1) >= meta_ref[0])
    def _():
        o_ref[...] = jnp.zeros_like(o_ref)


def _moe_down_body(gord_ref, gexp_ref, meta_ref, a_ref, w_ref, rw_ref, o_ref, wf_ref, sem_ref, wb_ref):
    @pl.when(pl.program_id(1) < meta_ref[0])
    def _():
        _expert_weights(gord_ref, gexp_ref, meta_ref, (w_ref,), (wf_ref,), sem_ref, wb_ref)
        for rows in _row_splits(o_ref.shape[0]):
            y = jnp.dot(a_ref[rows, :], wb_ref[...], preferred_element_type=F32)
            o_ref[rows, :] = (y * rw_ref[rows, :]).astype(o_ref.dtype)

    @pl.when(pl.program_id(1) >= meta_ref[0])
    def _():
        o_ref[...] = jnp.zeros_like(o_ref)


def moe_experts(x_sorted, group_of_tile, group_expert, meta, row_w, w_gate, w_up, w_down, tm=MOE_TM, tn=512):
    p, d = x_sorted.shape
    f = w_gate.shape[2]
    nt = p // tm
    hbm = pl.BlockSpec(memory_space=pl.ANY)
    fetch = lambda k: pltpu.VMEM((2, k, tn), F32)
    up_spec = pltpu.PrefetchScalarGridSpec(
        num_scalar_prefetch=3,
        grid=(f // tn, nt),
        in_specs=[pl.BlockSpec((tm, d), lambda j, t, *_: (t, 0)), hbm, hbm],
        out_specs=pl.BlockSpec((tm, tn), lambda j, t, *_: (t, j)),
        scratch_shapes=[fetch(d), fetch(d), pltpu.SemaphoreType.DMA((2,)), pltpu.VMEM((d, 2 * tn), BF16)],
    )
    act = pl.pallas_call(
        _moe_up_body, grid_spec=up_spec,
        out_shape=jax.ShapeDtypeStruct((p, f), BF16),
        compiler_params=_cparams(("arbitrary", "arbitrary")),
        name="moe_up",
    )(group_of_tile, group_expert, meta, x_sorted, w_gate, w_up)
    down_spec = pltpu.PrefetchScalarGridSpec(
        num_scalar_prefetch=3,
        grid=(d // tn, nt),
        in_specs=[pl.BlockSpec((tm, f), lambda j, t, *_: (t, 0)), hbm,
                  pl.BlockSpec((tm, 1), lambda j, t, *_: (t, 0))],
        out_specs=pl.BlockSpec((tm, tn), lambda j, t, *_: (t, j)),
        scratch_shapes=[fetch(f), pltpu.SemaphoreType.DMA((2,)), pltpu.VMEM((f, tn), BF16)],
    )
    return pl.pallas_call(
        _moe_down_body, grid_spec=down_spec,
        out_shape=jax.ShapeDtypeStruct((p, d), F32),
        compiler_params=_cparams(("arbitrary", "arbitrary")),
        name="moe_down",
    )(group_of_tile, group_expert, meta, act, w_down, row_w)


def moe_routing(top_i, top_w, n_experts, tm):
    m = top_i.shape[0]
    pairs = m * TOP_K
    p = pairs + n_experts * tm
    flat_e = top_i.reshape(pairs)
    onehot = (flat_e[:, None] == jnp.arange(n_experts, dtype=jnp.int32)[None, :]).astype(jnp.int32)
    rank = jnp.sum((jnp.cumsum(onehot, axis=0) - onehot) * onehot, axis=1)
    counts = jnp.sum(onehot, axis=0)
    padded = ((counts + tm - 1) // tm) * tm
    ends = jnp.cumsum(padded)
    starts = ends - padded
    dest = starts[flat_e] + rank
    row_token = jnp.zeros((p,), jnp.int32).at[dest].set(jnp.arange(pairs, dtype=jnp.int32) // TOP_K)
    row_w = jnp.zeros((p,), F32).at[dest].set(top_w.reshape(pairs)).reshape(p, 1)
    tile_start = jnp.arange(p // tm, dtype=jnp.int32) * tm
    tile_expert = jnp.minimum(jnp.sum((tile_start[:, None] >= ends[None, :]).astype(jnp.int32), axis=1),
                              n_experts - 1).astype(jnp.int32)
    nonempty = (counts > 0).astype(jnp.int32)
    group_of_tile = (jnp.cumsum(nonempty) - 1)[tile_expert].astype(jnp.int32)
    group_expert = jnp.argsort(1 - nonempty, stable=True).astype(jnp.int32)
    meta = jnp.stack([ends[-1] // tm, jnp.sum(nonempty)]).astype(jnp.int32)
    return row_token, row_w, dest.reshape(m, TOP_K), group_of_tile, group_expert, meta


def _rotary_tables(inv_freq, seq, blocks, scale):
    ang = jnp.arange(seq, dtype=F32)[:, None] * inv_freq[None, :]
    cos, sin, zero = jnp.cos(ang) * scale, jnp.sin(ang) * scale, jnp.zeros_like(ang)
    rep = lambda a, b: jnp.tile(jnp.concatenate([a, b], axis=1), (1, blocks))
    return rep(cos, cos), rep(-sin, zero), rep(zero, sin), inv_freq.shape[0]


def _layer_even(hb, batch, seq, w_in, conv_w, conv_b, dt_bias, a_log, d_skip, ssd_norm_w,
                lam_q1, lam_k1, lam_q2, lam_k2, diff_norm_w, w_out, lambda_init):
    zx_cols = SSD_INNER + SSD_XBC
    qkv_lo = zx_cols + SSD_HEADS
    wt = jnp.swapaxes(w_in, 1, 2)
    zx = project_t(hb, wt, F32, 1024, 512, n=zx_cols)
    dt_raw = project_t(hb, wt, F32, 1024, LANES, row_lo=zx_cols, n=LANES)
    inv_freq = ROPE_THETA ** (-jnp.arange(0, DIFF_QK_DIM, 2, dtype=F32) / DIFF_QK_DIM)
    rot_q = _rotary_tables(inv_freq, seq, 2, DIFF_QK_DIM ** -0.5 * math.log2(math.e))
    rot_k = _rotary_tables(inv_freq, seq, 2, 1.0)
    q = project_t(hb, wt, BF16, 1024, 512, row_lo=qkv_lo, n=DIFF_WIDTH, rot=rot_q, seq=seq)
    k = project_t(hb, wt, BF16, 1024, 512, row_lo=qkv_lo + DIFF_WIDTH, n=DIFF_WIDTH, rot=rot_k, seq=seq)
    v = project_t(hb, wt, BF16, 1024, 512, row_lo=qkv_lo + 2 * DIFF_WIDTH, n=DIFF_WIDTH)
    y = ssd_mixer(zx, dt_raw, conv_w, conv_b, dt_bias, -jnp.exp(a_log), jnp.repeat(d_skip, SSD_HEAD_DIM),
                  ssd_norm_w, batch=batch)
    lam = (jnp.exp(jnp.sum(lam_q1 * lam_k1)) - jnp.exp(jnp.sum(lam_q2 * lam_k2)) + lambda_init)
    o = diff_attention(q, k, v, lam, diff_norm_w, 1.0 - lambda_init, batch=batch)
    return project([y, o], w_out.astype(BF16), F32, 512, 512)


def _layer_odd_mixer(hb, batch, seq, w_in, w_out):
    inv_freq = 1.0 / (ROPE_THETA ** jnp.linspace(0.0, 1.0, RET_QK_DIM // 2, dtype=F32))
    rot_q = _rotary_tables(inv_freq, seq, 1, 1.0)
    rot_k = _rotary_tables(inv_freq, seq, 1, RET_QK_DIM ** -0.5)
    proj = functools.partial(project_t, hb, w_in, BF16, 1024, 512, transposed=False)
    q = proj(n=RET_QK_WIDTH, rot=rot_q, seq=seq)
    k = proj(row_lo=RET_QK_WIDTH, n=RET_QK_WIDTH, rot=rot_k, seq=seq)
    vg = proj(row_lo=2 * RET_QK_WIDTH, n=2 * RET_V_WIDTH)
    o = retention_mixer(q, k, vg, batch=batch)
    return project([o], w_out.astype(BF16), F32, 512, 512)


def _moe_ln(h, w_router, w_gate, w_up, w_down, ln_g, ln_b):
    top_i, top_w = moe_router(h, w_router)
    row_token, row_w, pos, group_of_tile, group_expert, meta = moe_routing(top_i, top_w, N_EXPERTS, MOE_TM)
    x_sorted = gather_rows(h, row_token, BF16)
    y = moe_experts(x_sorted, group_of_tile, group_expert, meta, row_w, w_gate, w_up, w_down)
    return combine_deepnorm_ln(h, y, pos[:, 0], pos[:, 1], ln_g, ln_b)


def kernel(x, ab_w_in, ab_conv_w, ab_conv_b, ab_dt_bias, ab_a_log, ab_d_skip, ab_ssd_norm_w, ab_lam_q1, ab_lam_k1, ab_lam_q2, ab_lam_k2, ab_diff_norm_w, ab_w_out, ffn_w_gate, ffn_w_up, ffn_w_down, c_w_in, c_w_out, moe_w_router, moe_w_gate, moe_w_up, moe_w_down, ln_mix_g, ln_mix_b, ln_ffn_g, ln_ffn_b):
    batch, seq, d = x.shape
    m = batch * seq
    h = x.reshape(m, d)
    hb = h.astype(BF16)
    lambda_init = 0.8 - 0.6 * math.exp(-0.3 * 0)
    mix = _layer_even(hb, batch, seq, ab_w_in, ab_conv_w[0], ab_conv_b[0], ab_dt_bias[0], ab_a_log[0],
                      ab_d_skip[0], ab_ssd_norm_w[0], ab_lam_q1[0], ab_lam_k1[0], ab_lam_q2[0], ab_lam_k2[0],
                      ab_diff_norm_w[0], ab_w_out[0], lambda_init)
    h, hb = deepnorm_ln(h, mix, ln_mix_g[0], ln_mix_b[0])
    act = swiglu_up(hb, ffn_w_gate, ffn_w_up, 1024, 256)
    ff = matmul(act, ffn_w_down[0].astype(BF16), F32, 512, 256)
    h, hb = deepnorm_ln(h, ff, ln_ffn_g[0], ln_ffn_b[0])
    mix = _layer_odd_mixer(hb, batch, seq, c_w_in, c_w_out[0])
    h, _ = deepnorm_ln(h, mix, ln_mix_g[1], ln_mix_b[1])
    h = _moe_ln(h, moe_w_router[0], moe_w_gate[0], moe_w_up[0], moe_w_down[0], ln_ffn_g[1], ln_ffn_b[1])
    return h.reshape(batch, seq, d)
```

```python
import functools
import math

import jax
import jax.numpy as jnp
from jax import lax
from jax.experimental import pallas as pl
from jax.experimental.pallas import tpu as pltpu

F32 = jnp.float32
BF16 = jnp.bfloat16
HIGHEST = lax.Precision.HIGHEST

D_MODEL = 4096
DEPTH = 2
CHUNK = 64
SSD_HEADS = 64
SSD_HEAD_DIM = 64
SSD_INNER = SSD_HEADS * SSD_HEAD_DIM
SSD_GROUPS = 8
SSD_STATE = 128
SSD_CONV = 4
SSD_BC = SSD_GROUPS * SSD_STATE
SSD_XBC = SSD_INNER + 2 * SSD_BC
DIFF_HEADS = 32
DIFF_QK_DIM = 64
DIFF_V_DIM = 2 * DIFF_QK_DIM
DIFF_WIDTH = DIFF_HEADS * DIFF_V_DIM
ROPE_THETA = 10000.0
RET_HEADS = 16
RET_QK_DIM = D_MODEL // RET_HEADS
RET_V_DIM = 2 * RET_QK_DIM
RET_QK_WIDTH = RET_HEADS * RET_QK_DIM
RET_V_WIDTH = RET_HEADS * RET_V_DIM
FFN_DIM = 11008
N_EXPERTS = 8
TOP_K = 2
EXPERT_DIM = 4096
DEEPNORM_ALPHA = (2 * DEPTH) ** 0.25
LN_EPS = 1e-5
RMS_EPS = 1e-5

LANES = 128
SUBLANES = 8
VMEM_LIMIT_BYTES = 56 * 1024 * 1024

SSD_BLOCK = 128
RET_BLOCK = 256
RET_HEADS_PER_STEP = 2
ATTN_BLOCK = 256
ATTN_HEADS_PER_STEP = 4
MOE_TM = 512


def _cparams(sem):
    return pltpu.CompilerParams(dimension_semantics=sem, vmem_limit_bytes=VMEM_LIMIT_BYTES)


def _silu(x):
    return x * (1.0 / (1.0 + jnp.exp(-x)))


SPLIT_ROWS = 512


def _row_splits(tm):
    return (slice(0, tm // 2), slice(tm // 2, tm)) if tm >= SPLIT_ROWS else (slice(0, tm),)


def _rotate(acc, cos_ref, sa_ref, sb_ref, half, rows):
    tn = acc.shape[1]
    reps = tn // cos_ref.shape[1]
    tile = lambda r: jnp.tile(r[rows, :], (1, reps))
    return (acc * tile(cos_ref) + pltpu.roll(acc, tn - half, 1) * tile(sa_ref)
            + pltpu.roll(acc, half, 1) * tile(sb_ref))


REALIGN_ROWS = 512


def _proj_body(*refs):
    *a_refs, w_ref, o_ref = refs
    for rows in _row_splits(o_ref.shape[0]):
        k0 = 0
        acc = None
        for a_ref in a_refs:
            ka = a_ref.shape[1]
            part = jnp.dot(a_ref[rows, :], w_ref[k0:k0 + ka, :], preferred_element_type=F32)
            acc = part if acc is None else acc + part
            k0 += ka
        o_ref[rows, :] = acc.astype(o_ref.dtype)


def project(a_list, w, out_dtype, tm, tn):
    m = a_list[0].shape[0]
    k, n = w.shape
    assert m % tm == 0 and n % tn == 0 and sum(a.shape[1] for a in a_list) == k and w.dtype == BF16
    in_specs = [pl.BlockSpec((tm, a.shape[1]), lambda j, i: (i, 0)) for a in a_list]
    in_specs.append(pl.BlockSpec((k, tn), lambda j, i: (0, j)))
    return pl.pallas_call(
        _proj_body,
        grid=(n // tn, m // tm),
        in_specs=in_specs,
        out_specs=pl.BlockSpec((tm, tn), lambda j, i: (i, j)),
        out_shape=jax.ShapeDtypeStruct((m, n), out_dtype),
        compiler_params=_cparams(("parallel", "arbitrary")),
        name="project",
    )(*a_list, w)


def _column_tile_weights(copies, consume):
    j = pl.program_id(0)

    @pl.when(pl.program_id(1) == 0)
    def _():
        slot = j % 2

        @pl.when(j == 0)
        def _():
            for c in copies(0, 0):
                c.start()

        for c in copies(0, slot):
            c.wait()

        @pl.when(j + 1 < pl.num_programs(0))
        def _():
            for c in copies(j + 1, 1 - slot):
                c.start()

        consume(slot)


def _proj_t_body(*refs, rot_half, lo, transposed):
    a_ref, w_ref = refs[:2]
    pos = 2
    if rot_half:
        cos_ref, sa_ref, sb_ref = refs[pos:pos + 3]
        pos += 3
    o_ref, wf_ref, sem_ref, wb_ref = refs[pos:pos + 4]
    k, tn = wb_ref.shape
    lead = (0,) * (len(w_ref.shape) - 2)

    def copies(col_tile, slot):
        if transposed:
            src = w_ref.at[lead + (pl.ds(pl.multiple_of(lo + col_tile * tn, SUBLANES), tn), slice(None))]
        else:
            src = w_ref.at[lead + (slice(None), pl.ds(pl.multiple_of(lo + col_tile * tn, LANES), tn))]
        return [pltpu.make_async_copy(src, wf_ref.at[slot], sem_ref.at[slot])]

    def consume(slot):
        if transposed:
            for c0 in range(0, k, REALIGN_ROWS):
                wb_ref[c0:c0 + REALIGN_ROWS, :] = wf_ref[slot, :, c0:c0 + REALIGN_ROWS].T.astype(BF16)
        else:
            wb_ref[...] = wf_ref[slot].astype(BF16)

    _column_tile_weights(copies, consume)
    for rows in _row_splits(o_ref.shape[0]):
        acc = jnp.dot(a_ref[rows, :], wb_ref[...], preferred_element_type=F32)
        if rot_half:
            acc = _rotate(acc, cos_ref, sa_ref, sb_ref, rot_half, rows)
        o_ref[rows, :] = acc.astype(o_ref.dtype)


def project_t(a, w, out_dtype, tm, tn, *, row_lo=0, n=None, rot=None, seq=None, transposed=True):
    m, k = a.shape
    feat_axis, k_axis = (1, 2) if transposed else (2, 1)
    n = w.shape[feat_axis] - row_lo if n is None else n
    assert m % tm == 0 and n % tn == 0 and w.shape[k_axis] == k and k % REALIGN_ROWS == 0
    assert row_lo % (SUBLANES if transposed else LANES) == 0 and row_lo + n <= w.shape[feat_axis]
    in_specs = [pl.BlockSpec((tm, k), lambda j, i: (i, 0)), pl.BlockSpec(memory_space=pl.ANY)]
    args = [a, w]
    rot_half = 0
    if rot is not None:
        cos, sa, sb, rot_half = rot
        nrow = seq // tm
        tab = pl.BlockSpec((tm, cos.shape[1]), lambda j, i: (i % nrow, 0))
        in_specs += [tab, tab, tab]
        args += [cos, sa, sb]
    return pl.pallas_call(
        functools.partial(_proj_t_body, rot_half=rot_half, lo=row_lo, transposed=transposed),
        grid=(n // tn, m // tm),
        in_specs=in_specs,
        out_specs=pl.BlockSpec((tm, tn), lambda j, i: (i, j)),
        out_shape=jax.ShapeDtypeStruct((m, n), out_dtype),
        scratch_shapes=[pltpu.VMEM((2, tn, k) if transposed else (2, k, tn), F32),
                        pltpu.SemaphoreType.DMA((2,)), pltpu.VMEM((k, tn), BF16)],
        compiler_params=_cparams(("arbitrary", "arbitrary")),
        name="project_t",
    )(*args)


def _swiglu_body(a_ref, wg_ref, wu_ref, o_ref, wgf_ref, wuf_ref, sem_ref, wb_ref):
    tn = o_ref.shape[1]

    def copies(col_tile, slot):
        cols = pl.ds(pl.multiple_of(col_tile * tn, LANES), tn)
        return [pltpu.make_async_copy(w_ref.at[0, :, cols], wf_ref.at[slot], sem_ref.at[slot])
                for w_ref, wf_ref in ((wg_ref, wgf_ref), (wu_ref, wuf_ref))]

    def consume(slot):
        wb_ref[:, :tn] = wgf_ref[slot].astype(BF16)
        wb_ref[:, tn:] = wuf_ref[slot].astype(BF16)

    _column_tile_weights(copies, consume)
    for rows in _row_splits(o_ref.shape[0]):
        gu = jnp.dot(a_ref[rows, :], wb_ref[...], preferred_element_type=F32)
        o_ref[rows, :] = (_silu(gu[:, :tn]) * gu[:, tn:]).astype(o_ref.dtype)


def swiglu_up(a, wg, wu, tm, tn):
    m, k = a.shape
    n = wg.shape[2]
    assert m % tm == 0 and n % tn == 0 and wg.shape == wu.shape == (1, k, n)
    hbm = pl.BlockSpec(memory_space=pl.ANY)
    return pl.pallas_call(
        _swiglu_body,
        grid=(n // tn, m // tm),
        in_specs=[pl.BlockSpec((tm, k), lambda j, i: (i, 0)), hbm, hbm],
        out_specs=pl.BlockSpec((tm, tn), lambda j, i: (i, j)),
        out_shape=jax.ShapeDtypeStruct((m, n), BF16),
        scratch_shapes=[pltpu.VMEM((2, k, tn), F32), pltpu.VMEM((2, k, tn), F32),
                        pltpu.SemaphoreType.DMA((2,)), pltpu.VMEM((k, 2 * tn), BF16)],
        compiler_params=_cparams(("arbitrary", "arbitrary")),
        name="swiglu_up",
    )(a, wg, wu)


def _mm_body(a_ref, w_ref, o_ref):
    for rows in _row_splits(o_ref.shape[0]):
        o_ref[rows, :] = jnp.dot(a_ref[rows, :], w_ref[...], preferred_element_type=F32).astype(o_ref.dtype)


def matmul(a, w, out_dtype, tm, tn):
    m, k = a.shape
    n = w.shape[1]
    assert m % tm == 0 and n % tn == 0, (m, n, tm, tn)
    return pl.pallas_call(
        _mm_body,
        grid=(m // tm, n // tn),
        in_specs=[pl.BlockSpec((tm, k), lambda i, j: (i, 0)),
                  pl.BlockSpec((k, tn), lambda i, j: (0, j))],
        out_specs=pl.BlockSpec((tm, tn), lambda i, j: (i, j)),
        out_shape=jax.ShapeDtypeStruct((m, n), out_dtype),
        compiler_params=_cparams(("parallel", "arbitrary")),
        name="matmul",
    )(a, w)


def _ln_body(h_ref, y_ref, g_ref, b_ref, o_ref, *maybe_ob_ref):
    x = DEEPNORM_ALPHA * h_ref[...] + y_ref[...]
    mu = jnp.mean(x, axis=-1, keepdims=True)
    xc = x - mu
    var = jnp.mean(xc * xc, axis=-1, keepdims=True)
    out = xc * lax.rsqrt(var + LN_EPS) * g_ref[...] + b_ref[...]
    o_ref[...] = out
    for ob_ref in maybe_ob_ref:
        ob_ref[...] = out.astype(BF16)


def deepnorm_ln(h, y, g, b, tm=256, with_bf16=True):
    m, d = h.shape
    row = pl.BlockSpec((tm, d), lambda i: (i, 0))
    vec = pl.BlockSpec((1, d), lambda i: (0, 0))
    out_shape = [jax.ShapeDtypeStruct((m, d), F32)] + ([jax.ShapeDtypeStruct((m, d), BF16)] if with_bf16 else [])
    return pl.pallas_call(
        _ln_body,
        grid=(m // tm,),
        in_specs=[row, row, vec, vec],
        out_specs=[row] * len(out_shape),
        out_shape=out_shape,
        compiler_params=_cparams(("parallel",)),
        name="deepnorm_ln",
    )(h, y, g.reshape(1, d), b.reshape(1, d))


def _row_copy(src_ref, src_row, dst_ref, dst_row, sem):
    return pltpu.make_async_copy(src_ref.at[pl.ds(src_row, 1), :], dst_ref.at[pl.ds(dst_row, 1), :], sem)


def _start_rows(src_ref, idx_ref, base, n, dst_ref, dst_base, sem):
    def body(r, carry):
        _row_copy(src_ref, idx_ref[base + r], dst_ref, dst_base + r, sem).start()
        return carry

    lax.fori_loop(0, n, body, 0, unroll=8)


def _wait_rows(src_ref, n, dst_ref, dst_base, sem):
    def body(r, carry):
        _row_copy(src_ref, 0, dst_ref, dst_base + r, sem).wait()
        return carry

    lax.fori_loop(0, n, body, 0, unroll=8)


def _gather_body(tok_ref, src_ref, o_ref, buf_ref, sem_ref):
    rows = o_ref.shape[0]
    i = pl.program_id(0)
    slot = i % 2

    @pl.when(i == 0)
    def _():
        _start_rows(src_ref, tok_ref, 0, rows, buf_ref.at[0], 0, sem_ref.at[0])

    @pl.when(i + 1 < pl.num_programs(0))
    def _():
        _start_rows(src_ref, tok_ref, (i + 1) * rows, rows, buf_ref.at[1 - slot], 0, sem_ref.at[1 - slot])

    _wait_rows(src_ref, rows, buf_ref.at[slot], 0, sem_ref.at[slot])
    o_ref[...] = buf_ref[slot].astype(o_ref.dtype)


def gather_rows(src, row_idx, out_dtype, rows=256):
    p = row_idx.shape[0]
    d = src.shape[1]
    assert p % rows == 0
    spec = pltpu.PrefetchScalarGridSpec(
        num_scalar_prefetch=1,
        grid=(p // rows,),
        in_specs=[pl.BlockSpec(memory_space=pl.ANY)],
        out_specs=pl.BlockSpec((rows, d), lambda i, tok: (i, 0)),
        scratch_shapes=[pltpu.VMEM((2, rows, d), src.dtype), pltpu.SemaphoreType.DMA((2,))],
    )
    return pl.pallas_call(
        _gather_body, grid_spec=spec,
        out_shape=jax.ShapeDtypeStruct((p, d), out_dtype),
        compiler_params=_cparams(("arbitrary",)),
        name="gather_rows",
    )(row_idx, src)


def _combine_ln_body(p0_ref, p1_ref, h_ref, y_ref, g_ref, b_ref, o_ref, buf_ref, sem_ref):
    rows = h_ref.shape[0]
    i = pl.program_id(0)
    slot = i % 2

    def start(step, s):
        _start_rows(y_ref, p0_ref, step * rows, rows, buf_ref.at[s], 0, sem_ref.at[s])
        _start_rows(y_ref, p1_ref, step * rows, rows, buf_ref.at[s], rows, sem_ref.at[s])

    @pl.when(i == 0)
    def _():
        start(0, 0)

    @pl.when(i + 1 < pl.num_programs(0))
    def _():
        start(i + 1, 1 - slot)

    _wait_rows(y_ref, 2 * rows, buf_ref.at[slot], 0, sem_ref.at[slot])
    x = DEEPNORM_ALPHA * h_ref[...] + (buf_ref[slot, 0:rows, :] + buf_ref[slot, rows:2 * rows, :])
    mu = jnp.mean(x, axis=-1, keepdims=True)
    xc = x - mu
    var = jnp.mean(xc * xc, axis=-1, keepdims=True)
    o_ref[...] = xc * lax.rsqrt(var + LN_EPS) * g_ref[...] + b_ref[...]


def combine_deepnorm_ln(h, y, pos0, pos1, g, b, rows=128):
    m, d = h.shape
    assert m % rows == 0
    row = pl.BlockSpec((rows, d), lambda i, p0, p1: (i, 0))
    vec = pl.BlockSpec((1, d), lambda i, p0, p1: (0, 0))
    spec = pltpu.PrefetchScalarGridSpec(
        num_scalar_prefetch=2,
        grid=(m // rows,),
        in_specs=[row, pl.BlockSpec(memory_space=pl.ANY), vec, vec],
        out_specs=row,
        scratch_shapes=[pltpu.VMEM((2, 2 * rows, d), y.dtype), pltpu.SemaphoreType.DMA((2,))],
    )
    return pl.pallas_call(
        _combine_ln_body, grid_spec=spec,
        out_shape=jax.ShapeDtypeStruct((m, d), F32),
        compiler_params=_cparams(("arbitrary",)),
        name="combine_deepnorm_ln",
    )(pos0, pos1, h, y, g.reshape(1, d), b.reshape(1, d))


def _conv_silu(raw_ref, stage_ref, w_ref, b_ref, first):
    blk = raw_ref.shape[0]
    kk = w_ref.shape[0]

    @pl.when(first)
    def _():
        stage_ref[0:SUBLANES, :] = jnp.zeros((SUBLANES, stage_ref.shape[1]), F32)

    stage_ref[SUBLANES:SUBLANES + blk, :] = raw_ref[...]
    y = b_ref[...]
    for i in range(kk):
        y = y + w_ref[i:i + 1, :] * stage_ref[pl.ds(SUBLANES - (kk - 1) + i, blk), :]
    stage_ref[0:SUBLANES, :] = stage_ref[blk:blk + SUBLANES, :]
    return _silu(y)


def _ssd_body(z_ref, xr_ref, br_ref, cr_ref, dtr_ref, cwx_ref, cwb_ref, cwc_ref, cbx_ref, cbb_ref, cbc_ref,
              dtb_ref, aneg_ref, dskip_ref, nw_ref, o_ref,
              state_ref, y_ref, sx_ref, sb_ref, sc_ref, *, heads, head_dim, groups, n_state):
    blk = xr_ref.shape[0]
    hpg = heads // groups
    gw = hpg * head_dim
    inner = heads * head_dim
    first = pl.program_id(1) == 0

    @pl.when(first)
    def _():
        state_ref[...] = jnp.zeros_like(state_ref)

    xs = _conv_silu(xr_ref, sx_ref, cwx_ref, cbx_ref, first)
    bmat = _conv_silu(br_ref, sb_ref, cwb_ref, cbb_ref, first).astype(BF16)
    cmat = _conv_silu(cr_ref, sc_ref, cwc_ref, cbc_ref, first).astype(BF16)

    row = lax.broadcasted_iota(jnp.int32, (blk, blk), 0)
    col = lax.broadcasted_iota(jnp.int32, (blk, blk), 1)
    causal = row >= col
    eh = lax.broadcasted_iota(jnp.int32, (heads, inner), 0)
    ec = lax.broadcasted_iota(jnp.int32, (heads, inner), 1)
    expand = (ec // head_dim == eh).astype(BF16)

    def per_head_to_columns(v):
        hi = v.astype(BF16)
        rest = v - hi.astype(F32)
        mid = rest.astype(BF16)
        low = (rest - mid.astype(F32)).astype(BF16)
        parts = [jnp.dot(piece, expand, preferred_element_type=F32) for piece in (hi, mid, low)]
        return (parts[0] + parts[1]) + parts[2]

    dtp = dtr_ref[:, :heads] + dtb_ref[...]
    dt = jnp.maximum(dtp, 0.0) + jnp.log(1.0 + jnp.exp(-jnp.abs(dtp)))
    a = dt * aneg_ref[...]
    cs = jnp.dot(causal.astype(F32), a, preferred_element_type=F32, precision=HIGHEST)
    cs_t = cs.T
    cs_x = per_head_to_columns(cs)
    dt_x = per_head_to_columns(dt)
    xdt = xs * dt_x
    last_x = cs_x[blk - 1:blk, :]
    x_in = (xdt * jnp.exp(last_x - cs_x)).astype(BF16)
    x_b = xdt.astype(BF16)
    grow = jnp.exp(cs_x)

    for g in range(groups):
        bg = bmat[:, g * n_state:(g + 1) * n_state]
        cg = cmat[:, g * n_state:(g + 1) * n_state]
        cb = lax.dot_general(cg, bg, (((1,), (1,)), ((), ())), preferred_element_type=F32)
        st = state_ref[g]
        y_off = jnp.dot(cg, st.astype(BF16), preferred_element_type=F32)
        y_ref[:, g * gw:(g + 1) * gw] = y_off * grow[:, g * gw:(g + 1) * gw]
        for r in range(hpg):
            h = g * hpg + r
            seg = cs[:, h:h + 1] - cs_t[h:h + 1, :]
            dec = jnp.exp(jnp.where(causal, seg, -jnp.inf))
            mh = (cb * dec).astype(BF16)
            yh = jnp.dot(mh, x_b[:, h * head_dim:(h + 1) * head_dim], preferred_element_type=F32)
            y_ref[:, h * head_dim:(h + 1) * head_dim] += yh
        new = lax.dot_general(bg, x_in[:, g * gw:(g + 1) * gw], (((0,), (0,)), ((), ())),
                              preferred_element_type=F32)
        state_ref[g] = st * jnp.exp(last_x[:, g * gw:(g + 1) * gw]) + new

    y = y_ref[...] + dskip_ref[...] * xs
    y = y * _silu(z_ref[...])
    for g in range(groups):
        yg = y[:, g * gw:(g + 1) * gw]
        ms = jnp.mean(yg * yg, axis=-1, keepdims=True)
        o_ref[:, g * gw:(g + 1) * gw] = (yg * lax.rsqrt(ms + RMS_EPS)
                                         * nw_ref[:, g * gw:(g + 1) * gw]).astype(o_ref.dtype)


def ssd_mixer(zx, dt_raw, conv_w, conv_b, dt_bias, a_neg, d_skip_x, norm_w, *, batch, heads=SSD_HEADS,
              head_dim=SSD_HEAD_DIM, groups=SSD_GROUPS, n_state=SSD_STATE, blk=SSD_BLOCK):
    m = zx.shape[0]
    inner = heads * head_dim
    bc = groups * n_state
    seq = m // batch
    nblk = seq // blk
    gw = inner // groups
    kk = conv_w.shape[0]
    rows = lambda w, cb: pl.BlockSpec((blk, w), lambda b, c: (b * nblk + c, cb))
    full = lambda r, w: pl.BlockSpec((r, w), lambda b, c: (0, 0))
    body = functools.partial(_ssd_body, heads=heads, head_dim=head_dim, groups=groups, n_state=n_state)
    cw = (conv_w[:, :inner], conv_w[:, inner:inner + bc], conv_w[:, inner + bc:])
    cb = tuple(t.reshape(1, -1) for t in (conv_b[:inner], conv_b[inner:inner + bc], conv_b[inner + bc:]))
    return pl.pallas_call(
        body,
        grid=(batch, nblk),
        in_specs=[rows(inner, 0), rows(inner, 1), rows(bc, 2 * inner // bc), rows(bc, 2 * inner // bc + 1),
                  rows(dt_raw.shape[1], 0),
                  full(kk, inner), full(kk, bc), full(kk, bc), full(1, inner), full(1, bc), full(1, bc),
                  full(1, heads), full(1, heads), full(1, inner), full(1, inner)],
        out_specs=rows(inner, 0),
        out_shape=jax.ShapeDtypeStruct((m, inner), BF16),
        scratch_shapes=[pltpu.VMEM((groups, n_state, gw), F32), pltpu.VMEM((blk, inner), F32),
                        pltpu.VMEM((blk + SUBLANES, inner), F32), pltpu.VMEM((blk + SUBLANES, bc), F32),
                        pltpu.VMEM((blk + SUBLANES, bc), F32)],
        compiler_params=_cparams(("parallel", "arbitrary")),
        name="ssd_mixer",
    )(zx, zx, zx, zx, dt_raw, *cw, *cb, dt_bias.reshape(1, heads), a_neg.reshape(1, heads),
      d_skip_x.reshape(1, inner), norm_w.reshape(1, inner))


def _attn_body(lam_ref, q_ref, k_ref, v_ref, nw_ref, o_ref, acc_ref, *, post_scale, chunk, hd):
    t = q_ref.shape[0]
    hps = q_ref.shape[1] // hd
    half = hd // 2
    i = pl.program_id(2)
    sub = lax.broadcasted_iota(jnp.int32, (hd, t), 0)
    qs_t = []
    for hh in range(hps):
        qt = q_ref[:, hh * hd:(hh + 1) * hd].astype(F32).T
        qs_t.append(jnp.concatenate([jnp.where(sub < half, qt, 0.0), jnp.where(sub >= half, qt, 0.0)],
                                    axis=1).astype(BF16))
    acc_ref[...] = jnp.zeros_like(acc_ref)

    def scores(j, hh):
        start = pl.multiple_of(j * t, t)
        kb = k_ref[pl.ds(start, t), hh * hd:(hh + 1) * hd]
        return jnp.dot(kb, qs_t[hh], preferred_element_type=F32)

    def step(j, carry, last):
        start = pl.multiple_of(j * t, t)
        out = []
        for hh in range(hps):
            m, l, s = carry[hh]
            s_next = s if last else scores(j + 1, hh)
            vb = v_ref[pl.ds(start, t), hh * hd:(hh + 1) * hd]
            if last:
                kk = lax.broadcasted_iota(jnp.int32, (t, 2 * t), 0)
                qq = lax.broadcasted_iota(jnp.int32, (t, 2 * t), 1)
                qq = jnp.where(qq >= t, qq - t, qq)
                s = jnp.where(kk // chunk <= qq // chunk, s, -jnp.inf)
            m_new = jnp.maximum(m, jnp.max(s, axis=0, keepdims=True))
            alpha = jnp.exp2(m - m_new)
            p = jnp.exp2(s - m_new)
            l = alpha * l + jnp.sum(p, axis=0, keepdims=True)
            pv = lax.dot_general(vb, p.astype(BF16), (((0,), (0,)), ((), ())),
                                 preferred_element_type=F32)
            acc_ref[hh] = alpha * acc_ref[hh] + pv
            out.append((m_new, l, s_next))
        return tuple(out)

    init = tuple((jnp.full((1, 2 * t), -jnp.inf, F32), jnp.zeros((1, 2 * t), F32), scores(0, hh))
                 for hh in range(hps))
    carry = lax.fori_loop(0, i, lambda j, c: step(j, c, False), init)
    carry = step(i, carry, True)
    for hh in range(hps):
        o = acc_ref[hh] * (1.0 / carry[hh][1])
        out = o[:, :t] - lam_ref[0] * o[:, t:]
        ms = jnp.mean(out * out, axis=0, keepdims=True)
        out = out * lax.rsqrt(ms + RMS_EPS) * (nw_ref[...] * post_scale)
        o_ref[:, hh * hd:(hh + 1) * hd] = out.T.astype(o_ref.dtype)


def diff_attention(q, k, v, lam, norm_w, post_scale, *, batch, heads=DIFF_HEADS, blk=ATTN_BLOCK,
                   chunk=CHUNK, hps=ATTN_HEADS_PER_STEP):
    m, width = q.shape
    hd = width // heads
    seq = m // batch
    nq = seq // blk
    q3, k3, v3 = (t.reshape(batch, seq, width) for t in (q, k, v))
    body = functools.partial(_attn_body, post_scale=post_scale, chunk=chunk, hd=hd)
    out = pl.pallas_call(
        body,
        grid=(batch, heads // hps, nq),
        in_specs=[pl.BlockSpec(memory_space=pltpu.SMEM),
                  pl.BlockSpec((None, blk, hps * hd), lambda b, h, i: (b, i, h)),
                  pl.BlockSpec((None, seq, hps * hd), lambda b, h, i: (b, 0, h)),
                  pl.BlockSpec((None, seq, hps * hd), lambda b, h, i: (b, 0, h)),
                  pl.BlockSpec((hd, 1), lambda b, h, i: (0, 0))],
        out_specs=pl.BlockSpec((None, blk, hps * hd), lambda b, h, i: (b, i, h)),
        out_shape=jax.ShapeDtypeStruct((batch, seq, width), BF16),
        scratch_shapes=[pltpu.VMEM((hps, hd, 2 * blk), F32)],
        compiler_params=_cparams(("parallel", "parallel", "arbitrary")),
        name="diff_attention",
    )(lam.reshape(1).astype(F32), q3, k3, v3, norm_w.reshape(hd, 1))
    return out.reshape(m, width)


def _ret_body(q_ref, k_ref, v_ref, g_ref, dmat_ref, qd_ref, kd_ref, bd_ref, o_ref, state_ref):
    hps = dmat_ref.shape[0]
    dk = q_ref.shape[1] // hps
    dv = v_ref.shape[1] // hps

    @pl.when(pl.program_id(2) == 0)
    def _():
        state_ref[...] = jnp.zeros_like(state_ref)

    for hh in range(hps):
        qb = q_ref[:, hh * dk:(hh + 1) * dk].astype(BF16)
        k = k_ref[:, hh * dk:(hh + 1) * dk]
        vb = v_ref[:, hh * dv:(hh + 1) * dv].astype(BF16)
        s = lax.dot_general(qb, k.astype(BF16), (((1,), (1,)), ((), ())), preferred_element_type=F32)
        y = jnp.dot((s * dmat_ref[hh]).astype(BF16), vb, preferred_element_type=F32)
        st = state_ref[hh]
        y = y + jnp.dot(qb, st.astype(BF16), preferred_element_type=F32) * qd_ref[hh]
        kdec = (k.astype(F32) * kd_ref[hh]).astype(BF16)
        state_ref[hh] = st * bd_ref[hh] + lax.dot_general(kdec, vb, (((0,), (0,)), ((), ())),
                                                          preferred_element_type=F32)
        mu = jnp.mean(y, axis=-1, keepdims=True)
        yc = y - mu
        var = jnp.mean(yc * yc, axis=-1, keepdims=True)
        gate = _silu(g_ref[:, hh * dv:(hh + 1) * dv].astype(F32))
        o_ref[:, hh * dv:(hh + 1) * dv] = (gate * (yc * lax.rsqrt(var + LN_EPS))).astype(o_ref.dtype)


def _retention_tables(heads, blk, chunk):
    log_g = jnp.log1p(-(2.0 ** (-5.0 - jnp.arange(heads, dtype=F32))))
    idx = jnp.arange(blk, dtype=F32)
    n, mm = idx[:, None], idx[None, :]
    cn, cm = jnp.floor(n / chunk), jnp.floor(mm / chunk)
    dist = jnp.where(cn == cm, jnp.abs(n - mm), n - mm)
    dmat = jnp.where((cm <= cn)[None], jnp.exp(dist[None] * log_g[:, None, None]), 0.0)
    qd = jnp.exp((idx + 1.0)[None, :, None] * log_g[:, None, None])
    kd = jnp.exp((blk - 1.0 - idx)[None, :, None] * log_g[:, None, None])
    bd = jnp.exp(blk * log_g)[:, None, None]
    return dmat, qd, kd, bd


def retention_mixer(q, k, vg, *, batch, heads=RET_HEADS, blk=RET_BLOCK, chunk=CHUNK, hps=RET_HEADS_PER_STEP):
    m = q.shape[0]
    dk = q.shape[1] // heads
    dv = vg.shape[1] // (2 * heads)
    seq = m // batch
    nblk = seq // blk
    groups = heads // hps
    dmat, qd, kd, bd = _retention_tables(heads, blk, chunk)
    rows = lambda w, off: pl.BlockSpec((blk, hps * w), lambda b, h, c: (b * nblk + c, h + off))
    table = lambda r, w: pl.BlockSpec((hps, r, w), lambda b, h, c: (h, 0, 0))
    return pl.pallas_call(
        _ret_body,
        grid=(batch, groups, nblk),
        in_specs=[rows(dk, 0), rows(dk, 0), rows(dv, 0), rows(dv, groups),
                  table(blk, blk), table(blk, 1), table(blk, 1), table(1, 1)],
        out_specs=rows(dv, 0),
        out_shape=jax.ShapeDtypeStruct((m, heads * dv), BF16),
        scratch_shapes=[pltpu.VMEM((hps, dk, dv), F32)],
        compiler_params=_cparams(("parallel", "parallel", "arbitrary")),
        name="retention_mixer",
    )(q, k, vg, vg, dmat, qd, kd, bd)


def _router_body(h_ref, w_ref, idx_ref, wt_ref, *, n_experts):
    logits = jnp.dot(h_ref[...], w_ref[...], preferred_element_type=F32, precision=HIGHEST)
    lane = lax.broadcasted_iota(jnp.int32, logits.shape, 1)
    logits = jnp.where(lane < n_experts, logits, -jnp.inf)
    v1 = jnp.max(logits, axis=-1, keepdims=True)
    i1 = jnp.min(jnp.where(logits == v1, lane, LANES), axis=-1, keepdims=True)
    rest = jnp.where(lane == i1, -jnp.inf, logits)
    v2 = jnp.max(rest, axis=-1, keepdims=True)
    i2 = jnp.min(jnp.where(rest == v2, lane, LANES), axis=-1, keepdims=True)
    e2 = jnp.exp(v2 - v1)
    w1 = 1.0 / (1.0 + e2)
    w2 = e2 / (1.0 + e2)
    idx_ref[...] = jnp.where(lane == 0, i1, jnp.where(lane == 1, i2, 0))
    wt_ref[...] = jnp.where(lane == 0, w1, jnp.where(lane == 1, w2, 0.0))


def moe_router(h, w_router, tm=512):
    m, d = h.shape
    n_experts = w_router.shape[1]
    w_pad = jnp.zeros((d, LANES), F32).at[:, :n_experts].set(w_router)
    row = pl.BlockSpec((tm, LANES), lambda i: (i, 0))
    idx, wt = pl.pallas_call(
        functools.partial(_router_body, n_experts=n_experts),
        grid=(m // tm,),
        in_specs=[pl.BlockSpec((tm, d), lambda i: (i, 0)), pl.BlockSpec((d, LANES), lambda i: (0, 0))],
        out_specs=[row, row],
        out_shape=[jax.ShapeDtypeStruct((m, LANES), jnp.int32), jax.ShapeDtypeStruct((m, LANES), F32)],
        compiler_params=_cparams(("parallel",)),
        name="moe_router",
    )(h, w_pad)
    return idx[:, :TOP_K], wt[:, :TOP_K]


def _expert_weights(gord_ref, gexp_ref, meta_ref, w_refs, wf_refs, sem_ref, wb_ref):
    j, t = pl.program_id(0), pl.program_id(1)
    tn = wf_refs[0].shape[2]
    n_groups = meta_ref[1]
    g = gord_ref[t]

    def copies(expert, col, slot):
        start = pl.multiple_of(col * tn, tn)
        return [pltpu.make_async_copy(w_ref.at[expert, :, pl.ds(start, tn)], wf_ref.at[slot], sem_ref.at[slot])
                for w_ref, wf_ref in zip(w_refs, wf_refs)]

    @pl.when(jnp.logical_or(t == 0, g != gord_ref[jnp.maximum(t - 1, 0)]))
    def _():
        seq_no = j * n_groups + g
        slot = seq_no % 2

        @pl.when(seq_no == 0)
        def _():
            for c in copies(gexp_ref[0], 0, 0):
                c.start()

        for c in copies(0, 0, slot):
            c.wait()
        more = g + 1 < n_groups

        @pl.when(jnp.logical_or(more, j + 1 < pl.num_programs(0)))
        def _():
            for c in copies(gexp_ref[jnp.where(more, g + 1, 0)], jnp.where(more, j, j + 1), 1 - slot):
                c.start()

        for i, wf_ref in enumerate(wf_refs):
            wb_ref[:, i * tn:(i + 1) * tn] = wf_ref[slot].astype(BF16)


def _moe_up_body(gord_ref, gexp_ref, meta_ref, x_ref, wg_ref, wu_ref, o_ref, wgf_ref, wuf_ref, sem_ref, wb_ref):
    tn = o_ref.shape[1]

    @pl.when(pl.program_id(1) < meta_ref[0])
    def _():
        _expert_weights(gord_ref, gexp_ref, meta_ref, (wg_ref, wu_ref), (wgf_ref, wuf_ref), sem_ref, wb_ref)
        for rows in _row_splits(o_ref.shape[0]):
            gu = jnp.dot(x_ref[rows, :], wb_ref[...], preferred_element_type=F32)
            o_ref[rows, :] = (_silu(gu[:, :tn]) * gu[:, tn:]).astype(o_ref.dtype)

    @pl.when(pl.program_id(1) >= meta_ref[0])
    def _():
        o_ref[...] = jnp.zeros_like(o_ref)


def _moe_down_body(gord_ref, gexp_ref, meta_ref, a_ref, w_ref, rw_ref, o_ref, wf_ref, sem_ref, wb_ref):
    @pl.when(pl.program_id(1) < meta_ref[0])
    def _():
        _expert_weights(gord_ref, gexp_ref, meta_ref, (w_ref,), (wf_ref,), sem_ref, wb_ref)
        for rows in _row_splits(o_ref.shape[0]):
            y = jnp.dot(a_ref[rows, :], wb_ref[...], preferred_element_type=F32)
            o_ref[rows, :] = (y * rw_ref[rows, :]).astype(o_ref.dtype)

    @pl.when(pl.program_id(1) >= meta_ref[0])
    def _():
        o_ref[...] = jnp.zeros_like(o_ref)


def moe_experts(x_sorted, group_of_tile, group_expert, meta, row_w, w_gate, w_up, w_down, tm=MOE_TM, tn=512):
    p, d = x_sorted.shape
    f = w_gate.shape[2]
    nt = p // tm
    hbm = pl.BlockSpec(memory_space=pl.ANY)
    fetch = lambda k: pltpu.VMEM((2, k, tn), F32)
    up_spec = pltpu.PrefetchScalarGridSpec(
        num_scalar_prefetch=3,
        grid=(f // tn, nt),
        in_specs=[pl.BlockSpec((tm, d), lambda j, t, *_: (t, 0)), hbm, hbm],
        out_specs=pl.BlockSpec((tm, tn), lambda j, t, *_: (t, j)),
        scratch_shapes=[fetch(d), fetch(d), pltpu.SemaphoreType.DMA((2,)), pltpu.VMEM((d, 2 * tn), BF16)],
    )
    act = pl.pallas_call(
        _moe_up_body, grid_spec=up_spec,
        out_shape=jax.ShapeDtypeStruct((p, f), BF16),
        compiler_params=_cparams(("arbitrary", "arbitrary")),
        name="moe_up",
    )(group_of_tile, group_expert, meta, x_sorted, w_gate, w_up)
    down_spec = pltpu.PrefetchScalarGridSpec(
        num_scalar_prefetch=3,
        grid=(d // tn, nt),
        in_specs=[pl.BlockSpec((tm, f), lambda j, t, *_: (t, 0)), hbm,
                  pl.BlockSpec((tm, 1), lambda j, t, *_: (t, 0))],
        out_specs=pl.BlockSpec((tm, tn), lambda j, t, *_: (t, j)),
        scratch_shapes=[fetch(f), pltpu.SemaphoreType.DMA((2,)), pltpu.VMEM((f, tn), BF16)],
    )
    return pl.pallas_call(
        _moe_down_body, grid_spec=down_spec,
        out_shape=jax.ShapeDtypeStruct((p, d), F32),
        compiler_params=_cparams(("arbitrary", "arbitrary")),
        name="moe_down",
    )(group_of_tile, group_expert, meta, act, w_down, row_w)


def moe_routing(top_i, top_w, n_experts, tm):
    m = top_i.shape[0]
    pairs = m * TOP_K
    p = pairs + n_experts * tm
    flat_e = top_i.reshape(pairs)
    onehot = (flat_e[:, None] == jnp.arange(n_experts, dtype=jnp.int32)[None, :]).astype(jnp.int32)
    rank = jnp.sum((jnp.cumsum(onehot, axis=0) - onehot) * onehot, axis=1)
    counts = jnp.sum(onehot, axis=0)
    padded = ((counts + tm - 1) // tm) * tm
    ends = jnp.cumsum(padded)
    starts = ends - padded
    dest = starts[flat_e] + rank
    row_token = jnp.zeros((p,), jnp.int32).at[dest].set(jnp.arange(pairs, dtype=jnp.int32) // TOP_K)
    row_w = jnp.zeros((p,), F32).at[dest].set(top_w.reshape(pairs)).reshape(p, 1)
    tile_start = jnp.arange(p // tm, dtype=jnp.int32) * tm
    tile_expert = jnp.minimum(jnp.sum((tile_start[:, None] >= ends[None, :]).astype(jnp.int32), axis=1),
                              n_experts - 1).astype(jnp.int32)
    nonempty = (counts > 0).astype(jnp.int32)
    group_of_tile = (jnp.cumsum(nonempty) - 1)[tile_expert].astype(jnp.int32)
    group_expert = jnp.argsort(1 - nonempty, stable=True).astype(jnp.int32)
    meta = jnp.stack([ends[-1] // tm, jnp.sum(nonempty)]).astype(jnp.int32)
    return row_token, row_w, dest.reshape(m, TOP_K), group_of_tile, group_expert, meta


def _rotary_tables(inv_freq, seq, blocks, scale):
    ang = jnp.arange(seq, dtype=F32)[:, None] * inv_freq[None, :]
    cos, sin, zero = jnp.cos(ang) * scale, jnp.sin(ang) * scale, jnp.zeros_like(ang)
    rep = lambda a, b: jnp.tile(jnp.concatenate([a, b], axis=1), (1, blocks))
    return rep(cos, cos), rep(-sin, zero), rep(zero, sin), inv_freq.shape[0]


def _layer_even(hb, batch, seq, w_in, conv_w, conv_b, dt_bias, a_log, d_skip, ssd_norm_w,
                lam_q1, lam_k1, lam_q2, lam_k2, diff_norm_w, w_out, lambda_init):
    zx_cols = SSD_INNER + SSD_XBC
    qkv_lo = zx_cols + SSD_HEADS
    wt = jnp.swapaxes(w_in, 1, 2)
    zx = project_t(hb, wt, F32, 1024, 512, n=zx_cols)
    dt_raw = project_t(hb, wt, F32, 1024, LANES, row_lo=zx_cols, n=LANES)
    inv_freq = ROPE_THETA ** (-jnp.arange(0, DIFF_QK_DIM, 2, dtype=F32) / DIFF_QK_DIM)
    rot_q = _rotary_tables(inv_freq, seq, 2, DIFF_QK_DIM ** -0.5 * math.log2(math.e))
    rot_k = _rotary_tables(inv_freq, seq, 2, 1.0)
    q = project_t(hb, wt, BF16, 1024, 512, row_lo=qkv_lo, n=DIFF_WIDTH, rot=rot_q, seq=seq)
    k = project_t(hb, wt, BF16, 1024, 512, row_lo=qkv_lo + DIFF_WIDTH, n=DIFF_WIDTH, rot=rot_k, seq=seq)
    v = project_t(hb, wt, BF16, 1024, 512, row_lo=qkv_lo + 2 * DIFF_WIDTH, n=DIFF_WIDTH)
    y = ssd_mixer(zx, dt_raw, conv_w, conv_b, dt_bias, -jnp.exp(a_log), jnp.repeat(d_skip, SSD_HEAD_DIM),
                  ssd_norm_w, batch=batch)
    lam = (jnp.exp(jnp.sum(lam_q1 * lam_k1)) - jnp.exp(jnp.sum(lam_q2 * lam_k2)) + lambda_init)
    o = diff_attention(q, k, v, lam, diff_norm_w, 1.0 - lambda_init, batch=batch)
    return project([y, o], w_out.astype(BF16), F32, 512, 512)


def _layer_odd_mixer(hb, batch, seq, w_in, w_out):
    inv_freq = 1.0 / (ROPE_THETA ** jnp.linspace(0.0, 1.0, RET_QK_DIM // 2, dtype=F32))
    rot_q = _rotary_tables(inv_freq, seq, 1, 1.0)
    rot_k = _rotary_tables(inv_freq, seq, 1, RET_QK_DIM ** -0.5)
    proj = functools.partial(project_t, hb, w_in, BF16, 1024, 512, transposed=False)
    q = proj(n=RET_QK_WIDTH, rot=rot_q, seq=seq)
    k = proj(row_lo=RET_QK_WIDTH, n=RET_QK_WIDTH, rot=rot_k, seq=seq)
    vg = proj(row_lo=2 * RET_QK_WIDTH, n=2 * RET_V_WIDTH)
    o = retention_mixer(q, k, vg, batch=batch)
    return project([o], w_out.astype(BF16), F32, 512, 512)


def _moe_ln(h, w_router, w_gate, w_up, w_down, ln_g, ln_b):
    top_i, top_w = moe_router(h, w_router)
    row_token, row_w, pos, group_of_tile, group_expert, meta = moe_routing(top_i, top_w, N_EXPERTS, MOE_TM)
    x_sorted = gather_rows(h, row_token, BF16)
    y = moe_experts(x_sorted, group_of_tile, group_expert, meta, row_w, w_gate, w_up, w_down)
    return combine_deepnorm_ln(h, y, pos[:, 0], pos[:, 1], ln_g, ln_b)


def kernel(x, ab_w_in, ab_conv_w, ab_conv_b, ab_dt_bias, ab_a_log, ab_d_skip, ab_ssd_norm_w, ab_lam_q1, ab_lam_k1, ab_lam_q2, ab_lam_k2, ab_diff_norm_w, ab_w_out, ffn_w_gate, ffn_w_up, ffn_w_down, c_w_in, c_w_out, moe_w_router, moe_w_gate, moe_w_up, moe_w_down, ln_mix_g, ln_mix_b, ln_ffn_g, ln_ffn_b):
    batch, seq, d = x.shape
    m = batch * seq
    h = x.reshape(m, d)
    hb = h.astype(BF16)
    lambda_init = 0.8 - 0.6 * math.exp(-0.3 * 0)
    mix = _layer_even(hb, batch, seq, ab_w_in, ab_conv_w[0], ab_conv_b[0], ab_dt_bias[0], ab_a_log[0],
                      ab_d_skip[0], ab_ssd_norm_w[0], ab_lam_q1[0], ab_lam_k1[0], ab_lam_q2[0], ab_lam_k2[0],
                      ab_diff_norm_w[0], ab_w_out[0], lambda_init)
    h, hb = deepnorm_ln(h, mix, ln_mix_g[0], ln_mix_b[0])
    act = swiglu_up(hb, ffn_w_gate, ffn_w_up, 1024, 256)
    ff = matmul(act, ffn_w_down[0].astype(BF16), F32, 512, 256)
    h, hb = deepnorm_ln(h, ff, ln_ffn_g[0], ln_ffn_b[0])
    mix = _layer_odd_mixer(hb, batch, seq, c_w_in, c_w_out[0])
    (h,) = deepnorm_ln(h, mix, ln_mix_g[1], ln_mix_b[1], with_bf16=False)
    h = _moe_ln(h, moe_w_router[0], moe_w_gate[0], moe_w_up[0], moe_w_down[0], ln_ffn_g[1], ln_ffn_b[1])
    return h.reshape(batch, seq, d)
```

```python
import functools
import math

import jax
import jax.numpy as jnp
from jax import lax
from jax.experimental import pallas as pl
from jax.experimental.pallas import tpu as pltpu

F32 = jnp.float32
BF16 = jnp.bfloat16
HIGHEST = lax.Precision.HIGHEST

D_MODEL = 4096
DEPTH = 2
CHUNK = 64
SSD_HEADS = 64
SSD_HEAD_DIM = 64
SSD_INNER = SSD_HEADS * SSD_HEAD_DIM
SSD_GROUPS = 8
SSD_STATE = 128
SSD_CONV = 4
SSD_BC = SSD_GROUPS * SSD_STATE
SSD_XBC = SSD_INNER + 2 * SSD_BC
DIFF_HEADS = 32
DIFF_QK_DIM = 64
DIFF_V_DIM = 2 * DIFF_QK_DIM
DIFF_WIDTH = DIFF_HEADS * DIFF_V_DIM
ROPE_THETA = 10000.0
RET_HEADS = 16
RET_QK_DIM = D_MODEL // RET_HEADS
RET_V_DIM = 2 * RET_QK_DIM
RET_QK_WIDTH = RET_HEADS * RET_QK_DIM
RET_V_WIDTH = RET_HEADS * RET_V_DIM
FFN_DIM = 11008
N_EXPERTS = 8
TOP_K = 2
EXPERT_DIM = 4096
DEEPNORM_ALPHA = (2 * DEPTH) ** 0.25
LN_EPS = 1e-5
RMS_EPS = 1e-5

LANES = 128
SUBLANES = 8
VMEM_LIMIT_BYTES = 56 * 1024 * 1024

SSD_BLOCK = 128
RET_BLOCK = 256
RET_HEADS_PER_STEP = 2
ATTN_BLOCK = 256
ATTN_HEADS_PER_STEP = 4
MOE_TM = 512


def _cparams(sem):
    return pltpu.CompilerParams(dimension_semantics=sem, vmem_limit_bytes=VMEM_LIMIT_BYTES)


def _silu(x):
    return x * (1.0 / (1.0 + jnp.exp(-x)))


SPLIT_ROWS = 512


def _row_splits(tm):
    return (slice(0, tm // 2), slice(tm // 2, tm)) if tm >= SPLIT_ROWS else (slice(0, tm),)


def _rotate(acc, cos_ref, sa_ref, sb_ref, half, rows):
    tn = acc.shape[1]
    reps = tn // cos_ref.shape[1]
    tile = lambda r: jnp.tile(r[rows, :], (1, reps))
    return (acc * tile(cos_ref) + pltpu.roll(acc, tn - half, 1) * tile(sa_ref)
            + pltpu.roll(acc, half, 1) * tile(sb_ref))


REALIGN_ROWS = 512


def _proj_body(*refs):
    *a_refs, w_ref, o_ref = refs
    for rows in _row_splits(o_ref.shape[0]):
        k0 = 0
        acc = None
        for a_ref in a_refs:
            ka = a_ref.shape[1]
            part = jnp.dot(a_ref[rows, :], w_ref[k0:k0 + ka, :], preferred_element_type=F32)
            acc = part if acc is None else acc + part
            k0 += ka
        o_ref[rows, :] = acc.astype(o_ref.dtype)


def project(a_list, w, out_dtype, tm, tn):
    m = a_list[0].shape[0]
    k, n = w.shape
    assert m % tm == 0 and n % tn == 0 and sum(a.shape[1] for a in a_list) == k and w.dtype == BF16
    in_specs = [pl.BlockSpec((tm, a.shape[1]), lambda j, i: (i, 0)) for a in a_list]
    in_specs.append(pl.BlockSpec((k, tn), lambda j, i: (0, j)))
    return pl.pallas_call(
        _proj_body,
        grid=(n // tn, m // tm),
        in_specs=in_specs,
        out_specs=pl.BlockSpec((tm, tn), lambda j, i: (i, j)),
        out_shape=jax.ShapeDtypeStruct((m, n), out_dtype),
        compiler_params=_cparams(("parallel", "arbitrary")),
        name="project",
    )(*a_list, w)


def _column_tile_weights(copies, consume):
    j = pl.program_id(0)

    @pl.when(pl.program_id(1) == 0)
    def _():
        slot = j % 2

        @pl.when(j == 0)
        def _():
            for c in copies(0, 0):
                c.start()

        for c in copies(0, slot):
            c.wait()

        @pl.when(j + 1 < pl.num_programs(0))
        def _():
            for c in copies(j + 1, 1 - slot):
                c.start()

        consume(slot)


def _proj_t_body(*refs, rot_half, lo, transposed):
    a_ref, w_ref = refs[:2]
    pos = 2
    if rot_half:
        cos_ref, sa_ref, sb_ref = refs[pos:pos + 3]
        pos += 3
    o_ref, wf_ref, sem_ref, wb_ref = refs[pos:pos + 4]
    k, tn = wb_ref.shape
    lead = (0,) * (len(w_ref.shape) - 2)

    def copies(col_tile, slot):
        if transposed:
            src = w_ref.at[lead + (pl.ds(pl.multiple_of(lo + col_tile * tn, SUBLANES), tn), slice(None))]
        else:
            src = w_ref.at[lead + (slice(None), pl.ds(pl.multiple_of(lo + col_tile * tn, LANES), tn))]
        return [pltpu.make_async_copy(src, wf_ref.at[slot], sem_ref.at[slot])]

    def consume(slot):
        if transposed:
            for c0 in range(0, k, REALIGN_ROWS):
                wb_ref[c0:c0 + REALIGN_ROWS, :] = wf_ref[slot, :, c0:c0 + REALIGN_ROWS].T.astype(BF16)
        else:
            wb_ref[...] = wf_ref[slot].astype(BF16)

    _column_tile_weights(copies, consume)
    for rows in _row_splits(o_ref.shape[0]):
        acc = jnp.dot(a_ref[rows, :], wb_ref[...], preferred_element_type=F32)
        if rot_half:
            acc = _rotate(acc, cos_ref, sa_ref, sb_ref, rot_half, rows)
        o_ref[rows, :] = acc.astype(o_ref.dtype)


def project_t(a, w, out_dtype, tm, tn, *, row_lo=0, n=None, rot=None, seq=None, transposed=True):
    m, k = a.shape
    feat_axis, k_axis = (1, 2) if transposed else (2, 1)
    n = w.shape[feat_axis] - row_lo if n is None else n
    assert m % tm == 0 and n % tn == 0 and w.shape[k_axis] == k and k % REALIGN_ROWS == 0
    assert row_lo % (SUBLANES if transposed else LANES) == 0 and row_lo + n <= w.shape[feat_axis]
    in_specs = [pl.BlockSpec((tm, k), lambda j, i: (i, 0)), pl.BlockSpec(memory_space=pl.ANY)]
    args = [a, w]
    rot_half = 0
    if rot is not None:
        cos, sa, sb, rot_half = rot
        nrow = seq // tm
        tab = pl.BlockSpec((tm, cos.shape[1]), lambda j, i: (i % nrow, 0))
        in_specs += [tab, tab, tab]
        args += [cos, sa, sb]
    return pl.pallas_call(
        functools.partial(_proj_t_body, rot_half=rot_half, lo=row_lo, transposed=transposed),
        grid=(n // tn, m // tm),
        in_specs=in_specs,
        out_specs=pl.BlockSpec((tm, tn), lambda j, i: (i, j)),
        out_shape=jax.ShapeDtypeStruct((m, n), out_dtype),
        scratch_shapes=[pltpu.VMEM((2, tn, k) if transposed else (2, k, tn), F32),
                        pltpu.SemaphoreType.DMA((2,)), pltpu.VMEM((k, tn), BF16)],
        compiler_params=_cparams(("arbitrary", "arbitrary")),
        name="project_t",
    )(*args)


def _swiglu_body(a_ref, wg_ref, wu_ref, o_ref, wgf_ref, wuf_ref, sem_ref, wb_ref):
    tn = o_ref.shape[1]

    def copies(col_tile, slot):
        cols = pl.ds(pl.multiple_of(col_tile * tn, LANES), tn)
        return [pltpu.make_async_copy(w_ref.at[0, :, cols], wf_ref.at[slot], sem_ref.at[slot])
                for w_ref, wf_ref in ((wg_ref, wgf_ref), (wu_ref, wuf_ref))]

    def consume(slot):
        wb_ref[:, :tn] = wgf_ref[slot].astype(BF16)
        wb_ref[:, tn:] = wuf_ref[slot].astype(BF16)

    _column_tile_weights(copies, consume)
    for rows in _row_splits(o_ref.shape[0]):
        gu = jnp.dot(a_ref[rows, :], wb_ref[...], preferred_element_type=F32)
        o_ref[rows, :] = (_silu(gu[:, :tn]) * gu[:, tn:]).astype(o_ref.dtype)


def swiglu_up(a, wg, wu, tm, tn):
    m, k = a.shape
    n = wg.shape[2]
    assert m % tm == 0 and n % tn == 0 and wg.shape == wu.shape == (1, k, n)
    hbm = pl.BlockSpec(memory_space=pl.ANY)
    return pl.pallas_call(
        _swiglu_body,
        grid=(n // tn, m // tm),
        in_specs=[pl.BlockSpec((tm, k), lambda j, i: (i, 0)), hbm, hbm],
        out_specs=pl.BlockSpec((tm, tn), lambda j, i: (i, j)),
        out_shape=jax.ShapeDtypeStruct((m, n), BF16),
        scratch_shapes=[pltpu.VMEM((2, k, tn), F32), pltpu.VMEM((2, k, tn), F32),
                        pltpu.SemaphoreType.DMA((2,)), pltpu.VMEM((k, 2 * tn), BF16)],
        compiler_params=_cparams(("arbitrary", "arbitrary")),
        name="swiglu_up",
    )(a, wg, wu)


def _mm_body(a_ref, w_ref, o_ref):
    for rows in _row_splits(o_ref.shape[0]):
        o_ref[rows, :] = jnp.dot(a_ref[rows, :], w_ref[...], preferred_element_type=F32).astype(o_ref.dtype)


def matmul(a, w, out_dtype, tm, tn):
    m, k = a.shape
    n = w.shape[1]
    assert m % tm == 0 and n % tn == 0, (m, n, tm, tn)
    return pl.pallas_call(
        _mm_body,
        grid=(m // tm, n // tn),
        in_specs=[pl.BlockSpec((tm, k), lambda i, j: (i, 0)),
                  pl.BlockSpec((k, tn), lambda i, j: (0, j))],
        out_specs=pl.BlockSpec((tm, tn), lambda i, j: (i, j)),
        out_shape=jax.ShapeDtypeStruct((m, n), out_dtype),
        compiler_params=_cparams(("parallel", "arbitrary")),
        name="matmul",
    )(a, w)


def _ln_body(h_ref, y_ref, g_ref, b_ref, o_ref, *maybe_ob_ref):
    x = DEEPNORM_ALPHA * h_ref[...] + y_ref[...]
    mu = jnp.mean(x, axis=-1, keepdims=True)
    xc = x - mu
    var = jnp.mean(xc * xc, axis=-1, keepdims=True)
    out = xc * lax.rsqrt(var + LN_EPS) * g_ref[...] + b_ref[...]
    o_ref[...] = out
    for ob_ref in maybe_ob_ref:
        ob_ref[...] = out.astype(BF16)


def deepnorm_ln(h, y, g, b, tm=256, with_bf16=True):
    m, d = h.shape
    row = pl.BlockSpec((tm, d), lambda i: (i, 0))
    vec = pl.BlockSpec((1, d), lambda i: (0, 0))
    out_shape = [jax.ShapeDtypeStruct((m, d), F32)] + ([jax.ShapeDtypeStruct((m, d), BF16)] if with_bf16 else [])
    return pl.pallas_call(
        _ln_body,
        grid=(m // tm,),
        in_specs=[row, row, vec, vec],
        out_specs=[row] * len(out_shape),
        out_shape=out_shape,
        compiler_params=_cparams(("parallel",)),
        name="deepnorm_ln",
    )(h, y, g.reshape(1, d), b.reshape(1, d))


ROW_DMA_UNROLL = 8


def _row_copy(src_ref, src_row, dst_ref, dst_row, sem):
    return pltpu.make_async_copy(src_ref.at[pl.ds(src_row, 1), :], dst_ref.at[pl.ds(dst_row, 1), :], sem)


def _start_rows(src_ref, idx_ref, base, n, dst_ref, dst_base, sem):
    def body(g, carry):
        for u in range(ROW_DMA_UNROLL):
            r = g * ROW_DMA_UNROLL + u
            _row_copy(src_ref, idx_ref[base + r], dst_ref, dst_base + r, sem).start(priority=u % 2)
        return carry

    lax.fori_loop(0, n // ROW_DMA_UNROLL, body, 0)


def _wait_rows(src_ref, n, dst_ref, dst_base, sem):
    def body(r, carry):
        _row_copy(src_ref, 0, dst_ref, dst_base + r, sem).wait()
        return carry

    lax.fori_loop(0, n, body, 0, unroll=8)


def _gather_body(tok_ref, src_ref, o_ref, buf_ref, sem_ref):
    rows = o_ref.shape[0]
    i = pl.program_id(0)
    slot = i % 2

    @pl.when(i == 0)
    def _():
        _start_rows(src_ref, tok_ref, 0, rows, buf_ref.at[0], 0, sem_ref.at[0])

    @pl.when(i + 1 < pl.num_programs(0))
    def _():
        _start_rows(src_ref, tok_ref, (i + 1) * rows, rows, buf_ref.at[1 - slot], 0, sem_ref.at[1 - slot])

    _wait_rows(src_ref, rows, buf_ref.at[slot], 0, sem_ref.at[slot])
    o_ref[...] = buf_ref[slot].astype(o_ref.dtype)


def gather_rows(src, row_idx, out_dtype, rows=256):
    p = row_idx.shape[0]
    d = src.shape[1]
    assert p % rows == 0
    spec = pltpu.PrefetchScalarGridSpec(
        num_scalar_prefetch=1,
        grid=(p // rows,),
        in_specs=[pl.BlockSpec(memory_space=pl.ANY)],
        out_specs=pl.BlockSpec((rows, d), lambda i, tok: (i, 0)),
        scratch_shapes=[pltpu.VMEM((2, rows, d), src.dtype), pltpu.SemaphoreType.DMA((2,))],
    )
    return pl.pallas_call(
        _gather_body, grid_spec=spec,
        out_shape=jax.ShapeDtypeStruct((p, d), out_dtype),
        compiler_params=_cparams(("arbitrary",)),
        name="gather_rows",
    )(row_idx, src)


def _combine_ln_body(p0_ref, p1_ref, h_ref, y_ref, g_ref, b_ref, o_ref, buf_ref, sem_ref):
    rows = h_ref.shape[0]
    i = pl.program_id(0)
    slot = i % 2

    def start(step, s):
        _start_rows(y_ref, p0_ref, step * rows, rows, buf_ref.at[s], 0, sem_ref.at[s])
        _start_rows(y_ref, p1_ref, step * rows, rows, buf_ref.at[s], rows, sem_ref.at[s])

    @pl.when(i == 0)
    def _():
        start(0, 0)

    @pl.when(i + 1 < pl.num_programs(0))
    def _():
        start(i + 1, 1 - slot)

    _wait_rows(y_ref, 2 * rows, buf_ref.at[slot], 0, sem_ref.at[slot])
    x = DEEPNORM_ALPHA * h_ref[...] + (buf_ref[slot, 0:rows, :] + buf_ref[slot, rows:2 * rows, :])
    mu = jnp.mean(x, axis=-1, keepdims=True)
    xc = x - mu
    var = jnp.mean(xc * xc, axis=-1, keepdims=True)
    o_ref[...] = xc * lax.rsqrt(var + LN_EPS) * g_ref[...] + b_ref[...]


def combine_deepnorm_ln(h, y, pos0, pos1, g, b, rows=128):
    m, d = h.shape
    assert m % rows == 0
    row = pl.BlockSpec((rows, d), lambda i, p0, p1: (i, 0))
    vec = pl.BlockSpec((1, d), lambda i, p0, p1: (0, 0))
    spec = pltpu.PrefetchScalarGridSpec(
        num_scalar_prefetch=2,
        grid=(m // rows,),
        in_specs=[row, pl.BlockSpec(memory_space=pl.ANY), vec, vec],
        out_specs=row,
        scratch_shapes=[pltpu.VMEM((2, 2 * rows, d), y.dtype), pltpu.SemaphoreType.DMA((2,))],
    )
    return pl.pallas_call(
        _combine_ln_body, grid_spec=spec,
        out_shape=jax.ShapeDtypeStruct((m, d), F32),
        compiler_params=_cparams(("arbitrary",)),
        name="combine_deepnorm_ln",
    )(pos0, pos1, h, y, g.reshape(1, d), b.reshape(1, d))


def _conv_silu(raw_ref, stage_ref, w_ref, b_ref, first):
    blk = raw_ref.shape[0]
    kk = w_ref.shape[0]

    @pl.when(first)
    def _():
        stage_ref[0:SUBLANES, :] = jnp.zeros((SUBLANES, stage_ref.shape[1]), F32)

    stage_ref[SUBLANES:SUBLANES + blk, :] = raw_ref[...]
    y = b_ref[...]
    for i in range(kk):
        y = y + w_ref[i:i + 1, :] * stage_ref[pl.ds(SUBLANES - (kk - 1) + i, blk), :]
    stage_ref[0:SUBLANES, :] = stage_ref[blk:blk + SUBLANES, :]
    return _silu(y)


def _ssd_body(z_ref, xr_ref, br_ref, cr_ref, dtr_ref, cwx_ref, cwb_ref, cwc_ref, cbx_ref, cbb_ref, cbc_ref,
              dtb_ref, aneg_ref, dskip_ref, nw_ref, o_ref,
              state_ref, y_ref, sx_ref, sb_ref, sc_ref, *, heads, head_dim, groups, n_state):
    blk = xr_ref.shape[0]
    hpg = heads // groups
    gw = hpg * head_dim
    inner = heads * head_dim
    first = pl.program_id(1) == 0

    @pl.when(first)
    def _():
        state_ref[...] = jnp.zeros_like(state_ref)

    xs = _conv_silu(xr_ref, sx_ref, cwx_ref, cbx_ref, first)
    bmat = _conv_silu(br_ref, sb_ref, cwb_ref, cbb_ref, first).astype(BF16)
    cmat = _conv_silu(cr_ref, sc_ref, cwc_ref, cbc_ref, first).astype(BF16)

    row = lax.broadcasted_iota(jnp.int32, (blk, blk), 0)
    col = lax.broadcasted_iota(jnp.int32, (blk, blk), 1)
    causal = row >= col
    eh = lax.broadcasted_iota(jnp.int32, (heads, inner), 0)
    ec = lax.broadcasted_iota(jnp.int32, (heads, inner), 1)
    expand = (ec // head_dim == eh).astype(BF16)

    def per_head_to_columns(v):
        hi = v.astype(BF16)
        rest = v - hi.astype(F32)
        mid = rest.astype(BF16)
        low = (rest - mid.astype(F32)).astype(BF16)
        parts = [jnp.dot(piece, expand, preferred_element_type=F32) for piece in (hi, mid, low)]
        return (parts[0] + parts[1]) + parts[2]

    dtp = dtr_ref[:, :heads] + dtb_ref[...]
    dt = jnp.maximum(dtp, 0.0) + jnp.log(1.0 + jnp.exp(-jnp.abs(dtp)))
    a = dt * aneg_ref[...]
    cs = jnp.dot(causal.astype(F32), a, preferred_element_type=F32, precision=HIGHEST)
    cs_t = cs.T
    cs_x = per_head_to_columns(cs)
    dt_x = per_head_to_columns(dt)
    xdt = xs * dt_x
    last_x = cs_x[blk - 1:blk, :]
    x_in = (xdt * jnp.exp(last_x - cs_x)).astype(BF16)
    x_b = xdt.astype(BF16)
    grow = jnp.exp(cs_x)

    for g in range(groups):
        bg = bmat[:, g * n_state:(g + 1) * n_state]
        cg = cmat[:, g * n_state:(g + 1) * n_state]
        cb = lax.dot_general(cg, bg, (((1,), (1,)), ((), ())), preferred_element_type=F32)
        st = state_ref[g]
        y_off = jnp.dot(cg, st.astype(BF16), preferred_element_type=F32)
        y_ref[:, g * gw:(g + 1) * gw] = y_off * grow[:, g * gw:(g + 1) * gw]
        for r in range(hpg):
            h = g * hpg + r
            seg = cs[:, h:h + 1] - cs_t[h:h + 1, :]
            dec = jnp.exp(jnp.where(causal, seg, -jnp.inf))
            mh = (cb * dec).astype(BF16)
            yh = jnp.dot(mh, x_b[:, h * head_dim:(h + 1) * head_dim], preferred_element_type=F32)
            y_ref[:, h * head_dim:(h + 1) * head_dim] += yh
        new = lax.dot_general(bg, x_in[:, g * gw:(g + 1) * gw], (((0,), (0,)), ((), ())),
                              preferred_element_type=F32)
        state_ref[g] = st * jnp.exp(last_x[:, g * gw:(g + 1) * gw]) + new

    y = y_ref[...] + dskip_ref[...] * xs
    y = y * _silu(z_ref[...])
    for g in range(groups):
        yg = y[:, g * gw:(g + 1) * gw]
        ms = jnp.mean(yg * yg, axis=-1, keepdims=True)
        o_ref[:, g * gw:(g + 1) * gw] = (yg * lax.rsqrt(ms + RMS_EPS)
                                         * nw_ref[:, g * gw:(g + 1) * gw]).astype(o_ref.dtype)


def ssd_mixer(zx, dt_raw, conv_w, conv_b, dt_bias, a_neg, d_skip_x, norm_w, *, batch, heads=SSD_HEADS,
              head_dim=SSD_HEAD_DIM, groups=SSD_GROUPS, n_state=SSD_STATE, blk=SSD_BLOCK):
    m = zx.shape[0]
    inner = heads * head_dim
    bc = groups * n_state
    seq = m // batch
    nblk = seq // blk
    gw = inner // groups
    kk = conv_w.shape[0]
    rows = lambda w, cb: pl.BlockSpec((blk, w), lambda b, c: (b * nblk + c, cb))
    full = lambda r, w: pl.BlockSpec((r, w), lambda b, c: (0, 0))
    body = functools.partial(_ssd_body, heads=heads, head_dim=head_dim, groups=groups, n_state=n_state)
    cw = (conv_w[:, :inner], conv_w[:, inner:inner + bc], conv_w[:, inner + bc:])
    cb = tuple(t.reshape(1, -1) for t in (conv_b[:inner], conv_b[inner:inner + bc], conv_b[inner + bc:]))
    return pl.pallas_call(
        body,
        grid=(batch, nblk),
        in_specs=[rows(inner, 0), rows(inner, 1), rows(bc, 2 * inner // bc), rows(bc, 2 * inner // bc + 1),
                  rows(dt_raw.shape[1], 0),
                  full(kk, inner), full(kk, bc), full(kk, bc), full(1, inner), full(1, bc), full(1, bc),
                  full(1, heads), full(1, heads), full(1, inner), full(1, inner)],
        out_specs=rows(inner, 0),
        out_shape=jax.ShapeDtypeStruct((m, inner), BF16),
        scratch_shapes=[pltpu.VMEM((groups, n_state, gw), F32), pltpu.VMEM((blk, inner), F32),
                        pltpu.VMEM((blk + SUBLANES, inner), F32), pltpu.VMEM((blk + SUBLANES, bc), F32),
                        pltpu.VMEM((blk + SUBLANES, bc), F32)],
        compiler_params=_cparams(("parallel", "arbitrary")),
        name="ssd_mixer",
    )(zx, zx, zx, zx, dt_raw, *cw, *cb, dt_bias.reshape(1, heads), a_neg.reshape(1, heads),
      d_skip_x.reshape(1, inner), norm_w.reshape(1, inner))


def _attn_body(lam_ref, q_ref, k_ref, v_ref, nw_ref, o_ref, acc_ref, *, post_scale, chunk, hd):
    t = q_ref.shape[0]
    hps = q_ref.shape[1] // hd
    half = hd // 2
    i = pl.program_id(2)
    sub = lax.broadcasted_iota(jnp.int32, (hd, t), 0)
    qs_t = []
    for hh in range(hps):
        qt = q_ref[:, hh * hd:(hh + 1) * hd].astype(F32).T
        qs_t.append(jnp.concatenate([jnp.where(sub < half, qt, 0.0), jnp.where(sub >= half, qt, 0.0)],
                                    axis=1).astype(BF16))
    acc_ref[...] = jnp.zeros_like(acc_ref)

    def scores(j, hh):
        start = pl.multiple_of(j * t, t)
        kb = k_ref[pl.ds(start, t), hh * hd:(hh + 1) * hd]
        return jnp.dot(kb, qs_t[hh], preferred_element_type=F32)

    def step(j, carry, last):
        start = pl.multiple_of(j * t, t)
        out = []
        for hh in range(hps):
            m, l, s = carry[hh]
            s_next = s if last else scores(j + 1, hh)
            vb = v_ref[pl.ds(start, t), hh * hd:(hh + 1) * hd]
            if last:
                kk = lax.broadcasted_iota(jnp.int32, (t, 2 * t), 0)
                qq = lax.broadcasted_iota(jnp.int32, (t, 2 * t), 1)
                qq = jnp.where(qq >= t, qq - t, qq)
                s = jnp.where(kk // chunk <= qq // chunk, s, -jnp.inf)
            m_new = jnp.maximum(m, jnp.max(s, axis=0, keepdims=True))
            alpha = jnp.exp2(m - m_new)
            p = jnp.exp2(s - m_new)
            l = alpha * l + jnp.sum(p, axis=0, keepdims=True)
            pv = lax.dot_general(vb, p.astype(BF16), (((0,), (0,)), ((), ())),
                                 preferred_element_type=F32)
            acc_ref[hh] = alpha * acc_ref[hh] + pv
            out.append((m_new, l, s_next))
        return tuple(out)

    init = tuple((jnp.full((1, 2 * t), -jnp.inf, F32), jnp.zeros((1, 2 * t), F32), scores(0, hh))
                 for hh in range(hps))
    carry = lax.fori_loop(0, i, lambda j, c: step(j, c, False), init)
    carry = step(i, carry, True)
    for hh in range(hps):
        o = acc_ref[hh] * (1.0 / carry[hh][1])
        out = o[:, :t] - lam_ref[0] * o[:, t:]
        ms = jnp.mean(out * out, axis=0, keepdims=True)
        out = out * lax.rsqrt(ms + RMS_EPS) * (nw_ref[...] * post_scale)
        o_ref[:, hh * hd:(hh + 1) * hd] = out.T.astype(o_ref.dtype)


def diff_attention(q, k, v, lam, norm_w, post_scale, *, batch, heads=DIFF_HEADS, blk=ATTN_BLOCK,
                   chunk=CHUNK, hps=ATTN_HEADS_PER_STEP):
    m, width = q.shape
    hd = width // heads
    seq = m // batch
    nq = seq // blk
    q3, k3, v3 = (t.reshape(batch, seq, width) for t in (q, k, v))
    body = functools.partial(_attn_body, post_scale=post_scale, chunk=chunk, hd=hd)
    out = pl.pallas_call(
        body,
        grid=(batch, heads // hps, nq),
        in_specs=[pl.BlockSpec(memory_space=pltpu.SMEM),
                  pl.BlockSpec((None, blk, hps * hd), lambda b, h, i: (b, i, h)),
                  pl.BlockSpec((None, seq, hps * hd), lambda b, h, i: (b, 0, h)),
                  pl.BlockSpec((None, seq, hps * hd), lambda b, h, i: (b, 0, h)),
                  pl.BlockSpec((hd, 1), lambda b, h, i: (0, 0))],
        out_specs=pl.BlockSpec((None, blk, hps * hd), lambda b, h, i: (b, i, h)),
        out_shape=jax.ShapeDtypeStruct((batch, seq, width), BF16),
        scratch_shapes=[pltpu.VMEM((hps, hd, 2 * blk), F32)],
        compiler_params=_cparams(("parallel", "parallel", "arbitrary")),
        name="diff_attention",
    )(lam.reshape(1).astype(F32), q3, k3, v3, norm_w.reshape(hd, 1))
    return out.reshape(m, width)


def _ret_body(q_ref, k_ref, v_ref, g_ref, dmat_ref, qd_ref, kd_ref, bd_ref, o_ref, state_ref):
    hps = dmat_ref.shape[0]
    dk = q_ref.shape[1] // hps
    dv = v_ref.shape[1] // hps

    @pl.when(pl.program_id(2) == 0)
    def _():
        state_ref[...] = jnp.zeros_like(state_ref)

    for hh in range(hps):
        qb = q_ref[:, hh * dk:(hh + 1) * dk].astype(BF16)
        k = k_ref[:, hh * dk:(hh + 1) * dk]
        vb = v_ref[:, hh * dv:(hh + 1) * dv].astype(BF16)
        s = lax.dot_general(qb, k.astype(BF16), (((1,), (1,)), ((), ())), preferred_element_type=F32)
        y = jnp.dot((s * dmat_ref[hh]).astype(BF16), vb, preferred_element_type=F32)
        st = state_ref[hh]
        y = y + jnp.dot(qb, st.astype(BF16), preferred_element_type=F32) * qd_ref[hh]
        kdec = (k.astype(F32) * kd_ref[hh]).astype(BF16)
        state_ref[hh] = st * bd_ref[hh] + lax.dot_general(kdec, vb, (((0,), (0,)), ((), ())),
                                                          preferred_element_type=F32)
        mu = jnp.mean(y, axis=-1, keepdims=True)
        yc = y - mu
        var = jnp.mean(yc * yc, axis=-1, keepdims=True)
        gate = _silu(g_ref[:, hh * dv:(hh + 1) * dv].astype(F32))
        o_ref[:, hh * dv:(hh + 1) * dv] = (gate * (yc * lax.rsqrt(var + LN_EPS))).astype(o_ref.dtype)


def _retention_tables(heads, blk, chunk):
    log_g = jnp.log1p(-(2.0 ** (-5.0 - jnp.arange(heads, dtype=F32))))
    idx = jnp.arange(blk, dtype=F32)
    n, mm = idx[:, None], idx[None, :]
    cn, cm = jnp.floor(n / chunk), jnp.floor(mm / chunk)
    dist = jnp.where(cn == cm, jnp.abs(n - mm), n - mm)
    dmat = jnp.where((cm <= cn)[None], jnp.exp(dist[None] * log_g[:, None, None]), 0.0)
    qd = jnp.exp((idx + 1.0)[None, :, None] * log_g[:, None, None])
    kd = jnp.exp((blk - 1.0 - idx)[None, :, None] * log_g[:, None, None])
    bd = jnp.exp(blk * log_g)[:, None, None]
    return dmat, qd, kd, bd


def retention_mixer(q, k, vg, *, batch, heads=RET_HEADS, blk=RET_BLOCK, chunk=CHUNK, hps=RET_HEADS_PER_STEP):
    m = q.shape[0]
    dk = q.shape[1] // heads
    dv = vg.shape[1] // (2 * heads)
    seq = m // batch
    nblk = seq // blk
    groups = heads // hps
    dmat, qd, kd, bd = _retention_tables(heads, blk, chunk)
    rows = lambda w, off: pl.BlockSpec((blk, hps * w), lambda b, h, c: (b * nblk + c, h + off))
    table = lambda r, w: pl.BlockSpec((hps, r, w), lambda b, h, c: (h, 0, 0))
    return pl.pallas_call(
        _ret_body,
        grid=(batch, groups, nblk),
        in_specs=[rows(dk, 0), rows(dk, 0), rows(dv, 0), rows(dv, groups),
                  table(blk, blk), table(blk, 1), table(blk, 1), table(1, 1)],
        out_specs=rows(dv, 0),
        out_shape=jax.ShapeDtypeStruct((m, heads * dv), BF16),
        scratch_shapes=[pltpu.VMEM((hps, dk, dv), F32)],
        compiler_params=_cparams(("parallel", "parallel", "arbitrary")),
        name="retention_mixer",
    )(q, k, vg, vg, dmat, qd, kd, bd)


def _router_body(h_ref, w_ref, idx_ref, wt_ref, *, n_experts):
    logits = jnp.dot(h_ref[...], w_ref[...], preferred_element_type=F32, precision=HIGHEST)
    lane = lax.broadcasted_iota(jnp.int32, logits.shape, 1)
    logits = jnp.where(lane < n_experts, logits, -jnp.inf)
    v1 = jnp.max(logits, axis=-1, keepdims=True)
    i1 = jnp.min(jnp.where(logits == v1, lane, LANES), axis=-1, keepdims=True)
    rest = jnp.where(lane == i1, -jnp.inf, logits)
    v2 = jnp.max(rest, axis=-1, keepdims=True)
    i2 = jnp.min(jnp.where(rest == v2, lane, LANES), axis=-1, keepdims=True)
    e2 = jnp.exp(v2 - v1)
    w1 = 1.0 / (1.0 + e2)
    w2 = e2 / (1.0 + e2)
    idx_ref[...] = jnp.where(lane == 0, i1, jnp.where(lane == 1, i2, 0))
    wt_ref[...] = jnp.where(lane == 0, w1, jnp.where(lane == 1, w2, 0.0))


def moe_router(h, w_router, tm=512):
    m, d = h.shape
    n_experts = w_router.shape[1]
    w_pad = jnp.zeros((d, LANES), F32).at[:, :n_experts].set(w_router)
    row = pl.BlockSpec((tm, LANES), lambda i: (i, 0))
    idx, wt = pl.pallas_call(
        functools.partial(_router_body, n_experts=n_experts),
        grid=(m // tm,),
        in_specs=[pl.BlockSpec((tm, d), lambda i: (i, 0)), pl.BlockSpec((d, LANES), lambda i: (0, 0))],
        out_specs=[row, row],
        out_shape=[jax.ShapeDtypeStruct((m, LANES), jnp.int32), jax.ShapeDtypeStruct((m, LANES), F32)],
        compiler_params=_cparams(("parallel",)),
        name="moe_router",
    )(h, w_pad)
    return idx[:, :TOP_K], wt[:, :TOP_K]


def _expert_weights(gord_ref, gexp_ref, meta_ref, w_refs, wf_refs, sem_ref, wb_ref):
    j, t = pl.program_id(0), pl.program_id(1)
    tn = wf_refs[0].shape[2]
    n_groups = meta_ref[1]
    g = gord_ref[t]

    def copies(expert, col, slot):
        start = pl.multiple_of(col * tn, tn)
        return [pltpu.make_async_copy(w_ref.at[expert, :, pl.ds(start, tn)], wf_ref.at[slot], sem_ref.at[slot])
                for w_ref, wf_ref in zip(w_refs, wf_refs)]

    @pl.when(jnp.logical_or(t == 0, g != gord_ref[jnp.maximum(t - 1, 0)]))
    def _():
        seq_no = j * n_groups + g
        slot = seq_no % 2

        @pl.when(seq_no == 0)
        def _():
            for c in copies(gexp_ref[0], 0, 0):
                c.start()

        for c in copies(0, 0, slot):
            c.wait()
        more = g + 1 < n_groups

        @pl.when(jnp.logical_or(more, j + 1 < pl.num_programs(0)))
        def _():
            for c in copies(gexp_ref[jnp.where(more, g + 1, 0)], jnp.where(more, j, j + 1), 1 - slot):
                c.start()

        for i, wf_ref in enumerate(wf_refs):
            wb_ref[:, i * tn:(i + 1) * tn] = wf_ref[slot].astype(BF16)


def _moe_up_body(gord_ref, gexp_ref, meta_ref, x_ref, wg_ref, wu_ref, o_ref, wgf_ref, wuf_ref, sem_ref, wb_ref):
    tn = o_ref.shape[1]

    @pl.when(pl.program_id(1) < meta_ref[0])
    def _():
        _expert_weights(gord_ref, gexp_ref, meta_ref, (wg_ref, wu_ref), (wgf_ref, wuf_ref), sem_ref, wb_ref)
        for rows in _row_splits(o_ref.shape[0]):
            gu = jnp.dot(x_ref[rows, :], wb_ref[...], preferred_element_type=F32)
            o_ref[rows, :] = (_silu(gu[:, :tn]) * gu[:, tn:]).astype(o_ref.dtype)

    @pl.when(pl.program_id(1) >= meta_ref[0])
    def _():
        o_ref[...] = jnp.zeros_like(o_ref)


def _moe_down_body(gord_ref, gexp_ref, meta_ref, a_ref, w_ref, rw_ref, o_ref, wf_ref, sem_ref, wb_ref):
    @pl.when(pl.program_id(1) < meta_ref[0])
    def _():
        _expert_weights(gord_ref, gexp_ref, meta_ref, (w_ref,), (wf_ref,), sem_ref, wb_ref)
        for rows in _row_splits(o_ref.shape[0]):
            y = jnp.dot(a_ref[rows, :], wb_ref[...], preferred_element_type=F32)
            o_ref[rows, :] = (y * rw_ref[rows, :]).astype(o_ref.dtype)

    @pl.when(pl.program_id(1) >= meta_ref[0])
    def _():
        o_ref[...] = jnp.zeros_like(o_ref)


def moe_experts(x_sorted, group_of_tile, group_expert, meta, row_w, w_gate, w_up, w_down, tm=MOE_TM, tn=512):
    p, d = x_sorted.shape
    f = w_gate.shape[2]
    nt = p // tm
    hbm = pl.BlockSpec(memory_space=pl.ANY)
    fetch = lambda k: pltpu.VMEM((2, k, tn), F32)
    up_spec = pltpu.PrefetchScalarGridSpec(
        num_scalar_prefetch=3,
        grid=(f // tn, nt),
        in_specs=[pl.BlockSpec((tm, d), lambda j, t, *_: (t, 0)), hbm, hbm],
        out_specs=pl.BlockSpec((tm, tn), lambda j, t, *_: (t, j)),
        scratch_shapes=[fetch(d), fetch(d), pltpu.SemaphoreType.DMA((2,)), pltpu.VMEM((d, 2 * tn), BF16)],
    )
    act = pl.pallas_call(
        _moe_up_body, grid_spec=up_spec,
        out_shape=jax.ShapeDtypeStruct((p, f), BF16),
        compiler_params=_cparams(("arbitrary", "arbitrary")),
        name="moe_up",
    )(group_of_tile, group_expert, meta, x_sorted, w_gate, w_up)
    down_spec = pltpu.PrefetchScalarGridSpec(
        num_scalar_prefetch=3,
        grid=(d // tn, nt),
        in_specs=[pl.BlockSpec((tm, f), lambda j, t, *_: (t, 0)), hbm,
                  pl.BlockSpec((tm, 1), lambda j, t, *_: (t, 0))],
        out_specs=pl.BlockSpec((tm, tn), lambda j, t, *_: (t, j)),
        scratch_shapes=[fetch(f), pltpu.SemaphoreType.DMA((2,)), pltpu.VMEM((f, tn), BF16)],
    )
    return pl.pallas_call(
        _moe_down_body, grid_spec=down_spec,
        out_shape=jax.ShapeDtypeStruct((p, d), F32),
        compiler_params=_cparams(("arbitrary", "arbitrary")),
        name="moe_down",
    )(group_of_tile, group_expert, meta, act, w_down, row_w)


def moe_routing(top_i, top_w, n_experts, tm):
    m = top_i.shape[0]
    pairs = m * TOP_K
    p = pairs + n_experts * tm
    flat_e = top_i.reshape(pairs)
    onehot = (flat_e[:, None] == jnp.arange(n_experts, dtype=jnp.int32)[None, :]).astype(jnp.int32)
    rank = jnp.sum((jnp.cumsum(onehot, axis=0) - onehot) * onehot, axis=1)
    counts = jnp.sum(onehot, axis=0)
    padded = ((counts + tm - 1) // tm) * tm
    ends = jnp.cumsum(padded)
    starts = ends - padded
    dest = starts[flat_e] + rank
    row_token = jnp.zeros((p,), jnp.int32).at[dest].set(jnp.arange(pairs, dtype=jnp.int32) // TOP_K)
    row_w = jnp.zeros((p,), F32).at[dest].set(top_w.reshape(pairs)).reshape(p, 1)
    tile_start = jnp.arange(p // tm, dtype=jnp.int32) * tm
    tile_expert = jnp.minimum(jnp.sum((tile_start[:, None] >= ends[None, :]).astype(jnp.int32), axis=1),
                              n_experts - 1).astype(jnp.int32)
    nonempty = (counts > 0).astype(jnp.int32)
    group_of_tile = (jnp.cumsum(nonempty) - 1)[tile_expert].astype(jnp.int32)
    group_expert = jnp.argsort(1 - nonempty, stable=True).astype(jnp.int32)
    meta = jnp.stack([ends[-1] // tm, jnp.sum(nonempty)]).astype(jnp.int32)
    return row_token, row_w, dest.reshape(m, TOP_K), group_of_tile, group_expert, meta


def _rotary_tables(inv_freq, seq, blocks, scale):
    ang = jnp.arange(seq, dtype=F32)[:, None] * inv_freq[None, :]
    cos, sin, zero = jnp.cos(ang) * scale, jnp.sin(ang) * scale, jnp.zeros_like(ang)
    rep = lambda a, b: jnp.tile(jnp.concatenate([a, b], axis=1), (1, blocks))
    return rep(cos, cos), rep(-sin, zero), rep(zero, sin), inv_freq.shape[0]


def _layer_even(hb, batch, seq, w_in, conv_w, conv_b, dt_bias, a_log, d_skip, ssd_norm_w,
                lam_q1, lam_k1, lam_q2, lam_k2, diff_norm_w, w_out, lambda_init):
    zx_cols = SSD_INNER + SSD_XBC
    qkv_lo = zx_cols + SSD_HEADS
    wt = jnp.swapaxes(w_in, 1, 2)
    zx = project_t(hb, wt, F32, 1024, 512, n=zx_cols)
    dt_raw = project_t(hb, wt, F32, 1024, LANES, row_lo=zx_cols, n=LANES)
    inv_freq = ROPE_THETA ** (-jnp.arange(0, DIFF_QK_DIM, 2, dtype=F32) / DIFF_QK_DIM)
    rot_q = _rotary_tables(inv_freq, seq, 2, DIFF_QK_DIM ** -0.5 * math.log2(math.e))
    rot_k = _rotary_tables(inv_freq, seq, 2, 1.0)
    q = project_t(hb, wt, BF16, 1024, 512, row_lo=qkv_lo, n=DIFF_WIDTH, rot=rot_q, seq=seq)
    k = project_t(hb, wt, BF16, 1024, 512, row_lo=qkv_lo + DIFF_WIDTH, n=DIFF_WIDTH, rot=rot_k, seq=seq)
    v = project_t(hb, wt, BF16, 1024, 512, row_lo=qkv_lo + 2 * DIFF_WIDTH, n=DIFF_WIDTH)
    y = ssd_mixer(zx, dt_raw, conv_w, conv_b, dt_bias, -jnp.exp(a_log), jnp.repeat(d_skip, SSD_HEAD_DIM),
                  ssd_norm_w, batch=batch)
    lam = (jnp.exp(jnp.sum(lam_q1 * lam_k1)) - jnp.exp(jnp.sum(lam_q2 * lam_k2)) + lambda_init)
    o = diff_attention(q, k, v, lam, diff_norm_w, 1.0 - lambda_init, batch=batch)
    return project([y, o], w_out.astype(BF16), F32, 512, 512)


def _layer_odd_mixer(hb, batch, seq, w_in, w_out):
    inv_freq = 1.0 / (ROPE_THETA ** jnp.linspace(0.0, 1.0, RET_QK_DIM // 2, dtype=F32))
    rot_q = _rotary_tables(inv_freq, seq, 1, 1.0)
    rot_k = _rotary_tables(inv_freq, seq, 1, RET_QK_DIM ** -0.5)
    proj = functools.partial(project_t, hb, w_in, BF16, 1024, 512, transposed=False)
    q = proj(n=RET_QK_WIDTH, rot=rot_q, seq=seq)
    k = proj(row_lo=RET_QK_WIDTH, n=RET_QK_WIDTH, rot=rot_k, seq=seq)
    vg = proj(row_lo=2 * RET_QK_WIDTH, n=2 * RET_V_WIDTH)
    o = retention_mixer(q, k, vg, batch=batch)
    return project([o], w_out.astype(BF16), F32, 512, 512)


def _moe_ln(h, w_router, w_gate, w_up, w_down, ln_g, ln_b):
    top_i, top_w = moe_router(h, w_router)
    row_token, row_w, pos, group_of_tile, group_expert, meta = moe_routing(top_i, top_w, N_EXPERTS, MOE_TM)
    x_sorted = gather_rows(h, row_token, BF16)
    y = moe_experts(x_sorted, group_of_tile, group_expert, meta, row_w, w_gate, w_up, w_down)
    return combine_deepnorm_ln(h, y, pos[:, 0], pos[:, 1], ln_g, ln_b)


def kernel(x, ab_w_in, ab_conv_w, ab_conv_b, ab_dt_bias, ab_a_log, ab_d_skip, ab_ssd_norm_w, ab_lam_q1, ab_lam_k1, ab_lam_q2, ab_lam_k2, ab_diff_norm_w, ab_w_out, ffn_w_gate, ffn_w_up, ffn_w_down, c_w_in, c_w_out, moe_w_router, moe_w_gate, moe_w_up, moe_w_down, ln_mix_g, ln_mix_b, ln_ffn_g, ln_ffn_b):
    batch, seq, d = x.shape
    m = batch * seq
    h = x.reshape(m, d)
    hb = h.astype(BF16)
    lambda_init = 0.8 - 0.6 * math.exp(-0.3 * 0)
    mix = _layer_even(hb, batch, seq, ab_w_in, ab_conv_w[0], ab_conv_b[0], ab_dt_bias[0], ab_a_log[0],
                      ab_d_skip[0], ab_ssd_norm_w[0], ab_lam_q1[0], ab_lam_k1[0], ab_lam_q2[0], ab_lam_k2[0],
                      ab_diff_norm_w[0], ab_w_out[0], lambda_init)
    h, hb = deepnorm_ln(h, mix, ln_mix_g[0], ln_mix_b[0])
    act = swiglu_up(hb, ffn_w_gate, ffn_w_up, 1024, 256)
    ff = matmul(act, ffn_w_down[0].astype(BF16), F32, 512, 256)
    h, hb = deepnorm_ln(h, ff, ln_ffn_g[0], ln_ffn_b[0])
    mix = _layer_odd_mixer(hb, batch, seq, c_w_in, c_w_out[0])
    (h,) = deepnorm_ln(h, mix, ln_mix_g[1], ln_mix_b[1], with_bf16=False)
    h = _moe_ln(h, moe_w_router[0], moe_w_gate[0], moe_w_up[0], moe_w_down[0], ln_ffn_g[1], ln_ffn_b[1])
    return h.reshape(batch, seq, d)
```
